```python
import math
import jax, jax.numpy as jnp
from jax import lax
import numpy as np

D_MODEL = 1024
BATCH = 16
SEQ = 256
DEPTH = 4
DEC_BATCH = 8
DEC_SEQ = 4096
PAST_LEN = 512

GRID_W = 64
N_MIXERS = 3
N_LAYERS_A = (DEPTH + 2) // 3
N_LAYERS_B = (DEPTH + 1) // 3
N_LAYERS_C = DEPTH // 3
N_LAYERS_DENSE = (DEPTH + 1) // 2
N_LAYERS_MOE = DEPTH // 2
DEEPNORM_ALPHA = (2.0 * DEPTH) ** 0.25
DEEPNORM_BETA = (8.0 * DEPTH) ** -0.25
LN_EPS = 1e-5
RMS_EPS = 1e-6

CHUNK_A = 128
GMLP_WIDTH = 2 * D_MODEL
GMLP_GROUPS = 8
GMLP_GROUP_DIM = GMLP_WIDTH // GMLP_GROUPS

MLA_HEADS = 8
MLA_Q_RANK = D_MODEL // 2
MLA_KV_RANK = D_MODEL // 4
MLA_D_NOPE = 128
MLA_D_ROPE = 64
MLA_D_V = 128
ROPE_BASE = 10000.0
Q_BLOCK = 128

GLA_HEADS = 4
GLA_DK = D_MODEL // 2 // GLA_HEADS
GLA_DV = D_MODEL // GLA_HEADS
GLA_DECAY_RANK = 16
GLA_GATE_NORM = 16.0
GLA_CHUNK = 64

D_FF = 2816
N_EXPERTS = 8
TOP_K = 2
D_FF_EXPERT = 3584

kernel_name = "hybrid_gmlp_mla_gla_diffusion_step"


def layer_norm(x, g, b):
    xf = x.astype(jnp.float32)
    mu = jnp.mean(xf, axis=-1, keepdims=True)
    var = jnp.mean(jnp.square(xf - mu), axis=-1, keepdims=True)
    return ((xf - mu) * lax.rsqrt(var + LN_EPS) * g + b).astype(x.dtype)


def rms_norm(x, g):
    xf = x.astype(jnp.float32)
    return (xf * lax.rsqrt(jnp.mean(jnp.square(xf), axis=-1, keepdims=True) + RMS_EPS) * g).astype(x.dtype)


def adaln(cond, w, b):
    m = jax.nn.silu(cond) @ w + b
    return jnp.split(m[..., None, :], 6, axis=-1)


def modulate(x, shift, scale):
    return x * (1.0 + scale) + shift


def post_norm_residual(x, out, gate, g, b):
    return layer_norm(DEEPNORM_ALPHA * x + gate * out, g, b)


def axial_rope_tables(n_tokens):
    rows = n_tokens // GRID_W
    pos_r, pos_c = jnp.meshgrid(jnp.arange(rows, dtype=jnp.float32),
                                jnp.arange(GRID_W, dtype=jnp.float32), indexing="ij")
    n_freq = MLA_D_ROPE // 4
    inv = ROPE_BASE ** (-jnp.arange(n_freq, dtype=jnp.float32) / n_freq)
    ang_r = pos_r.reshape(-1, 1, 1) * inv
    ang_c = pos_c.reshape(-1, 1, 1) * inv
    return jnp.cos(ang_r), jnp.sin(ang_r), jnp.cos(ang_c), jnp.sin(ang_c)


def _rotate(x, cos, sin):
    x1, x2 = jnp.split(x, 2, axis=-1)
    return jnp.concatenate([x1 * cos - x2 * sin, x1 * sin + x2 * cos], axis=-1)


def apply_axial_rope(x, tables):
    cos_r, sin_r, cos_c, sin_c = tables
    x_row, x_col = jnp.split(x.astype(jnp.float32), 2, axis=-1)
    out = jnp.concatenate([_rotate(x_row, cos_r, sin_r), _rotate(x_col, cos_c, sin_c)], axis=-1)
    return out.astype(x.dtype)


def attend(q, k, v):
    s = jnp.einsum("bqhd,bkhd->bhqk", q, k).astype(jnp.float32) * (q.shape[-1] ** -0.5)
    p = jax.nn.softmax(s, axis=-1).astype(v.dtype)
    return jnp.einsum("bhqk,bkhd->bqhd", p, v)


def attend_blocks(q, k, v):
    B, T, H, d = q.shape
    qb = q.reshape(B, T // Q_BLOCK, Q_BLOCK, H, d).swapaxes(0, 1)
    ob = lax.map(lambda qi: attend(qi, k, v), qb)
    return ob.swapaxes(0, 1).reshape(B, T, H, v.shape[-1])


def gmlp_mix(h, w_in, b_in, v_g, v_b, w_s, b_s, w_out):
    B, T, _ = h.shape
    z = jax.nn.gelu(h @ w_in + b_in)
    u, v = jnp.split(z, 2, axis=-1)
    v = layer_norm(v, v_g, v_b).reshape(B, T // CHUNK_A, CHUNK_A, GMLP_GROUPS, GMLP_GROUP_DIM)
    v = jnp.einsum("gij,bnjgc->bnigc", w_s, v) + b_s.T[:, :, None]
    return (u * v.reshape(B, T, GMLP_WIDTH)) @ w_out


def mla_project(h, w_dqkv, q_norm_g, kv_norm_g, w_uq):
    B, T, _ = h.shape
    c_q, c_kv, k_rope = jnp.split(h @ w_dqkv, [MLA_Q_RANK, MLA_Q_RANK + MLA_KV_RANK], axis=-1)
    q = (rms_norm(c_q, q_norm_g) @ w_uq).reshape(B, T, MLA_HEADS, MLA_D_NOPE + MLA_D_ROPE)
    return q, rms_norm(c_kv, kv_norm_g), k_rope


def mla_keys_values(c_kv, k_rope, w_ukv):
    B, L, _ = c_kv.shape
    kv = (c_kv @ w_ukv).reshape(B, L, MLA_HEADS, MLA_D_NOPE + MLA_D_V)
    k_nope, v = jnp.split(kv, [MLA_D_NOPE], axis=-1)
    k_r = jnp.broadcast_to(k_rope[:, :, None, :], (B, L, MLA_HEADS, MLA_D_ROPE))
    return jnp.concatenate([k_nope, k_r], axis=-1), v


def mla_context(h, w_dqkv, q_norm_g, kv_norm_g, w_uq, w_ukv, w_o):
    B, T, _ = h.shape
    q, c_kv, k_rope = mla_project(h, w_dqkv, q_norm_g, kv_norm_g, w_uq)
    k, v = mla_keys_values(c_kv, k_rope, w_ukv)
    o = attend_blocks(q, k, v)
    return o.reshape(B, T, MLA_HEADS * MLA_D_V) @ w_o, c_kv, k_rope


def mla_latent(h, ctx_ckv, ctx_krope, tables, w_dqkv, q_norm_g, kv_norm_g, w_uq, w_ukv, w_o):
    B, T, _ = h.shape
    q, c_kv, k_rope = mla_project(h, w_dqkv, q_norm_g, kv_norm_g, w_uq)
    q = jnp.concatenate([q[..., :MLA_D_NOPE], apply_axial_rope(q[..., MLA_D_NOPE:], tables)], axis=-1)
    k_rope = apply_axial_rope(k_rope[:, :, None, :], tables)[:, :, 0, :]
    k_lat, v_lat = mla_keys_values(c_kv, k_rope, w_ukv)
    k_ctx, v_ctx = mla_keys_values(ctx_ckv, ctx_krope, w_ukv)
    k = jnp.concatenate([k_ctx, k_lat], axis=1)
    v = jnp.concatenate([v_ctx, v_lat], axis=1)
    o = attend_blocks(q, k, v)
    return o.reshape(B, T, MLA_HEADS * MLA_D_V) @ w_o


def gla_project(h, w_in, wa1, wa2, ba):
    B, T, _ = h.shape
    dqk = GLA_HEADS * GLA_DK
    dv = GLA_HEADS * GLA_DV
    q, k, v, r = jnp.split(h @ w_in, [dqk, 2 * dqk, 2 * dqk + dv], axis=-1)
    heads = lambda a, d: a.reshape(B, T, GLA_HEADS, d)

    def log_decay(d):
        logit = ((h @ wa1[d]) @ wa2[d] + ba[d]).astype(jnp.float32)
        return heads(jax.nn.log_sigmoid(logit) / GLA_GATE_NORM, GLA_DK)

    q = heads(q, GLA_DK) * (GLA_DK ** -0.5)
    return q, heads(k, GLA_DK), heads(v, GLA_DV), heads(r, GLA_DV), log_decay(0), log_decay(1)


def gla_scan(q, k, v, g, s0):
    B, T, H, _ = q.shape
    n_chunks = T // GLA_CHUNK
    out_dtype = v.dtype
    rs = lambda a: a.astype(jnp.float32).reshape(B, n_chunks, GLA_CHUNK, H, a.shape[-1])
    q, k, v, g = rs(q), rs(k), rs(v), rs(g)
    b = jnp.cumsum(g, axis=2)
    b_last = b[:, :, -1:]
    qe = q * jnp.exp(b)
    ke = k * jnp.exp(-b)
    kd = k * jnp.exp(b_last - b)
    mask = jnp.tril(jnp.ones((GLA_CHUNK, GLA_CHUNK), dtype=bool))
    a = jnp.where(mask, jnp.einsum("bnihk,bnjhk->bnhij", qe, ke), 0.0)
    o_intra = jnp.einsum("bnhij,bnjhv->bnihv", a, v)

    def step(S, xs):
        qe_n, kd_n, v_n, dl_n = xs
        o_n = jnp.einsum("bchk,bhkv->bchv", qe_n, S)
        S = S * dl_n[..., None] + jnp.einsum("bchk,bchv->bhkv", kd_n, v_n)
        return S, o_n

    xs = (qe.swapaxes(0, 1), kd.swapaxes(0, 1), v.swapaxes(0, 1), jnp.exp(b_last[:, :, 0]).swapaxes(0, 1))
    s_fin, o_inter = lax.scan(step, s0.astype(jnp.float32), xs)
    o = o_intra + o_inter.swapaxes(0, 1)
    return o.reshape(B, T, H, GLA_DV).astype(out_dtype), s_fin.astype(out_dtype)


def gla_bidir(q, k, v, g_f, g_b, s0_f, s0_b):
    o_f, s_f = gla_scan(q, k, v, g_f, s0_f)
    flip = lambda a: jnp.flip(a, axis=1)
    o_b, s_b = gla_scan(flip(q), flip(k), flip(v), flip(g_b), s0_b)
    return o_f + flip(o_b), s_f, s_b


def gla_output(o, r, norm_g, w_o):
    B, T = o.shape[:2]
    o = rms_norm(o, norm_g) * jax.nn.silu(r)
    return o.reshape(B, T, GLA_HEADS * GLA_DV) @ w_o


def swiglu(h, w13, w2):
    a, b = jnp.split(h @ w13, 2, axis=-1)
    return (jax.nn.silu(a) * b) @ w2


def moe_swiglu(h, w_router, w13, w2):
    B, T, D = h.shape
    x = h.reshape(B * T, D)
    logits = (x @ w_router).astype(jnp.float32)
    top_v, top_i = lax.top_k(logits, TOP_K)
    top_w = jax.nn.softmax(top_v, axis=-1)
    gates = jnp.sum(jax.nn.one_hot(top_i, N_EXPERTS, dtype=jnp.float32) * top_w[..., None], axis=1)
    gates = gates.astype(x.dtype)
    y = jnp.zeros_like(x)
    for e in range(N_EXPERTS):
        y = y + gates[:, e:e + 1] * swiglu(x, w13[e], w2[e])
    return y.reshape(B, T, D)


def setup_inputs(seed: int = 0) -> dict:
    key = jax.random.key(seed)
    ks = iter(jax.random.split(key, 48))
    nrm = lambda shape, scale: jax.random.normal(next(ks), shape, jnp.float32) * scale
    gain = lambda shape: 1.0 + nrm(shape, 0.02)
    D = D_MODEL
    dqk_gla = GLA_HEADS * GLA_DK
    dv_gla = GLA_HEADS * GLA_DV
    return {
        "x_prompt": nrm((BATCH, SEQ, D), 1.0),
        "x_sample": nrm((DEC_BATCH, DEC_SEQ, D), 1.0),
        "cache_mla_ckv": nrm((DEC_BATCH, N_LAYERS_B, PAST_LEN, MLA_KV_RANK), 1.0),
        "cache_mla_krope": nrm((DEC_BATCH, N_LAYERS_B, PAST_LEN, MLA_D_ROPE), 1.0),
        "state_gla": nrm((DEC_BATCH, N_LAYERS_C, 2, GLA_HEADS, GLA_DK, GLA_DV), 1.0),
        "c": nrm((DEC_BATCH, D), 1.0),
        "c_ctx": nrm((D,), 1.0),
        "ada_w": nrm((DEPTH, D, 6 * D), 0.5 * D ** -0.5),
        "ada_b": nrm((DEPTH, 6 * D), 0.1),
        "ln_g": gain((DEPTH, 2, D)),
        "ln_b": nrm((DEPTH, 2, D), 0.02),
        "gmlp_w_in": nrm((N_LAYERS_A, D, 2 * GMLP_WIDTH), D ** -0.5),
        "gmlp_b_in": nrm((N_LAYERS_A, 2 * GMLP_WIDTH), 0.02),
        "gmlp_v_g": gain((N_LAYERS_A, GMLP_WIDTH)),
        "gmlp_v_b": nrm((N_LAYERS_A, GMLP_WIDTH), 0.02),
        "gmlp_w_s": nrm((N_LAYERS_A, GMLP_GROUPS, CHUNK_A, CHUNK_A), CHUNK_A ** -0.5),
        "gmlp_b_s": gain((N_LAYERS_A, GMLP_GROUPS, CHUNK_A)),
        "gmlp_w_out": nrm((N_LAYERS_A, GMLP_WIDTH, D), DEEPNORM_BETA * GMLP_WIDTH ** -0.5),
        "mla_w_dqkv": nrm((N_LAYERS_B, D, MLA_Q_RANK + MLA_KV_RANK + MLA_D_ROPE), D ** -0.5),
        "mla_q_norm": gain((N_LAYERS_B, MLA_Q_RANK)),
        "mla_kv_norm": gain((N_LAYERS_B, MLA_KV_RANK)),
        "mla_w_uq": nrm((N_LAYERS_B, MLA_Q_RANK, MLA_HEADS * (MLA_D_NOPE + MLA_D_ROPE)), MLA_Q_RANK ** -0.5),
        "mla_w_ukv": nrm((N_LAYERS_B, MLA_KV_RANK, MLA_HEADS * (MLA_D_NOPE + MLA_D_V)), MLA_KV_RANK ** -0.5),
        "mla_w_o": nrm((N_LAYERS_B, MLA_HEADS * MLA_D_V, D), DEEPNORM_BETA * (MLA_HEADS * MLA_D_V) ** -0.5),
        "gla_w_in": nrm((N_LAYERS_C, D, 2 * dqk_gla + 2 * dv_gla), D ** -0.5),
        "gla_wa1": nrm((N_LAYERS_C, 2, D, GLA_DECAY_RANK), D ** -0.5),
        "gla_wa2": nrm((N_LAYERS_C, 2, GLA_DECAY_RANK, dqk_gla), GLA_DECAY_RANK ** -0.5),
        "gla_ba": nrm((N_LAYERS_C, 2, dqk_gla), 0.1),
        "gla_norm": gain((N_LAYERS_C, GLA_DV)),
        "gla_w_o": nrm((N_LAYERS_C, dv_gla, D), DEEPNORM_BETA * dv_gla ** -0.5),
        "ffn_w13": nrm((N_LAYERS_DENSE, D, 2 * D_FF), D ** -0.5),
        "ffn_w2": nrm((N_LAYERS_DENSE, D_FF, D), DEEPNORM_BETA * D_FF ** -0.5),
        "moe_router": nrm((N_LAYERS_MOE, D, N_EXPERTS), D ** -0.5),
        "moe_w13": nrm((N_LAYERS_MOE, N_EXPERTS, D, 2 * D_FF_EXPERT), D ** -0.5),
        "moe_w2": nrm((N_LAYERS_MOE, N_EXPERTS, D_FF_EXPERT, D), DEEPNORM_BETA * D_FF_EXPERT ** -0.5),
    }


def reference(x_prompt, x_sample, cache_mla_ckv, cache_mla_krope, state_gla, c, c_ctx,
              ada_w, ada_b, ln_g, ln_b,
              gmlp_w_in, gmlp_b_in, gmlp_v_g, gmlp_v_b, gmlp_w_s, gmlp_b_s, gmlp_w_out,
              mla_w_dqkv, mla_q_norm, mla_kv_norm, mla_w_uq, mla_w_ukv, mla_w_o,
              gla_w_in, gla_wa1, gla_wa2, gla_ba, gla_norm, gla_w_o,
              ffn_w13, ffn_w2, moe_router, moe_w13, moe_w2):
    tables = axial_rope_tables(x_sample.shape[1])
    xc, xl = x_prompt, x_sample
    new_ckv, new_krope, new_gla = [], [], []
    for i in range(DEPTH):
        m_ctx = adaln(c_ctx, ada_w[i], ada_b[i])
        m_lat = adaln(c, ada_w[i], ada_b[i])
        hc = modulate(xc, m_ctx[0], m_ctx[1])
        hl = modulate(xl, m_lat[0], m_lat[1])
        kind, j = i % N_MIXERS, i // N_MIXERS
        if kind == 0:
            a_par = (gmlp_w_in[j], gmlp_b_in[j], gmlp_v_g[j], gmlp_v_b[j], gmlp_w_s[j], gmlp_b_s[j], gmlp_w_out[j])
            oc = gmlp_mix(hc, *a_par)
            ol = gmlp_mix(hl, *a_par)
        elif kind == 1:
            b_par = (mla_w_dqkv[j], mla_q_norm[j], mla_kv_norm[j], mla_w_uq[j], mla_w_ukv[j], mla_w_o[j])
            oc, ckv, krope = mla_context(hc, *b_par)
            new_ckv.append(ckv)
            new_krope.append(krope)
            ol = mla_latent(hl, cache_mla_ckv[:, j], cache_mla_krope[:, j], tables, *b_par)
        else:
            q, k, v, r, g_f, g_b = gla_project(hc, gla_w_in[j], gla_wa1[j], gla_wa2[j], gla_ba[j])
            s_zero = jnp.zeros((xc.shape[0], GLA_HEADS, GLA_DK, GLA_DV), jnp.float32)
            o, s_f, s_b = gla_bidir(q, k, v, g_f, g_b, s_zero, s_zero)
            oc = gla_output(o, r, gla_norm[j], gla_w_o[j])
            new_gla.append(jnp.stack([s_f, s_b], axis=1))
            q, k, v, r, g_f, g_b = gla_project(hl, gla_w_in[j], gla_wa1[j], gla_wa2[j], gla_ba[j])
            o, _, _ = gla_bidir(q, k, v, g_f, g_b, state_gla[:, j, 0], state_gla[:, j, 1])
            ol = gla_output(o, r, gla_norm[j], gla_w_o[j])
        xc = post_norm_residual(xc, oc, m_ctx[2], ln_g[i, 0], ln_b[i, 0])
        xl = post_norm_residual(xl, ol, m_lat[2], ln_g[i, 0], ln_b[i, 0])

        hc = modulate(xc, m_ctx[3], m_ctx[4])
        hl = modulate(xl, m_lat[3], m_lat[4])
        f = i // 2
        if i % 2 == 0:
            fc = swiglu(hc, ffn_w13[f], ffn_w2[f])
            fl = swiglu(hl, ffn_w13[f], ffn_w2[f])
        else:
            fc = moe_swiglu(hc, moe_router[f], moe_w13[f], moe_w2[f])
            fl = moe_swiglu(hl, moe_router[f], moe_w13[f], moe_w2[f])
        xc = post_norm_residual(xc, fc, m_ctx[5], ln_g[i, 1], ln_b[i, 1])
        xl = post_norm_residual(xl, fl, m_lat[5], ln_g[i, 1], ln_b[i, 1])

    new_mla_ckv = jnp.stack(new_ckv, axis=1)
    new_mla_krope = jnp.stack(new_krope, axis=1)
    new_state_gla = jnp.stack(new_gla, axis=1)
    return (xc, xl, new_mla_ckv, new_mla_krope, new_state_gla)
```

```python
import functools

import jax
import jax.numpy as jnp
from jax import lax
from jax.experimental import pallas as pl
from jax.experimental.pallas import tpu as pltpu

F32 = jnp.float32
BF16 = jnp.bfloat16
HIGHEST = lax.Precision.HIGHEST

D_MODEL = 1024
DEPTH = 4
GRID_W = 64
DEEPNORM_ALPHA = (2.0 * DEPTH) ** 0.25
LN_EPS = 1e-5
RMS_EPS = 1e-6
GROUP_TOKENS = 4096

CHUNK_A = 128
GMLP_WIDTH = 2 * D_MODEL
GMLP_GROUPS = 8
GMLP_GROUP_DIM = GMLP_WIDTH // GMLP_GROUPS

MLA_HEADS = 8
MLA_Q_RANK = D_MODEL // 2
MLA_KV_RANK = D_MODEL // 4
MLA_D_NOPE = 128
MLA_D_ROPE = 64
MLA_D_V = 128
MLA_D_QK = MLA_D_NOPE + MLA_D_ROPE
ROPE_BASE = 10000.0

GLA_HEADS = 4
GLA_DK = 128
GLA_DV = 256
GLA_DECAY_RANK = 16
GLA_GATE_NORM = 16.0
GLA_CHUNK = 64
GLA_DQK = GLA_HEADS * GLA_DK
GLA_DVT = GLA_HEADS * GLA_DV
GLA_PROJ_W = 2 * GLA_DQK + 2 * GLA_DVT + 2 * GLA_DQK

D_FF = 2816
N_EXPERTS = 8
TOP_K = 2
D_FF_EXPERT = 3584

VMEM_LIMIT_BYTES = 56 * 1024 * 1024
ADA_ROWS = 16
ADA_TN = 1536
TM_GMLP = 256
TM_FFN = 512
TF_FFN = 1408
TM_PROJ = 512
TQ_ATTN = 256
TM_OUT = 512
TB_GLA = 512
TM_ROUTE = 512
TM_EXPERT = 1024
TF_EXPERT = 896
TM_COMBINE = 512


def _cparams(*sem):
    return pltpu.CompilerParams(dimension_semantics=sem, vmem_limit_bytes=VMEM_LIMIT_BYTES)


def _layer_norm(y, g, b):
    mu = jnp.mean(y, axis=-1, keepdims=True)
    d = y - mu
    var = jnp.mean(d * d, axis=-1, keepdims=True)
    return d * lax.rsqrt(var + LN_EPS) * g + b


def _rms_norm(y, g):
    return y * lax.rsqrt(jnp.mean(y * y, axis=-1, keepdims=True) + RMS_EPS) * g


def _modulate(x, mod_ref, shift_row):
    return x * (1.0 + mod_ref[shift_row + 1:shift_row + 2, :]) + mod_ref[shift_row:shift_row + 1, :]


def _post_norm(x, out, mod_ref, gate_row, lng_ref, lnb_ref):
    y = DEEPNORM_ALPHA * x + mod_ref[gate_row:gate_row + 1, :] * out
    return _layer_norm(y, lng_ref[...], lnb_ref[...])


def _dot(a, b):
    return jnp.dot(a, b, preferred_element_type=F32)


def _dot_nt(a, b):
    return lax.dot_general(a, b, (((1,), (1,)), ((), ())), preferred_element_type=F32)


def _dot_tn(a, b):
    return lax.dot_general(a, b, (((0,), (0,)), ((), ())), preferred_element_type=F32)


def _mod_spec(tm):
    return pl.BlockSpec((None, 6, D_MODEL), lambda i, *_: ((i * tm) // GROUP_TOKENS, 0, 0))


def _full_spec(shape):
    nd = len(shape)
    return pl.BlockSpec(shape, lambda *_: (0,) * nd)


def _ada_kernel(c_ref, w_ref, b_ref, o_ref):
    c = c_ref[...]
    s = c * jax.nn.sigmoid(c)
    o_ref[...] = jnp.dot(s, w_ref[...], precision=HIGHEST, preferred_element_type=F32) + b_ref[...]


def _adaln_all(cond, ada_w, ada_b):
    n_out = 6 * D_MODEL
    return pl.pallas_call(
        _ada_kernel,
        grid=(DEPTH, n_out // ADA_TN),
        in_specs=[
            pl.BlockSpec((ADA_ROWS, D_MODEL), lambda l, j: (0, 0)),
            pl.BlockSpec((None, D_MODEL, ADA_TN), lambda l, j: (l, 0, j)),
            pl.BlockSpec((None, 1, ADA_TN), lambda l, j: (l, 0, j)),
        ],
        out_specs=pl.BlockSpec((None, ADA_ROWS, ADA_TN), lambda l, j: (l, 0, j)),
        out_shape=jax.ShapeDtypeStruct((DEPTH, ADA_ROWS, n_out), F32),
        compiler_params=_cparams("parallel", "parallel"),
        name="adaln",
    )(cond, ada_w, ada_b.reshape(DEPTH, 1, n_out))


def _gmlp_kernel(x_ref, mod_ref, win_ref, bin_ref, vg_ref, vb_ref, ws_ref, bs_ref, wout_ref,
                 lng_ref, lnb_ref, o_ref, u_ref, vn_ref, p_ref):
    x = x_ref[...]
    hb = _modulate(x, mod_ref, 0).astype(BF16)
    v = jax.nn.gelu(_dot(hb, win_ref[:, GMLP_WIDTH:]) + bin_ref[:, GMLP_WIDTH:])
    vn_ref[...] = _layer_norm(v, vg_ref[...], vb_ref[...]).astype(BF16)
    u_ref[...] = jax.nn.gelu(_dot(hb, win_ref[:, :GMLP_WIDTH]) + bin_ref[:, :GMLP_WIDTH])
    for c in range(x.shape[0] // CHUNK_A):
        rows = slice(c * CHUNK_A, (c + 1) * CHUNK_A)
        for g in range(GMLP_GROUPS):
            cols = slice(g * GMLP_GROUP_DIM, (g + 1) * GMLP_GROUP_DIM)
            mixed = _dot(ws_ref[g], vn_ref[rows, cols]) + bs_ref[g]
            p_ref[rows, cols] = (u_ref[rows, cols] * mixed).astype(BF16)
    out = _dot(p_ref[...], wout_ref[...])
    o_ref[...] = _post_norm(x, out, mod_ref, 2, lng_ref, lnb_ref)


def _gmlp_layer(x, mods, w_in, b_in, v_g, v_b, w_s, b_s, w_out, ln_g, ln_b):
    n = x.shape[0]
    tm = TM_GMLP
    row = pl.BlockSpec((tm, D_MODEL), lambda i: (i, 0))
    return pl.pallas_call(
        _gmlp_kernel,
        grid=(n // tm,),
        in_specs=[
            row, _mod_spec(tm),
            _full_spec((D_MODEL, 2 * GMLP_WIDTH)), _full_spec((1, 2 * GMLP_WIDTH)),
            _full_spec((1, GMLP_WIDTH)), _full_spec((1, GMLP_WIDTH)),
            _full_spec((GMLP_GROUPS, CHUNK_A, CHUNK_A)), _full_spec((GMLP_GROUPS, CHUNK_A, 1)),
            _full_spec((GMLP_WIDTH, D_MODEL)),
            _full_spec((1, D_MODEL)), _full_spec((1, D_MODEL)),
        ],
        out_specs=row,
        out_shape=jax.ShapeDtypeStruct((n, D_MODEL), F32),
        scratch_shapes=[pltpu.VMEM((tm, GMLP_WIDTH), F32), pltpu.VMEM((tm, GMLP_WIDTH), BF16),
                        pltpu.VMEM((tm, GMLP_WIDTH), BF16)],
        compiler_params=_cparams("parallel"),
        name="gmlp",
    )(x, mods, w_in.astype(BF16), b_in.reshape(1, -1), v_g.reshape(1, -1), v_b.reshape(1, -1),
      w_s.astype(BF16), b_s.reshape(GMLP_GROUPS, CHUNK_A, 1), w_out.astype(BF16),
      ln_g.reshape(1, -1), ln_b.reshape(1, -1))


def _ffn_kernel(x_ref, mod_ref, w1_ref, w3_ref, w2_ref, lng_ref, lnb_ref, o_ref, hb_ref, acc_ref):
    j = pl.program_id(1)

    @pl.when(j == 0)
    def _():
        hb_ref[...] = _modulate(x_ref[...], mod_ref, 3).astype(BF16)
        acc_ref[...] = jnp.zeros_like(acc_ref)

    hb = hb_ref[...]
    a = _dot(hb, w1_ref[...])
    b = _dot(hb, w3_ref[...])
    acc_ref[...] += _dot((a * jax.nn.sigmoid(a) * b).astype(BF16), w2_ref[...])

    @pl.when(j == pl.num_programs(1) - 1)
    def _():
        o_ref[...] = _post_norm(x_ref[...], acc_ref[...], mod_ref, 5, lng_ref, lnb_ref)


def _ffn_layer(x, mods, w13, w2, ln_g, ln_b):
    n = x.shape[0]
    tm, tf = TM_FFN, TF_FFN
    nf = D_FF // tf
    row = pl.BlockSpec((tm, D_MODEL), lambda i, j: (i, 0))
    return pl.pallas_call(
        _ffn_kernel,
        grid=(n // tm, nf),
        in_specs=[
            row, _mod_spec(tm),
            pl.BlockSpec((D_MODEL, tf), lambda i, j: (0, j)),
            pl.BlockSpec((D_MODEL, tf), lambda i, j: (0, j + nf)),
            pl.BlockSpec((tf, D_MODEL), lambda i, j: (j, 0)),
            _full_spec((1, D_MODEL)), _full_spec((1, D_MODEL)),
        ],
        out_specs=row,
        out_shape=jax.ShapeDtypeStruct((n, D_MODEL), F32),
        scratch_shapes=[pltpu.VMEM((tm, D_MODEL), BF16), pltpu.VMEM((tm, D_MODEL), F32)],
        compiler_params=_cparams("parallel", "arbitrary"),
        name="ffn",
    )(x, mods, w13, w13, w2, ln_g.reshape(1, -1), ln_b.reshape(1, -1))


def _rope_swap_perm():
    j = jnp.arange(MLA_D_ROPE)
    half = MLA_D_ROPE // 4
    return jnp.where((j % (2 * half)) < half, j + half, j - half)


def _rope_tables(n_tokens):
    n_freq = MLA_D_ROPE // 4
    t = jnp.arange(n_tokens)
    pos_r = (t // GRID_W).astype(F32)[:, None]
    pos_c = (t % GRID_W).astype(F32)[:, None]
    inv = ROPE_BASE ** (-jnp.arange(n_freq, dtype=F32) / n_freq)
    ang_r, ang_c = pos_r * inv, pos_c * inv
    cos = jnp.concatenate([jnp.cos(ang_r), jnp.cos(ang_r), jnp.cos(ang_c), jnp.cos(ang_c)], axis=-1)
    sin = jnp.concatenate([-jnp.sin(ang_r), jnp.sin(ang_r), -jnp.sin(ang_c), jnp.sin(ang_c)], axis=-1)
    return cos, sin


def _write_heads(q_ref, k_ref, v_ref, q_nope, q_rope, kv, k_rope):
    scale = MLA_D_QK ** -0.5
    k_rope = k_rope.astype(BF16)
    for h in range(MLA_HEADS):
        nope = slice(h * MLA_D_NOPE, (h + 1) * MLA_D_NOPE)
        if q_ref is not None:
            q_ref[h, :, :MLA_D_NOPE] = (q_nope[:, nope] * scale).astype(BF16)
            q_ref[h, :, MLA_D_NOPE:] = (q_rope[:, h * MLA_D_ROPE:(h + 1) * MLA_D_ROPE] * scale).astype(BF16)
        k_ref[h, :, :MLA_D_NOPE] = kv[:, nope].astype(BF16)
        k_ref[h, :, MLA_D_NOPE:] = k_rope
        vcol = MLA_HEADS * MLA_D_NOPE + h * MLA_D_V
        v_ref[h] = kv[:, vcol:vcol + MLA_D_V].astype(BF16)


def _mla_proj_kernel(*refs, rope):
    if rope:
        (x_ref, mod_ref, wd_ref, qg_ref, kvg_ref, wuq_ref, wukv_ref, cq_ref, sq_ref, ck_ref, sk_ref,
         q_ref, k_ref, v_ref, ckv_ref, kr_ref) = refs
    else:
        (x_ref, mod_ref, wd_ref, qg_ref, kvg_ref, wuq_ref, wukv_ref,
         q_ref, k_ref, v_ref, ckv_ref, kr_ref) = refs
    hb = _modulate(x_ref[...], mod_ref, 0).astype(BF16)
    c = _dot(hb, wd_ref[...])
    r0 = MLA_Q_RANK + MLA_KV_RANK
    c_q = _rms_norm(c[:, :MLA_Q_RANK], qg_ref[...]).astype(BF16)
    c_kv = _rms_norm(c[:, MLA_Q_RANK:r0], kvg_ref[...])
    k_rope = c[:, r0:r0 + MLA_D_ROPE]
    ckv_ref[...] = c_kv
    kr_ref[...] = k_rope
    q = _dot(c_q, wuq_ref[...])
    n_nope = MLA_HEADS * MLA_D_NOPE
    n_rope = MLA_HEADS * MLA_D_ROPE
    q_rope = q[:, n_nope:n_nope + n_rope]
    if rope:
        q_rope = q_rope * cq_ref[...] + q[:, n_nope + n_rope:] * sq_ref[...]
        k_rope = k_rope * ck_ref[...] + c[:, r0 + MLA_D_ROPE:] * sk_ref[...]
    kv = _dot(c_kv.astype(BF16), wukv_ref[...])
    _write_heads(q_ref, k_ref, v_ref, q[:, :n_nope], q_rope, kv, k_rope)


def _mla_project(x, row_offset, n_seq, seq_len, mods, weights, tables):
    wd, qg, kvg, wuq, wukv = weights
    tm = min(TM_PROJ, seq_len)
    per_seq = seq_len // tm
    n = n_seq * seq_len
    off = row_offset // tm
    rope = tables is not None
    in_specs = [
        pl.BlockSpec((tm, D_MODEL), lambda i: (off + i, 0)),
        pl.BlockSpec((None, 6, D_MODEL), lambda i: ((row_offset + i * tm) // GROUP_TOKENS, 0, 0)),
        _full_spec(wd.shape), _full_spec(qg.shape), _full_spec(kvg.shape),
        _full_spec(wuq.shape), _full_spec(wukv.shape),
    ]
    args = [x, mods, wd, qg, kvg, wuq, wukv]
    if rope:
        pos = lambda i: (i % per_seq, 0)
        in_specs += [pl.BlockSpec((tm, MLA_HEADS * MLA_D_ROPE), pos)] * 2 + [pl.BlockSpec((tm, MLA_D_ROPE), pos)] * 2
        args += list(tables)
    head_spec = lambda d: pl.BlockSpec((None, MLA_HEADS, tm, d), lambda i: (i // per_seq, 0, i % per_seq, 0))
    head_shape = lambda d: jax.ShapeDtypeStruct((n_seq, MLA_HEADS, seq_len, d), BF16)
    return pl.pallas_call(
        functools.partial(_mla_proj_kernel, rope=rope),
        grid=(n // tm,),
        in_specs=in_specs,
        out_specs=[head_spec(MLA_D_QK), head_spec(MLA_D_QK), head_spec(MLA_D_V),
                   pl.BlockSpec((tm, MLA_KV_RANK), lambda i: (i, 0)),
                   pl.BlockSpec((tm, MLA_D_ROPE), lambda i: (i, 0))],
        out_shape=[head_shape(MLA_D_QK), head_shape(MLA_D_QK), head_shape(MLA_D_V),
                   jax.ShapeDtypeStruct((n, MLA_KV_RANK), F32),
                   jax.ShapeDtypeStruct((n, MLA_D_ROPE), F32)],
        compiler_params=_cparams("parallel"),
        name="mla_proj_rope" if rope else "mla_proj",
    )(*args)


def _mla_cache_kernel(ckv_ref, kr_ref, wukv_ref, k_ref, v_ref):
    kv = _dot(ckv_ref[...].astype(BF16), wukv_ref[...])
    _write_heads(None, k_ref, v_ref, None, None, kv, kr_ref[...])


def _mla_cache_keys(ckv, krope, wukv):
    b, l, _ = ckv.shape
    head_spec = lambda d: pl.BlockSpec((None, MLA_HEADS, l, d), lambda i: (i, 0, 0, 0))
    return pl.pallas_call(
        _mla_cache_kernel,
        grid=(b,),
        in_specs=[pl.BlockSpec((None, l, MLA_KV_RANK), lambda i: (i, 0, 0)),
                  pl.BlockSpec((None, l, MLA_D_ROPE), lambda i: (i, 0, 0)),
                  _full_spec(wukv.shape)],
        out_specs=[head_spec(MLA_D_QK), head_spec(MLA_D_V)],
        out_shape=[jax.ShapeDtypeStruct((b, MLA_HEADS, l, MLA_D_QK), BF16),
                   jax.ShapeDtypeStruct((b, MLA_HEADS, l, MLA_D_V), BF16)],
        compiler_params=_cparams("parallel"),
        name="mla_cache_keys",
    )(ckv, krope, wukv)


def _attn_kernel(*refs, n_seg):
    q_ref = refs[0]
    k_refs = refs[1:1 + n_seg]
    v_refs = refs[1 + n_seg:1 + 2 * n_seg]
    o_ref = refs[1 + 2 * n_seg]
    q = q_ref[...]
    s = [_dot_nt(q, k[...]) for k in k_refs]
    m = functools.reduce(jnp.maximum, [jnp.max(si, axis=-1, keepdims=True) for si in s])
    p = [jnp.exp(si - m) for si in s]
    denom = sum(jnp.sum(pi, axis=-1, keepdims=True) for pi in p)
    o = sum(_dot(pi.astype(BF16), v[...]) for pi, v in zip(p, v_refs))
    o_ref[...] = (o / denom).astype(BF16)


def _attention(q, ks, vs):
    b, h, t, _ = q.shape
    tq = min(TQ_ATTN, t)
    seg = lambda a: pl.BlockSpec((None, None) + a.shape[2:], lambda bi, hi, qi: (bi, hi, 0, 0))
    return pl.pallas_call(
        functools.partial(_attn_kernel, n_seg=len(ks)),
        grid=(b, h, t // tq),
        in_specs=[pl.BlockSpec((None, None, tq, MLA_D_QK), lambda bi, hi, qi: (bi, hi, qi, 0))]
        + [seg(a) for a in ks] + [seg(a) for a in vs],
        out_specs=pl.BlockSpec((None, tq, MLA_D_V), lambda bi, hi, qi: (bi, qi, hi)),
        out_shape=jax.ShapeDtypeStruct((b, t, h * MLA_D_V), BF16),
        compiler_params=_cparams("parallel", "parallel", "arbitrary"),
        name="mla_attention",
    )(q, *ks, *vs)


def _out_proj_kernel(x_ref, mod_ref, a_ref, w_ref, lng_ref, lnb_ref, o_ref):
    out = _dot(a_ref[...], w_ref[...])
    o_ref[...] = _post_norm(x_ref[...], out, mod_ref, 2, lng_ref, lnb_ref)


def _out_proj_layer(x, mods, a, w_o, ln_g, ln_b):
    n = x.shape[0]
    tm = TM_OUT
    row = pl.BlockSpec((tm, D_MODEL), lambda i: (i, 0))
    return pl.pallas_call(
        _out_proj_kernel,
        grid=(n // tm,),
        in_specs=[row, _mod_spec(tm), pl.BlockSpec((tm, a.shape[1]), lambda i: (i, 0)),
                  _full_spec(w_o.shape), _full_spec((1, D_MODEL)), _full_spec((1, D_MODEL))],
        out_specs=row,
        out_shape=jax.ShapeDtypeStruct((n, D_MODEL), F32),
        compiler_params=_cparams("parallel"),
        name="mla_out_proj",
    )(x, mods, a, w_o, ln_g.reshape(1, -1), ln_b.reshape(1, -1))


def _gla_proj_kernel(x_ref, mod_ref, win_ref, wa1_ref, wa2_ref, ba_ref, o_ref):
    h = _modulate(x_ref[...], mod_ref, 0)
    qkvr = _dot(h.astype(BF16), win_ref[...])
    o_ref[:, :GLA_DQK] = qkvr[:, :GLA_DQK] * (GLA_DK ** -0.5)
    n_qkvr = 2 * GLA_DQK + 2 * GLA_DVT
    o_ref[:, GLA_DQK:n_qkvr] = qkvr[:, GLA_DQK:]
    low = jnp.dot(h, wa1_ref[...], precision=HIGHEST, preferred_element_type=F32)
    logit = jnp.dot(low, wa2_ref[...], precision=HIGHEST, preferred_element_type=F32) + ba_ref[...]
    log_sig = jnp.minimum(logit, 0.0) - jnp.log(1.0 + jnp.exp(-jnp.abs(logit)))
    o_ref[:, n_qkvr:] = log_sig / GLA_GATE_NORM


def _gla_project(x, mods, w_in, wa1, wa2, ba):
    n = x.shape[0]
    tm = TM_PROJ
    wa1_cat = jnp.concatenate([wa1[0], wa1[1]], axis=1)
    zeros = jnp.zeros((GLA_DECAY_RANK, GLA_DQK), F32)
    wa2_blk = jnp.concatenate([jnp.concatenate([wa2[0], zeros], axis=1),
                               jnp.concatenate([zeros, wa2[1]], axis=1)], axis=0)
    return pl.pallas_call(
        _gla_proj_kernel,
        grid=(n // tm,),
        in_specs=[pl.BlockSpec((tm, D_MODEL), lambda i: (i, 0)), _mod_spec(tm),
                  _full_spec(w_in.shape), _full_spec(wa1_cat.shape), _full_spec(wa2_blk.shape),
                  _full_spec((1, 2 * GLA_DQK))],
        out_specs=pl.BlockSpec((tm, GLA_PROJ_W), lambda i: (i, 0)),
        out_shape=jax.ShapeDtypeStruct((n, GLA_PROJ_W), F32),
        compiler_params=_cparams("parallel"),
        name="gla_proj",
    )(x, mods, w_in.astype(BF16), wa1_cat, wa2_blk, ba.reshape(1, -1))


def _chunk_cumsum(g, reverse):
    n = g.shape[0]
    pos = lax.broadcasted_iota(jnp.int32, g.shape, 0) % GLA_CHUNK
    b = g
    shift = 1
    while shift < GLA_CHUNK:
        if reverse:
            moved = pltpu.roll(b, n - shift, 0)
            keep = pos < GLA_CHUNK - shift
        else:
            moved = pltpu.roll(b, shift, 0)
            keep = pos >= shift
        b = b + jnp.where(keep, moved, 0.0)
        shift *= 2
    return b


def _gla_scan_kernel(q_ref, k_ref, v_ref, g_ref, s0_ref, o_ref, sfin_ref, st_ref, *, reverse):
    i = pl.program_id(2)

    @pl.when(i == 0)
    def _():
        st_ref[...] = s0_ref[...]

    tb = q_ref.shape[0]
    b_all = _chunk_cumsum(g_ref[...], reverse)
    ii = lax.broadcasted_iota(jnp.int32, (GLA_CHUNK, GLA_CHUNK), 0)
    jj = lax.broadcasted_iota(jnp.int32, (GLA_CHUNK, GLA_CHUNK), 1)
    mask = (jj >= ii) if reverse else (jj <= ii)
    chunks = range(tb // GLA_CHUNK)
    for c in (reversed(chunks) if reverse else chunks):
        rows = slice(c * GLA_CHUNK, (c + 1) * GLA_CHUNK)
        b = b_all[rows]
        b_tot = b[:1] if reverse else b[GLA_CHUNK - 1:]
        q = q_ref[rows, :]
        k = k_ref[rows, :]
        v = v_ref[rows, :].astype(BF16)
        qe = (q * jnp.exp(b)).astype(BF16)
        ke = (k * jnp.exp(-b)).astype(BF16)
        kd = (k * jnp.exp(b_tot - b)).astype(BF16)
        a = jnp.where(mask, _dot_nt(qe, ke), 0.0).astype(BF16)
        st = st_ref[...]
        o_ref[rows, :] = _dot(a, v) + _dot_nt(qe, st.astype(BF16))
        st_ref[...] = st * jnp.exp(b_tot) + _dot_tn(v, kd)

    @pl.when(i == pl.num_programs(2) - 1)
    def _():
        sfin_ref[...] = st_ref[...]


def _gla_scan(proj, row_offset, n_seq, seq_len, s0_t, reverse):
    tb = min(TB_GLA, seq_len)
    nb = seq_len // tb
    off = row_offset // tb
    d = 1 if reverse else 0

    def blk(s, i):
        return off + s * nb + ((nb - 1 - i) if reverse else i)

    def oblk(s, i):
        return s * nb + ((nb - 1 - i) if reverse else i)

    v_col0 = 2 * GLA_DQK // GLA_DV
    g_col0 = (2 * GLA_DQK + 2 * GLA_DVT) // GLA_DK + d * GLA_HEADS
    state_spec = pl.BlockSpec((None, None, GLA_DV, GLA_DK), lambda s, h, i: (s, h, 0, 0))
    return pl.pallas_call(
        functools.partial(_gla_scan_kernel, reverse=reverse),
        grid=(n_seq, GLA_HEADS, nb),
        in_specs=[
            pl.BlockSpec((tb, GLA_DK), lambda s, h, i: (blk(s, i), h)),
            pl.BlockSpec((tb, GLA_DK), lambda s, h, i: (blk(s, i), GLA_HEADS + h)),
            pl.BlockSpec((tb, GLA_DV), lambda s, h, i: (blk(s, i), v_col0 + h)),
            pl.BlockSpec((tb, GLA_DK), lambda s, h, i: (blk(s, i), g_col0 + h)),
            state_spec,
        ],
        out_specs=[pl.BlockSpec((tb, GLA_DV), lambda s, h, i: (oblk(s, i), h)), state_spec],
        out_shape=[jax.ShapeDtypeStruct((n_seq * seq_len, GLA_DVT), F32),
                   jax.ShapeDtypeStruct((n_seq, GLA_HEADS, GLA_DV, GLA_DK), F32)],
        scratch_shapes=[pltpu.VMEM((GLA_DV, GLA_DK), F32)],
        compiler_params=_cparams("parallel", "parallel", "arbitrary"),
        name="gla_scan_bwd" if reverse else "gla_scan_fwd",
    )(proj, proj, proj, proj, s0_t)


def _gla_out_kernel(x_ref, mod_ref, of_ref, ob_ref, r_ref, ng_ref, wo_ref, lng_ref, lnb_ref, o_ref, p_ref):
    for h in range(GLA_HEADS):
        cols = slice(h * GLA_DV, (h + 1) * GLA_DV)
        o = _rms_norm(of_ref[:, cols] + ob_ref[:, cols], ng_ref[...])
        r = r_ref[:, cols]
        p_ref[:, cols] = (o * (r * jax.nn.sigmoid(r))).astype(BF16)
    out = _dot(p_ref[...], wo_ref[...])
    o_ref[...] = _post_norm(x_ref[...], out, mod_ref, 2, lng_ref, lnb_ref)


def _gla_out_layer(x, mods, o_f, o_b, proj, norm_g, w_o, ln_g, ln_b):
    n = x.shape[0]
    tm = TM_OUT
    row = pl.BlockSpec((tm, D_MODEL), lambda i: (i, 0))
    r_col = 2 * GLA_DQK // GLA_DVT
    return pl.pallas_call(
        _gla_out_kernel,
        grid=(n // tm,),
        in_specs=[row, _mod_spec(tm), row, row,
                  pl.BlockSpec((tm, GLA_DVT), lambda i: (i, r_col + 1)),
                  _full_spec((1, GLA_DV)), _full_spec(w_o.shape),
                  _full_spec((1, D_MODEL)), _full_spec((1, D_MODEL))],
        out_specs=row,
        out_shape=jax.ShapeDtypeStruct((n, D_MODEL), F32),
        scratch_shapes=[pltpu.VMEM((tm, GLA_DVT), BF16)],
        compiler_params=_cparams("parallel"),
        name="gla_out_proj",
    )(x, mods, o_f, o_b, proj, norm_g.reshape(1, -1), w_o.astype(BF16),
      ln_g.reshape(1, -1), ln_b.reshape(1, -1))


def _router_kernel(x_ref, mod_ref, wr_ref, h_ref, idx_ref, wt_ref):
    h = _modulate(x_ref[...], mod_ref, 3)
    h_ref[...] = h
    logits = lax.dot_general(wr_ref[...], h, (((1,), (1,)), ((), ())), precision=HIGHEST,
                             preferred_element_type=F32)
    e = lax.broadcasted_iota(jnp.int32, logits.shape, 0)
    m1 = jnp.max(logits, axis=0, keepdims=True)
    i1 = jnp.min(jnp.where(logits == m1, e, N_EXPERTS), axis=0, keepdims=True)
    rest = jnp.where(e == i1, -jnp.inf, logits)
    m2 = jnp.max(rest, axis=0, keepdims=True)
    i2 = jnp.min(jnp.where(rest == m2, e, N_EXPERTS), axis=0, keepdims=True)
    z = jnp.exp(m2 - m1)
    idx_ref[0:1, :] = i1
    idx_ref[1:2, :] = i2
    wt_ref[0:1, :] = 1.0 / (1.0 + z)
    wt_ref[1:2, :] = z / (1.0 + z)


def _route(x, mods, w_router):
    n = x.shape[0]
    tm = TM_ROUTE
    row = pl.BlockSpec((tm, D_MODEL), lambda i: (i, 0))
    pair = pl.BlockSpec((TOP_K, tm), lambda i: (0, i))
    return pl.pallas_call(
        _router_kernel,
        grid=(n // tm,),
        in_specs=[row, _mod_spec(tm), _full_spec((N_EXPERTS, D_MODEL))],
        out_specs=[row, pair, pair],
        out_shape=[jax.ShapeDtypeStruct((n, D_MODEL), F32),
                   jax.ShapeDtypeStruct((TOP_K, n), jnp.int32),
                   jax.ShapeDtypeStruct((TOP_K, n), F32)],
        compiler_params=_cparams("parallel"),
        name="moe_router",
    )(x, mods, w_router.T)


def _dispatch_plan(expert_idx, n_tiles):
    n = expert_idx.shape[1]
    flat = expert_idx.reshape(-1)
    onehot = (flat[:, None] == jnp.arange(N_EXPERTS)[None, :]).astype(jnp.int32)
    running = jnp.cumsum(onehot, axis=0)
    rank = jnp.sum(running * onehot, axis=1) - 1
    counts = running[-1]
    tiles = (counts + TM_EXPERT - 1) // TM_EXPERT
    tile_end = jnp.cumsum(tiles)
    tile_start = tile_end - tiles
    slot = tile_start[flat] * TM_EXPERT + rank
    token = jnp.tile(jnp.arange(n, dtype=jnp.int32), TOP_K)
    gather_idx = jnp.zeros((n_tiles * TM_EXPERT,), jnp.int32).at[slot].set(token)
    tile_id = jnp.arange(n_tiles)
    tile_active = (tile_id < tile_end[-1]).astype(jnp.int32)
    tile_expert = jnp.minimum(jnp.searchsorted(tile_end, tile_id, side="right"), N_EXPERTS - 1)
    last_expert = tile_expert[jnp.maximum(tile_end[-1] - 1, 0)]
    tile_expert = jnp.where(tile_active == 1, tile_expert, last_expert).astype(jnp.int32)
    return gather_idx, slot.reshape(TOP_K, n).astype(jnp.int32), tile_expert, tile_active


def _row_gather(idx_ref, base, count, src_hbm, dst_ref, sem):
    def issue(r, carry):
        pltpu.make_async_copy(src_hbm.at[pl.ds(idx_ref[base + r], 1), :],
                              dst_ref.at[pl.ds(r, 1), :], sem).start()
        return carry

    lax.fori_loop(0, count, issue, 0)
    pltpu.make_async_copy(src_hbm.at[pl.ds(0, count), :], dst_ref, sem).wait()


def _dispatch_kernel(idx_ref, h_hbm, o_ref, buf_ref, sem):
    tm = buf_ref.shape[0]
    _row_gather(idx_ref, pl.program_id(0) * tm, tm, h_hbm, buf_ref, sem)
    o_ref[...] = buf_ref[...].astype(BF16)


def _dispatch(h, gather_idx):
    tm = TM_EXPERT
    n_slots = gather_idx.shape[0]
    return pl.pallas_call(
        _dispatch_kernel,
        grid_spec=pltpu.PrefetchScalarGridSpec(
            num_scalar_prefetch=1,
            grid=(n_slots // tm,),
            in_specs=[pl.BlockSpec(memory_space=pl.ANY)],
            out_specs=pl.BlockSpec((tm, D_MODEL), lambda i, idx: (i, 0)),
            scratch_shapes=[pltpu.VMEM((tm, D_MODEL), F32), pltpu.SemaphoreType.DMA(())],
        ),
        out_shape=jax.ShapeDtypeStruct((n_slots, D_MODEL), BF16),
        compiler_params=_cparams("arbitrary"),
        name="moe_dispatch",
    )(gather_idx, h)


def _expert_kernel(te_ref, ta_ref, x_ref, w1_ref, w3_ref, w2_ref, o_ref, acc_ref):
    i, j = pl.program_id(0), pl.program_id(1)
    last = pl.num_programs(1) - 1
    active = ta_ref[i] == 1

    @pl.when(active)
    def _():
        x = x_ref[...]
        a = _dot(x, w1_ref[...])
        b = _dot(x, w3_ref[...])
        part = _dot((a * jax.nn.sigmoid(a) * b).astype(BF16), w2_ref[...])

        @pl.when(j == 0)
        def _():
            acc_ref[...] = part

        @pl.when(j > 0)
        def _():
            acc_ref[...] += part

    @pl.when(jnp.logical_and(j == last, active))
    def _():
        o_ref[...] = acc_ref[...]

    @pl.when(jnp.logical_and(j == last, jnp.logical_not(active)))
    def _():
        o_ref[...] = jnp.zeros_like(o_ref)


def _expert_ffn(xs, tile_expert, tile_active, w13, w2):
    tm, tf = TM_EXPERT, TF_EXPERT
    nf = D_FF_EXPERT // tf
    n_slots = xs.shape[0]
    row = pl.BlockSpec((tm, D_MODEL), lambda i, j, te, ta: (i, 0))
    return pl.pallas_call(
        _expert_kernel,
        grid_spec=pltpu.PrefetchScalarGridSpec(
            num_scalar_prefetch=2,
            grid=(n_slots // tm, nf),
            in_specs=[
                row,
                pl.BlockSpec((None, D_MODEL, tf), lambda i, j, te, ta: (te[i], 0, j)),
                pl.BlockSpec((None, D_MODEL, tf), lambda i, j, te, ta: (te[i], 0, j + nf)),
                pl.BlockSpec((None, tf, D_MODEL), lambda i, j, te, ta: (te[i], j, 0)),
            ],
            out_specs=row,
            scratch_shapes=[pltpu.VMEM((tm, D_MODEL), F32)],
        ),
        out_shape=jax.ShapeDtypeStruct((n_slots, D_MODEL), F32),
        compiler_params=_cparams("parallel", "arbitrary"),
        name="moe_experts",
    )(tile_expert, tile_active, xs, w13, w13, w2)


def _combine_kernel(s1_ref, s2_ref, x_ref, mod_ref, wt_ref, y_hbm, lng_ref, lnb_ref, o_ref,
                    b1_ref, b2_ref, sem1, sem2):
    tm = x_ref.shape[0]
    base = pl.program_id(0) * tm
    _row_gather(s1_ref, base, tm, y_hbm, b1_ref, sem1)
    _row_gather(s2_ref, base, tm, y_hbm, b2_ref, sem2)
    y = wt_ref[:, 0:1] * b1_ref[...] + wt_ref[:, 1:2] * b2_ref[...]
    o_ref[...] = _post_norm(x_ref[...], y, mod_ref, 5, lng_ref, lnb_ref)


def _combine(x, mods, slot_of, weights_t, ys, ln_g, ln_b):
    n = x.shape[0]
    tm = TM_COMBINE
    row = pl.BlockSpec((tm, D_MODEL), lambda i, s1, s2: (i, 0))
    return pl.pallas_call(
        _combine_kernel,
        grid_spec=pltpu.PrefetchScalarGridSpec(
            num_scalar_prefetch=2,
            grid=(n // tm,),
            in_specs=[row,
                      pl.BlockSpec((None, 6, D_MODEL), lambda i, s1, s2: ((i * tm) // GROUP_TOKENS, 0, 0)),
                      pl.BlockSpec((tm, TOP_K), lambda i, s1, s2: (i, 0)),
                      pl.BlockSpec(memory_space=pl.ANY),
                      pl.BlockSpec((1, D_MODEL), lambda i, s1, s2: (0, 0)),
                      pl.BlockSpec((1, D_MODEL), lambda i, s1, s2: (0, 0))],
            out_specs=row,
            scratch_shapes=[pltpu.VMEM((tm, D_MODEL), F32), pltpu.VMEM((tm, D_MODEL), F32),
                            pltpu.SemaphoreType.DMA(()), pltpu.SemaphoreType.DMA(())],
        ),
        out_shape=jax.ShapeDtypeStruct((n, D_MODEL), F32),
        compiler_params=_cparams("arbitrary"),
        name="moe_combine",
    )(slot_of[0], slot_of[1], x, mods, weights_t, ys, ln_g.reshape(1, -1), ln_b.reshape(1, -1))


def _moe_layer(x, mods, w_router, w13, w2, ln_g, ln_b):
    n = x.shape[0]
    n_tiles = (TOP_K * n) // TM_EXPERT + N_EXPERTS
    h, expert_idx, weights = _route(x, mods, w_router)
    gather_idx, slot_of, tile_expert, tile_active = _dispatch_plan(expert_idx, n_tiles)
    xs = _dispatch(h, gather_idx)
    ys = _expert_ffn(xs, tile_expert, tile_active, w13, w2)
    return _combine(x, mods, slot_of, weights.T, ys, ln_g, ln_b)


def kernel(x_prompt, x_sample, cache_mla_ckv, cache_mla_krope, state_gla, c, c_ctx, ada_w, ada_b, ln_g, ln_b, gmlp_w_in, gmlp_b_in, gmlp_v_g, gmlp_v_b, gmlp_w_s, gmlp_b_s, gmlp_w_out, mla_w_dqkv, mla_q_norm, mla_kv_norm, mla_w_uq, mla_w_ukv, mla_w_o, gla_w_in, gla_wa1, gla_wa2, gla_ba, gla_norm, gla_w_o, ffn_w13, ffn_w2, moe_router, moe_w13, moe_w2):
    n_ctx_seq, ctx_len, _ = x_prompt.shape
    n_lat_seq, lat_len, _ = x_sample.shape
    n_ctx = n_ctx_seq * ctx_len
    assert n_ctx == GROUP_TOKENS and lat_len == GROUP_TOKENS
    x = jnp.concatenate([x_prompt.reshape(n_ctx, D_MODEL), x_sample.reshape(-1, D_MODEL)], axis=0)

    cond = jnp.concatenate([c_ctx[None, :], c, jnp.zeros((ADA_ROWS - 1 - n_lat_seq, D_MODEL), F32)], axis=0)
    n_groups = 1 + n_lat_seq
    mods_all = _adaln_all(cond, ada_w, ada_b)[:, :n_groups].reshape(DEPTH, n_groups, 6, D_MODEL)

    new_ckv, new_krope, new_gla = [], [], []
    for i in range(DEPTH):
        mods = mods_all[i]
        kind, j = i % 3, i // 3
        lng, lnb = ln_g[i, 0], ln_b[i, 0]
        if kind == 0:
            x = _gmlp_layer(x, mods, gmlp_w_in[j], gmlp_b_in[j], gmlp_v_g[j], gmlp_v_b[j], gmlp_w_s[j],
                            gmlp_b_s[j], gmlp_w_out[j], lng, lnb)
        elif kind == 1:
            swap = _rope_swap_perm()
            wd = mla_w_dqkv[j]
            r0 = MLA_Q_RANK + MLA_KV_RANK
            wd = jnp.concatenate([wd, wd[:, r0:][:, swap]], axis=1).astype(BF16)
            wuq = mla_w_uq[j].reshape(MLA_Q_RANK, MLA_HEADS, MLA_D_QK)
            wuq_rope = wuq[:, :, MLA_D_NOPE:]
            wuq = jnp.concatenate([wuq[:, :, :MLA_D_NOPE].reshape(MLA_Q_RANK, -1),
                                   wuq_rope.reshape(MLA_Q_RANK, -1),
                                   wuq_rope[:, :, swap].reshape(MLA_Q_RANK, -1)], axis=1).astype(BF16)
            wukv = mla_w_ukv[j].reshape(MLA_KV_RANK, MLA_HEADS, MLA_D_NOPE + MLA_D_V)
            wukv = jnp.concatenate([wukv[:, :, :MLA_D_NOPE].reshape(MLA_KV_RANK, -1),
                                    wukv[:, :, MLA_D_NOPE:].reshape(MLA_KV_RANK, -1)], axis=1).astype(BF16)
            weights = (wd, mla_q_norm[j].reshape(1, -1), mla_kv_norm[j].reshape(1, -1), wuq, wukv)
            cos, sin = _rope_tables(lat_len)
            tables = (jnp.tile(cos, (1, MLA_HEADS)), jnp.tile(sin, (1, MLA_HEADS)), cos, sin)

            qc, kc, vc, ckv, krope = _mla_project(x, 0, n_ctx_seq, ctx_len, mods, weights, None)
            new_ckv.append(ckv.reshape(n_ctx_seq, ctx_len, MLA_KV_RANK))
            new_krope.append(krope.reshape(n_ctx_seq, ctx_len, MLA_D_ROPE))
            ql, kl, vl, _, _ = _mla_project(x, n_ctx, n_lat_seq, lat_len, mods, weights, tables)
            k_cache, v_cache = _mla_cache_keys(cache_mla_ckv[:, j], cache_mla_krope[:, j], wukv)
            a_ctx = _attention(qc, [kc], [vc]).reshape(n_ctx, -1)
            a_lat = _attention(ql, [k_cache, kl], [v_cache, vl]).reshape(n_lat_seq * lat_len, -1)
            a = jnp.concatenate([a_ctx, a_lat], axis=0)
            x = _out_proj_layer(x, mods, a, mla_w_o[j].astype(BF16), lng, lnb)
        else:
            proj = _gla_project(x, mods, gla_w_in[j], gla_wa1[j], gla_wa2[j], gla_ba[j])
            zero_state = jnp.zeros((n_ctx_seq, GLA_HEADS, GLA_DV, GLA_DK), F32)
            lat_state = jnp.swapaxes(state_gla[:, j], -1, -2)
            o_dir, s_ctx = [], []
            for d, reverse in enumerate((False, True)):
                oc, sc = _gla_scan(proj, 0, n_ctx_seq, ctx_len, zero_state, reverse)
                ol, _ = _gla_scan(proj, n_ctx, n_lat_seq, lat_len, lat_state[:, d], reverse)
                o_dir.append(jnp.concatenate([oc, ol], axis=0))
                s_ctx.append(jnp.swapaxes(sc, -1, -2))
            new_gla.append(jnp.stack(s_ctx, axis=1))
            x = _gla_out_layer(x, mods, o_dir[0], o_dir[1], proj, gla_norm[j], gla_w_o[j], lng, lnb)

        lng, lnb = ln_g[i, 1], ln_b[i, 1]
        f = i // 2
        if i % 2 == 0:
            x = _ffn_layer(x, mods, ffn_w13[f].astype(BF16), ffn_w2[f].astype(BF16), lng, lnb)
        else:
            x = _moe_layer(x, mods, moe_router[f], moe_w13[f].astype(BF16), moe_w2[f].astype(BF16), lng, lnb)

    y_prompt = x[:n_ctx].reshape(n_ctx_seq, ctx_len, D_MODEL)
    y_sample = x[n_ctx:].reshape(n_lat_seq, lat_len, D_MODEL)
    return (y_prompt, y_sample, jnp.stack(new_ckv, axis=1), jnp.stack(new_krope, axis=1),
            jnp.stack(new_gla, axis=1))
```

```python
import functools

import jax
import jax.numpy as jnp
from jax import lax
from jax.experimental import pallas as pl
from jax.experimental.pallas import tpu as pltpu

F32 = jnp.float32
BF16 = jnp.bfloat16
HIGHEST = lax.Precision.HIGHEST

D_MODEL = 1024
DEPTH = 4
GRID_W = 64
DEEPNORM_ALPHA = (2.0 * DEPTH) ** 0.25
LN_EPS = 1e-5
RMS_EPS = 1e-6
GROUP_TOKENS = 4096

CHUNK_A = 128
GMLP_WIDTH = 2 * D_MODEL
GMLP_GROUPS = 8
GMLP_GROUP_DIM = GMLP_WIDTH // GMLP_GROUPS

MLA_HEADS = 8
MLA_Q_RANK = D_MODEL // 2
MLA_KV_RANK = D_MODEL // 4
MLA_D_NOPE = 128
MLA_D_ROPE = 64
MLA_D_V = 128
MLA_D_QK = MLA_D_NOPE + MLA_D_ROPE
MLA_D_VP = 2 * MLA_D_V
LOG2_E = 1.4426950408889634
ROPE_BASE = 10000.0

GLA_HEADS = 4
GLA_DK = 128
GLA_DV = 256
GLA_DECAY_RANK = 16
GLA_GATE_NORM = 16.0
GLA_CHUNK = 64
GLA_DQK = GLA_HEADS * GLA_DK
GLA_DVT = GLA_HEADS * GLA_DV
GLA_PROJ_W = 2 * GLA_DQK + 2 * GLA_DVT + 2 * GLA_DQK

D_FF = 2816
N_EXPERTS = 8
TOP_K = 2
D_FF_EXPERT = 3584

VMEM_LIMIT_BYTES = 56 * 1024 * 1024
ADA_ROWS = 16
ADA_TN = 1536
TM_GMLP = 256
TM_FFN = 512
TF_FFN = 1408
TM_PROJ = 512
TQ_ATTN = 1024
TQ_SUB_ATTN = 256
TM_OUT = 512
TB_GLA = 512
TM_ROUTE = 512
TM_EXPERT = 1024
TF_EXPERT = 896
TM_COMBINE = 512
DMA_ISSUE_UNROLL = 8


def _cparams(*sem):
    return pltpu.CompilerParams(dimension_semantics=sem, vmem_limit_bytes=VMEM_LIMIT_BYTES)


def _layer_norm(y, g, b):
    mu = jnp.mean(y, axis=-1, keepdims=True)
    d = y - mu
    var = jnp.mean(d * d, axis=-1, keepdims=True)
    return d * lax.rsqrt(var + LN_EPS) * g + b


def _rms_norm(y, g):
    return y * lax.rsqrt(jnp.mean(y * y, axis=-1, keepdims=True) + RMS_EPS) * g


def _modulate(x, mod_ref, shift_row):
    return x * (1.0 + mod_ref[shift_row + 1:shift_row + 2, :]) + mod_ref[shift_row:shift_row + 1, :]


def _post_norm(x, out, mod_ref, gate_row, lng_ref, lnb_ref):
    y = DEEPNORM_ALPHA * x + mod_ref[gate_row:gate_row + 1, :] * out
    return _layer_norm(y, lng_ref[...], lnb_ref[...])


def _dot(a, b):
    return jnp.dot(a, b, preferred_element_type=F32)


def _dot_nt(a, b):
    return lax.dot_general(a, b, (((1,), (1,)), ((), ())), preferred_element_type=F32)


def _dot_tn(a, b):
    return lax.dot_general(a, b, (((0,), (0,)), ((), ())), preferred_element_type=F32)


def _mod_spec(tm):
    return pl.BlockSpec((None, 6, D_MODEL), lambda i, *_: ((i * tm) // GROUP_TOKENS, 0, 0))


def _full_spec(shape):
    nd = len(shape)
    return pl.BlockSpec(shape, lambda *_: (0,) * nd)


def _ada_kernel(c_ref, w_ref, b_ref, o_ref):
    c = c_ref[...]
    s = c * jax.nn.sigmoid(c)
    o_ref[...] = jnp.dot(s, w_ref[...], precision=HIGHEST, preferred_element_type=F32) + b_ref[...]


def _adaln_all(cond, ada_w, ada_b):
    n_out = 6 * D_MODEL
    return pl.pallas_call(
        _ada_kernel,
        grid=(DEPTH, n_out // ADA_TN),
        in_specs=[
            pl.BlockSpec((ADA_ROWS, D_MODEL), lambda l, j: (0, 0)),
            pl.BlockSpec((None, D_MODEL, ADA_TN), lambda l, j: (l, 0, j)),
            pl.BlockSpec((None, 1, ADA_TN), lambda l, j: (l, 0, j)),
        ],
        out_specs=pl.BlockSpec((None, ADA_ROWS, ADA_TN), lambda l, j: (l, 0, j)),
        out_shape=jax.ShapeDtypeStruct((DEPTH, ADA_ROWS, n_out), F32),
        compiler_params=_cparams("parallel", "parallel"),
        name="adaln",
    )(cond, ada_w, ada_b.reshape(DEPTH, 1, n_out))


def _gmlp_kernel(x_ref, mod_ref, win_ref, bin_ref, vg_ref, vb_ref, ws_ref, bs_ref, wout_ref,
                 lng_ref, lnb_ref, o_ref, u_ref, vn_ref, p_ref):
    x = x_ref[...]
    hb = _modulate(x, mod_ref, 0).astype(BF16)
    v = jax.nn.gelu(_dot(hb, win_ref[:, GMLP_WIDTH:]) + bin_ref[:, GMLP_WIDTH:])
    vn_ref[...] = _layer_norm(v, vg_ref[...], vb_ref[...]).astype(BF16)
    u_ref[...] = jax.nn.gelu(_dot(hb, win_ref[:, :GMLP_WIDTH]) + bin_ref[:, :GMLP_WIDTH])
    for c in range(x.shape[0] // CHUNK_A):
        rows = slice(c * CHUNK_A, (c + 1) * CHUNK_A)
        for g in range(GMLP_GROUPS):
            cols = slice(g * GMLP_GROUP_DIM, (g + 1) * GMLP_GROUP_DIM)
            mixed = _dot(ws_ref[g], vn_ref[rows, cols]) + bs_ref[g]
            p_ref[rows, cols] = (u_ref[rows, cols] * mixed).astype(BF16)
    out = _dot(p_ref[...], wout_ref[...])
    o_ref[...] = _post_norm(x, out, mod_ref, 2, lng_ref, lnb_ref)


def _gmlp_layer(x, mods, w_in, b_in, v_g, v_b, w_s, b_s, w_out, ln_g, ln_b):
    n = x.shape[0]
    tm = TM_GMLP
    row = pl.BlockSpec((tm, D_MODEL), lambda i: (i, 0))
    return pl.pallas_call(
        _gmlp_kernel,
        grid=(n // tm,),
        in_specs=[
            row, _mod_spec(tm),
            _full_spec((D_MODEL, 2 * GMLP_WIDTH)), _full_spec((1, 2 * GMLP_WIDTH)),
            _full_spec((1, GMLP_WIDTH)), _full_spec((1, GMLP_WIDTH)),
            _full_spec((GMLP_GROUPS, CHUNK_A, CHUNK_A)), _full_spec((GMLP_GROUPS, CHUNK_A, 1)),
            _full_spec((GMLP_WIDTH, D_MODEL)),
            _full_spec((1, D_MODEL)), _full_spec((1, D_MODEL)),
        ],
        out_specs=row,
        out_shape=jax.ShapeDtypeStruct((n, D_MODEL), F32),
        scratch_shapes=[pltpu.VMEM((tm, GMLP_WIDTH), F32), pltpu.VMEM((tm, GMLP_WIDTH), BF16),
                        pltpu.VMEM((tm, GMLP_WIDTH), BF16)],
        compiler_params=_cparams("parallel"),
        name="gmlp",
    )(x, mods, w_in.astype(BF16), b_in.reshape(1, -1), v_g.reshape(1, -1), v_b.reshape(1, -1),
      w_s.astype(BF16), b_s.reshape(GMLP_GROUPS, CHUNK_A, 1), w_out.astype(BF16),
      ln_g.reshape(1, -1), ln_b.reshape(1, -1))


def _ffn_kernel(x_ref, mod_ref, w1_ref, w3_ref, w2_ref, lng_ref, lnb_ref, o_ref, hb_ref, acc_ref):
    j = pl.program_id(1)

    @pl.when(j == 0)
    def _():
        hb_ref[...] = _modulate(x_ref[...], mod_ref, 3).astype(BF16)
        acc_ref[...] = jnp.zeros_like(acc_ref)

    hb = hb_ref[...]
    a = _dot(hb, w1_ref[...])
    b = _dot(hb, w3_ref[...])
    acc_ref[...] += _dot((a * jax.nn.sigmoid(a) * b).astype(BF16), w2_ref[...])

    @pl.when(j == pl.num_programs(1) - 1)
    def _():
        o_ref[...] = _post_norm(x_ref[...], acc_ref[...], mod_ref, 5, lng_ref, lnb_ref)


def _ffn_layer(x, mods, w13, w2, ln_g, ln_b):
    n = x.shape[0]
    tm, tf = TM_FFN, TF_FFN
    nf = D_FF // tf
    row = pl.BlockSpec((tm, D_MODEL), lambda i, j: (i, 0))
    return pl.pallas_call(
        _ffn_kernel,
        grid=(n // tm, nf),
        in_specs=[
            row, _mod_spec(tm),
            pl.BlockSpec((D_MODEL, tf), lambda i, j: (0, j)),
            pl.BlockSpec((D_MODEL, tf), lambda i, j: (0, j + nf)),
            pl.BlockSpec((tf, D_MODEL), lambda i, j: (j, 0)),
            _full_spec((1, D_MODEL)), _full_spec((1, D_MODEL)),
        ],
        out_specs=row,
        out_shape=jax.ShapeDtypeStruct((n, D_MODEL), F32),
        scratch_shapes=[pltpu.VMEM((tm, D_MODEL), BF16), pltpu.VMEM((tm, D_MODEL), F32)],
        compiler_params=_cparams("parallel", "arbitrary"),
        name="ffn",
    )(x, mods, w13, w13, w2, ln_g.reshape(1, -1), ln_b.reshape(1, -1))


def _rope_swap_perm():
    j = jnp.arange(MLA_D_ROPE)
    half = MLA_D_ROPE // 4
    return jnp.where((j % (2 * half)) < half, j + half, j - half)


def _rope_tables(n_tokens):
    n_freq = MLA_D_ROPE // 4
    t = jnp.arange(n_tokens)
    pos_r = (t // GRID_W).astype(F32)[:, None]
    pos_c = (t % GRID_W).astype(F32)[:, None]
    inv = ROPE_BASE ** (-jnp.arange(n_freq, dtype=F32) / n_freq)
    ang_r, ang_c = pos_r * inv, pos_c * inv
    cos = jnp.concatenate([jnp.cos(ang_r), jnp.cos(ang_r), jnp.cos(ang_c), jnp.cos(ang_c)], axis=-1)
    sin = jnp.concatenate([-jnp.sin(ang_r), jnp.sin(ang_r), -jnp.sin(ang_c), jnp.sin(ang_c)], axis=-1)
    return cos, sin


def _write_heads(q_ref, k_ref, v_ref, q_nope, q_rope, kv, k_rope):
    scale = MLA_D_QK ** -0.5 * LOG2_E
    k_rope = k_rope.astype(BF16)
    ones = jnp.ones((kv.shape[0], MLA_D_VP - MLA_D_V), BF16)
    for h in range(MLA_HEADS):
        nope = slice(h * MLA_D_NOPE, (h + 1) * MLA_D_NOPE)
        if q_ref is not None:
            q_ref[h, :, :MLA_D_NOPE] = (q_nope[:, nope] * scale).astype(BF16)
            q_ref[h, :, MLA_D_NOPE:] = (q_rope[:, h * MLA_D_ROPE:(h + 1) * MLA_D_ROPE] * scale).astype(BF16)
        k_ref[h, :, :MLA_D_NOPE] = kv[:, nope].astype(BF16)
        k_ref[h, :, MLA_D_NOPE:] = k_rope
        vcol = MLA_HEADS * MLA_D_NOPE + h * MLA_D_V
        v_ref[h, :, :MLA_D_V] = kv[:, vcol:vcol + MLA_D_V].astype(BF16)
        v_ref[h, :, MLA_D_V:] = ones


def _mla_proj_kernel(*refs, rope):
    if rope:
        (x_ref, mod_ref, wd_ref, qg_ref, kvg_ref, wuq_ref, wukv_ref, cq_ref, sq_ref, ck_ref, sk_ref,
         q_ref, k_ref, v_ref, ckv_ref, kr_ref) = refs
    else:
        (x_ref, mod_ref, wd_ref, qg_ref, kvg_ref, wuq_ref, wukv_ref,
         q_ref, k_ref, v_ref, ckv_ref, kr_ref) = refs
    hb = _modulate(x_ref[...], mod_ref, 0).astype(BF16)
    c = _dot(hb, wd_ref[...])
    r0 = MLA_Q_RANK + MLA_KV_RANK
    c_q = _rms_norm(c[:, :MLA_Q_RANK], qg_ref[...]).astype(BF16)
    c_kv = _rms_norm(c[:, MLA_Q_RANK:r0], kvg_ref[...])
    k_rope = c[:, r0:r0 + MLA_D_ROPE]
    ckv_ref[...] = c_kv
    kr_ref[...] = k_rope
    q = _dot(c_q, wuq_ref[...])
    n_nope = MLA_HEADS * MLA_D_NOPE
    n_rope = MLA_HEADS * MLA_D_ROPE
    q_rope = q[:, n_nope:n_nope + n_rope]
    if rope:
        q_rope = q_rope * cq_ref[...] + q[:, n_nope + n_rope:] * sq_ref[...]
        k_rope = k_rope * ck_ref[...] + c[:, r0 + MLA_D_ROPE:] * sk_ref[...]
    kv = _dot(c_kv.astype(BF16), wukv_ref[...])
    _write_heads(q_ref, k_ref, v_ref, q[:, :n_nope], q_rope, kv, k_rope)


def _mla_project(x, row_offset, n_seq, seq_len, mods, weights, tables):
    wd, qg, kvg, wuq, wukv = weights
    tm = min(TM_PROJ, seq_len)
    per_seq = seq_len // tm
    n = n_seq * seq_len
    off = row_offset // tm
    rope = tables is not None
    in_specs = [
        pl.BlockSpec((tm, D_MODEL), lambda i: (off + i, 0)),
        pl.BlockSpec((None, 6, D_MODEL), lambda i: ((row_offset + i * tm) // GROUP_TOKENS, 0, 0)),
        _full_spec(wd.shape), _full_spec(qg.shape), _full_spec(kvg.shape),
        _full_spec(wuq.shape), _full_spec(wukv.shape),
    ]
    args = [x, mods, wd, qg, kvg, wuq, wukv]
    if rope:
        pos = lambda i: (i % per_seq, 0)
        in_specs += [pl.BlockSpec((tm, MLA_HEADS * MLA_D_ROPE), pos)] * 2 + [pl.BlockSpec((tm, MLA_D_ROPE), pos)] * 2
        args += list(tables)
    head_spec = lambda d: pl.BlockSpec((None, MLA_HEADS, tm, d), lambda i: (i // per_seq, 0, i % per_seq, 0))
    head_shape = lambda d: jax.ShapeDtypeStruct((n_seq, MLA_HEADS, seq_len, d), BF16)
    return pl.pallas_call(
        functools.partial(_mla_proj_kernel, rope=rope),
        grid=(n // tm,),
        in_specs=in_specs,
        out_specs=[head_spec(MLA_D_QK), head_spec(MLA_D_QK), head_spec(MLA_D_VP),
                   pl.BlockSpec((tm, MLA_KV_RANK), lambda i: (i, 0)),
                   pl.BlockSpec((tm, MLA_D_ROPE), lambda i: (i, 0))],
        out_shape=[head_shape(MLA_D_QK), head_shape(MLA_D_QK), head_shape(MLA_D_VP),
                   jax.ShapeDtypeStruct((n, MLA_KV_RANK), F32),
                   jax.ShapeDtypeStruct((n, MLA_D_ROPE), F32)],
        compiler_params=_cparams("parallel"),
        name="mla_proj_rope" if rope else "mla_proj",
    )(*args)


def _mla_cache_kernel(ckv_ref, kr_ref, wukv_ref, k_ref, v_ref):
    kv = _dot(ckv_ref[...].astype(BF16), wukv_ref[...])
    _write_heads(None, k_ref, v_ref, None, None, kv, kr_ref[...])


def _mla_cache_keys(ckv, krope, wukv):
    b, l, _ = ckv.shape
    head_spec = lambda d: pl.BlockSpec((None, MLA_HEADS, l, d), lambda i: (i, 0, 0, 0))
    return pl.pallas_call(
        _mla_cache_kernel,
        grid=(b,),
        in_specs=[pl.BlockSpec((None, l, MLA_KV_RANK), lambda i: (i, 0, 0)),
                  pl.BlockSpec((None, l, MLA_D_ROPE), lambda i: (i, 0, 0)),
                  _full_spec(wukv.shape)],
        out_specs=[head_spec(MLA_D_QK), head_spec(MLA_D_VP)],
        out_shape=[jax.ShapeDtypeStruct((b, MLA_HEADS, l, MLA_D_QK), BF16),
                   jax.ShapeDtypeStruct((b, MLA_HEADS, l, MLA_D_VP), BF16)],
        compiler_params=_cparams("parallel"),
        name="mla_cache_keys",
    )(ckv, krope, wukv)


def _attn_kernel(*refs, n_seg):
    q_ref = refs[0]
    k_refs = refs[1:1 + n_seg]
    v_refs = refs[1 + n_seg:1 + 2 * n_seg]
    o_ref = refs[1 + 2 * n_seg]
    s_ref, p_ref = refs[2 + 2 * n_seg:]
    n_buf, sub, _ = s_ref.shape
    n_sub = q_ref.shape[0] // sub
    seg_cols = []
    col = 0
    for k_ref in k_refs:
        seg_cols.append(slice(col, col + k_ref.shape[0]))
        col += k_ref.shape[0]

    def scores(i):
        q = q_ref[i * sub:(i + 1) * sub, :]
        for k_ref, cols in zip(k_refs, seg_cols):
            s_ref[i % n_buf, :, cols] = _dot_nt(q, k_ref[...])

    def finish(i):
        s = s_ref[i % n_buf]
        m = jnp.max(s, axis=-1, keepdims=True)
        p_ref[i % n_buf] = jnp.exp2((s - m).astype(BF16))
        acc = sum(_dot(p_ref[i % n_buf, :, cols], v_ref[...]) for v_ref, cols in zip(v_refs, seg_cols))
        o_ref[i * sub:(i + 1) * sub, :] = (acc[:, :MLA_D_V] / acc[:, MLA_D_V:MLA_D_V + 1]).astype(BF16)

    scores(0)
    for i in range(n_sub):
        if i + 1 < n_sub:
            scores(i + 1)
        finish(i)


def _attention(q, ks, vs):
    b, h, t, _ = q.shape
    tq = min(TQ_ATTN, t)
    sub = min(TQ_SUB_ATTN, tq)
    n_buf = min(2, tq // sub)
    n_keys = sum(a.shape[2] for a in ks)
    seg = lambda a: pl.BlockSpec((None, None) + a.shape[2:], lambda bi, hi, qi: (bi, hi, 0, 0))
    return pl.pallas_call(
        functools.partial(_attn_kernel, n_seg=len(ks)),
        grid=(b, h, t // tq),
        in_specs=[pl.BlockSpec((None, None, tq, MLA_D_QK), lambda bi, hi, qi: (bi, hi, qi, 0))]
        + [seg(a) for a in ks] + [seg(a) for a in vs],
        out_specs=pl.BlockSpec((None, tq, MLA_D_V), lambda bi, hi, qi: (bi, qi, hi)),
        out_shape=jax.ShapeDtypeStruct((b, t, h * MLA_D_V), BF16),
        scratch_shapes=[pltpu.VMEM((n_buf, sub, n_keys), F32), pltpu.VMEM((n_buf, sub, n_keys), BF16)],
        compiler_params=_cparams("parallel", "parallel", "arbitrary"),
        name="mla_attention",
    )(q, *ks, *vs)


def _ctx_lat_specs(tm, width, n_ctx_tiles, col=0):
    return [pl.BlockSpec((tm, width), lambda i: (jnp.minimum(i, n_ctx_tiles - 1), col)),
            pl.BlockSpec((tm, width), lambda i: (jnp.maximum(i - n_ctx_tiles, 0), col))]


def _out_proj_kernel(x_ref, mod_ref, ac_ref, al_ref, w_ref, lng_ref, lnb_ref, o_ref, *, n_ctx_tiles):
    def finish(a_ref):
        out = _dot(a_ref[...], w_ref[...])
        o_ref[...] = _post_norm(x_ref[...], out, mod_ref, 2, lng_ref, lnb_ref)

    is_ctx = pl.program_id(0) < n_ctx_tiles
    pl.when(is_ctx)(lambda: finish(ac_ref))
    pl.when(jnp.logical_not(is_ctx))(lambda: finish(al_ref))


def _out_proj_layer(x, mods, a_ctx, a_lat, w_o, ln_g, ln_b):
    n = x.shape[0]
    tm = TM_OUT
    n_ctx_tiles = a_ctx.shape[0] // tm
    row = pl.BlockSpec((tm, D_MODEL), lambda i: (i, 0))
    return pl.pallas_call(
        functools.partial(_out_proj_kernel, n_ctx_tiles=n_ctx_tiles),
        grid=(n // tm,),
        in_specs=[row, _mod_spec(tm)] + _ctx_lat_specs(tm, a_ctx.shape[1], n_ctx_tiles)
        + [_full_spec(w_o.shape), _full_spec((1, D_MODEL)), _full_spec((1, D_MODEL))],
        out_specs=row,
        out_shape=jax.ShapeDtypeStruct((n, D_MODEL), F32),
        compiler_params=_cparams("parallel"),
        name="mla_out_proj",
    )(x, mods, a_ctx, a_lat, w_o, ln_g.reshape(1, -1), ln_b.reshape(1, -1))


def _gla_proj_kernel(x_ref, mod_ref, win_ref, wa2_ref, ba_ref, o_ref):
    h = _modulate(x_ref[...], mod_ref, 0).astype(BF16)
    n_qkvr = 2 * GLA_DQK + 2 * GLA_DVT
    proj = _dot(h, win_ref[...])
    o_ref[:, :GLA_DQK] = proj[:, :GLA_DQK] * (GLA_DK ** -0.5)
    o_ref[:, GLA_DQK:n_qkvr] = proj[:, GLA_DQK:n_qkvr]
    logit = _dot(proj[:, n_qkvr:].astype(BF16), wa2_ref[...]) + ba_ref[...]
    log_sig = jnp.minimum(logit, 0.0) - jnp.log(1.0 + jnp.exp(-jnp.abs(logit)))
    o_ref[:, n_qkvr:] = log_sig / GLA_GATE_NORM


def _gla_project(x, mods, w_in, wa1, wa2, ba):
    n = x.shape[0]
    tm = TM_PROJ
    lane = 128
    low_w = 2 * GLA_DECAY_RANK
    w_ext = jnp.concatenate([w_in, wa1[0], wa1[1], jnp.zeros((D_MODEL, lane - low_w), F32)], axis=1)
    zeros = jnp.zeros((GLA_DECAY_RANK, GLA_DQK), F32)
    wa2_blk = jnp.concatenate([jnp.concatenate([wa2[0], zeros], axis=1),
                               jnp.concatenate([zeros, wa2[1]], axis=1),
                               jnp.zeros((lane - low_w, 2 * GLA_DQK), F32)], axis=0)
    return pl.pallas_call(
        _gla_proj_kernel,
        grid=(n // tm,),
        in_specs=[pl.BlockSpec((tm, D_MODEL), lambda i: (i, 0)), _mod_spec(tm),
                  _full_spec(w_ext.shape), _full_spec(wa2_blk.shape), _full_spec((1, 2 * GLA_DQK))],
        out_specs=pl.BlockSpec((tm, GLA_PROJ_W), lambda i: (i, 0)),
        out_shape=jax.ShapeDtypeStruct((n, GLA_PROJ_W), F32),
        compiler_params=_cparams("parallel"),
        name="gla_proj",
    )(x, mods, w_ext.astype(BF16), wa2_blk.astype(BF16), ba.reshape(1, -1))


def _chunk_cumsum(g, reverse):
    n = g.shape[0]
    pos = lax.broadcasted_iota(jnp.int32, g.shape, 0) % GLA_CHUNK
    b = g
    shift = 1
    while shift < GLA_CHUNK:
        if reverse:
            moved = pltpu.roll(b, n - shift, 0)
            keep = pos < GLA_CHUNK - shift
        else:
            moved = pltpu.roll(b, shift, 0)
            keep = pos >= shift
        b = b + jnp.where(keep, moved, 0.0)
        shift *= 2
    return b


def _gla_scan_kernel(q_ref, k_ref, v_ref, g_ref, s0_ref, o_ref, sfin_ref, st_ref, *, reverse):
    i = pl.program_id(2)

    @pl.when(i == 0)
    def _():
        st_ref[...] = s0_ref[...]

    tb = q_ref.shape[0]
    b_all = _chunk_cumsum(g_ref[...], reverse)
    ii = lax.broadcasted_iota(jnp.int32, (GLA_CHUNK, GLA_CHUNK), 0)
    jj = lax.broadcasted_iota(jnp.int32, (GLA_CHUNK, GLA_CHUNK), 1)
    mask = (jj >= ii) if reverse else (jj <= ii)
    chunks = range(tb // GLA_CHUNK)
    for c in (reversed(chunks) if reverse else chunks):
        rows = slice(c * GLA_CHUNK, (c + 1) * GLA_CHUNK)
        b = b_all[rows]
        b_tot = b[:1] if reverse else b[GLA_CHUNK - 1:]
        q = q_ref[rows, :]
        k = k_ref[rows, :]
        v = v_ref[rows, :].astype(BF16)
        qe = (q * jnp.exp(b)).astype(BF16)
        ke = (k * jnp.exp(-b)).astype(BF16)
        kd = (k * jnp.exp(b_tot - b)).astype(BF16)
        a = jnp.where(mask, _dot_nt(qe, ke), 0.0).astype(BF16)
        st = st_ref[...]
        o_ref[rows, :] = _dot(a, v) + _dot_nt(qe, st.astype(BF16))
        st_ref[...] = st * jnp.exp(b_tot) + _dot_tn(v, kd)

    @pl.when(i == pl.num_programs(2) - 1)
    def _():
        sfin_ref[...] = st_ref[...]


def _gla_scan(proj, row_offset, n_seq, seq_len, s0_t, reverse):
    tb = min(TB_GLA, seq_len)
    nb = seq_len // tb
    off = row_offset // tb
    d = 1 if reverse else 0

    def blk(s, i):
        return off + s * nb + ((nb - 1 - i) if reverse else i)

    def oblk(s, i):
        return s * nb + ((nb - 1 - i) if reverse else i)

    v_col0 = 2 * GLA_DQK // GLA_DV
    g_col0 = (2 * GLA_DQK + 2 * GLA_DVT) // GLA_DK + d * GLA_HEADS
    state_spec = pl.BlockSpec((None, None, GLA_DV, GLA_DK), lambda s, h, i: (s, h, 0, 0))
    return pl.pallas_call(
        functools.partial(_gla_scan_kernel, reverse=reverse),
        grid=(n_seq, GLA_HEADS, nb),
        in_specs=[
            pl.BlockSpec((tb, GLA_DK), lambda s, h, i: (blk(s, i), h)),
            pl.BlockSpec((tb, GLA_DK), lambda s, h, i: (blk(s, i), GLA_HEADS + h)),
            pl.BlockSpec((tb, GLA_DV), lambda s, h, i: (blk(s, i), v_col0 + h)),
            pl.BlockSpec((tb, GLA_DK), lambda s, h, i: (blk(s, i), g_col0 + h)),
            state_spec,
        ],
        out_specs=[pl.BlockSpec((tb, GLA_DV), lambda s, h, i: (oblk(s, i), h)), state_spec],
        out_shape=[jax.ShapeDtypeStruct((n_seq * seq_len, GLA_DVT), F32),
                   jax.ShapeDtypeStruct((n_seq, GLA_HEADS, GLA_DV, GLA_DK), F32)],
        scratch_shapes=[pltpu.VMEM((GLA_DV, GLA_DK), F32)],
        compiler_params=_cparams("parallel", "parallel", "arbitrary"),
        name="gla_scan_bwd" if reverse else "gla_scan_fwd",
    )(proj, proj, proj, proj, s0_t)


def _gla_out_kernel(x_ref, mod_ref, ofc_ref, ofl_ref, obc_ref, obl_ref, r_ref, ng_ref, wo_ref, lng_ref, lnb_ref,
                    o_ref, p_ref, *, n_ctx_tiles):
    def finish(of_ref, ob_ref):
        for h in range(GLA_HEADS):
            cols = slice(h * GLA_DV, (h + 1) * GLA_DV)
            o = _rms_norm(of_ref[:, cols] + ob_ref[:, cols], ng_ref[...])
            r = r_ref[:, cols]
            p_ref[:, cols] = (o * (r * jax.nn.sigmoid(r))).astype(BF16)
        out = _dot(p_ref[...], wo_ref[...])
        o_ref[...] = _post_norm(x_ref[...], out, mod_ref, 2, lng_ref, lnb_ref)

    is_ctx = pl.program_id(0) < n_ctx_tiles
    pl.when(is_ctx)(lambda: finish(ofc_ref, obc_ref))
    pl.when(jnp.logical_not(is_ctx))(lambda: finish(ofl_ref, obl_ref))


def _gla_out_layer(x, mods, o_fwd, o_bwd, proj, norm_g, w_o, ln_g, ln_b):
    n = x.shape[0]
    tm = TM_OUT
    n_ctx_tiles = o_fwd[0].shape[0] // tm
    row = pl.BlockSpec((tm, D_MODEL), lambda i: (i, 0))
    r_block = (2 * GLA_DQK + GLA_DVT) // GLA_DVT
    return pl.pallas_call(
        functools.partial(_gla_out_kernel, n_ctx_tiles=n_ctx_tiles),
        grid=(n // tm,),
        in_specs=[row, _mod_spec(tm)] + _ctx_lat_specs(tm, GLA_DVT, n_ctx_tiles) * 2
        + [pl.BlockSpec((tm, GLA_DVT), lambda i: (i, r_block)),
           _full_spec((1, GLA_DV)), _full_spec(w_o.shape),
           _full_spec((1, D_MODEL)), _full_spec((1, D_MODEL))],
        out_specs=row,
        out_shape=jax.ShapeDtypeStruct((n, D_MODEL), F32),
        scratch_shapes=[pltpu.VMEM((tm, GLA_DVT), BF16)],
        compiler_params=_cparams("parallel"),
        name="gla_out_proj",
    )(x, mods, *o_fwd, *o_bwd, proj, norm_g.reshape(1, -1), w_o.astype(BF16),
      ln_g.reshape(1, -1), ln_b.reshape(1, -1))


def _router_kernel(x_ref, mod_ref, wr_ref, idx_ref, wt_ref):
    h = _modulate(x_ref[...], mod_ref, 3)
    logits = lax.dot_general(wr_ref[...], h, (((1,), (1,)), ((), ())), precision=HIGHEST,
                             preferred_element_type=F32)
    e = lax.broadcasted_iota(jnp.int32, logits.shape, 0)
    m1 = jnp.max(logits, axis=0, keepdims=True)
    i1 = jnp.min(jnp.where(logits == m1, e, N_EXPERTS), axis=0, keepdims=True)
    rest = jnp.where(e == i1, -jnp.inf, logits)
    m2 = jnp.max(rest, axis=0, keepdims=True)
    i2 = jnp.min(jnp.where(rest == m2, e, N_EXPERTS), axis=0, keepdims=True)
    z = jnp.exp(m2 - m1)
    idx_ref[0:1, :] = i1
    idx_ref[1:2, :] = i2
    wt_ref[0:1, :] = 1.0 / (1.0 + z)
    wt_ref[1:2, :] = z / (1.0 + z)


def _route(x, mods, w_router):
    n = x.shape[0]
    tm = TM_ROUTE
    row = pl.BlockSpec((tm, D_MODEL), lambda i: (i, 0))
    pair = pl.BlockSpec((TOP_K, tm), lambda i: (0, i))
    return pl.pallas_call(
        _router_kernel,
        grid=(n // tm,),
        in_specs=[row, _mod_spec(tm), _full_spec((N_EXPERTS, D_MODEL))],
        out_specs=[pair, pair],
        out_shape=[jax.ShapeDtypeStruct((TOP_K, n), jnp.int32),
                   jax.ShapeDtypeStruct((TOP_K, n), F32)],
        compiler_params=_cparams("parallel"),
        name="moe_router",
    )(x, mods, w_router.T)


def _dispatch_plan(expert_idx, n_tiles):
    n = expert_idx.shape[1]
    flat = expert_idx.reshape(-1)
    onehot = (flat[:, None] == jnp.arange(N_EXPERTS)[None, :]).astype(jnp.int32)
    running = jnp.cumsum(onehot, axis=0)
    rank = jnp.sum(running * onehot, axis=1) - 1
    counts = running[-1]
    tiles = (counts + TM_EXPERT - 1) // TM_EXPERT
    tile_end = jnp.cumsum(tiles)
    tile_start = tile_end - tiles
    slot = jnp.sum(onehot * tile_start[None, :], axis=1) * TM_EXPERT + rank
    tile_id = jnp.arange(n_tiles)
    tile_active = (tile_id < tile_end[-1]).astype(jnp.int32)
    last_tile = jnp.minimum(tile_id, tile_end[-1] - 1)
    tile_expert = jnp.sum((tile_end[None, :] <= last_tile[:, None]).astype(jnp.int32), axis=1)
    return slot.reshape(TOP_K, n).astype(jnp.int32), tile_expert.astype(jnp.int32), tile_active


def _row_copies_wait(hbm_ref, buf_ref, sem):
    pltpu.make_async_copy(hbm_ref.at[pl.ds(0, buf_ref.shape[0]), :], buf_ref, sem).wait()


def _dispatch_kernel(s1_ref, s2_ref, x_ref, mod_ref, zero_hbm, xs_hbm, hbuf_ref, sems):
    del zero_hbm
    i, n_steps = pl.program_id(0), pl.num_programs(0)
    tm = x_ref.shape[0]
    b = i % 2

    def wait_rows(buf):
        for _ in range(TOP_K):
            _row_copies_wait(xs_hbm, hbuf_ref.at[buf], sems.at[buf])

    @pl.when(i >= 2)
    def _():
        wait_rows(b)

    hbuf_ref[b] = _modulate(x_ref[...], mod_ref, 3)
    base = i * tm

    def issue(r, carry):
        for s_ref in (s1_ref, s2_ref):
            pltpu.make_async_copy(hbuf_ref.at[b, pl.ds(r, 1), :],
                                  xs_hbm.at[pl.ds(s_ref[base + r], 1), :], sems.at[b]).start()
        return carry

    lax.fori_loop(0, tm, issue, 0, unroll=DMA_ISSUE_UNROLL)

    @pl.when(i == n_steps - 1)
    def _():
        wait_rows(b)

    @pl.when(jnp.logical_and(i == n_steps - 1, i >= 1))
    def _():
        wait_rows(1 - b)


def _dispatch(x, mods, slot_of, n_slots):
    n = x.shape[0]
    tm = TM_ROUTE
    return pl.pallas_call(
        _dispatch_kernel,
        grid_spec=pltpu.PrefetchScalarGridSpec(
            num_scalar_prefetch=2,
            grid=(n // tm,),
            in_specs=[pl.BlockSpec((tm, D_MODEL), lambda i, s1, s2: (i, 0)),
                      pl.BlockSpec((None, 6, D_MODEL), lambda i, s1, s2: ((i * tm) // GROUP_TOKENS, 0, 0)),
                      pl.BlockSpec(memory_space=pl.ANY)],
            out_specs=pl.BlockSpec(memory_space=pl.ANY),
            scratch_shapes=[pltpu.VMEM((2, tm, D_MODEL), F32), pltpu.SemaphoreType.DMA((2,))],
        ),
        out_shape=jax.ShapeDtypeStruct((n_slots, D_MODEL), F32),
        input_output_aliases={4: 0},
        compiler_params=_cparams("arbitrary"),
        name="moe_dispatch",
    )(slot_of[0], slot_of[1], x, mods, jnp.zeros((n_slots, D_MODEL), F32))


def _expert_kernel(te_ref, ta_ref, x_ref, w1_ref, w3_ref, w2_ref, o_ref, xb_ref, acc_ref):
    i, j = pl.program_id(0), pl.program_id(1)
    last = pl.num_programs(1) - 1
    active = ta_ref[i] == 1

    @pl.when(jnp.logical_and(j == 0, active))
    def _():
        xb_ref[...] = x_ref[...].astype(BF16)

    @pl.when(active)
    def _():
        x = xb_ref[...]
        a = _dot(x, w1_ref[...])
        b = _dot(x, w3_ref[...])
        part = _dot((a * jax.nn.sigmoid(a) * b).astype(BF16), w2_ref[...])

        @pl.when(j == 0)
        def _():
            acc_ref[...] = part

        @pl.when(j > 0)
        def _():
            acc_ref[...] += part

    @pl.when(jnp.logical_and(j == last, active))
    def _():
        o_ref[...] = acc_ref[...]

    @pl.when(jnp.logical_and(j == last, jnp.logical_not(active)))
    def _():
        o_ref[...] = jnp.zeros_like(o_ref)


def _expert_ffn(xs, tile_expert, tile_active, w13, w2):
    tm, tf = TM_EXPERT, TF_EXPERT
    nf = D_FF_EXPERT // tf
    n_slots = xs.shape[0]
    row = pl.BlockSpec((tm, D_MODEL), lambda i, j, te, ta: (i, 0))
    return pl.pallas_call(
        _expert_kernel,
        grid_spec=pltpu.PrefetchScalarGridSpec(
            num_scalar_prefetch=2,
            grid=(n_slots // tm, nf),
            in_specs=[
                row,
                pl.BlockSpec((None, D_MODEL, tf), lambda i, j, te, ta: (te[i], 0, j)),
                pl.BlockSpec((None, D_MODEL, tf), lambda i, j, te, ta: (te[i], 0, j + nf)),
                pl.BlockSpec((None, tf, D_MODEL), lambda i, j, te, ta: (te[i], j, 0)),
            ],
            out_specs=row,
            scratch_shapes=[pltpu.VMEM((tm, D_MODEL), BF16), pltpu.VMEM((tm, D_MODEL), F32)],
        ),
        out_shape=jax.ShapeDtypeStruct((n_slots, D_MODEL), F32),
        compiler_params=_cparams("parallel", "arbitrary"),
        name="moe_experts",
    )(tile_expert, tile_active, xs, w13, w13, w2)


def _combine_kernel(s1_ref, s2_ref, x_ref, mod_ref, wt_ref, y_hbm, lng_ref, lnb_ref, o_ref, ybuf_ref, sems):
    i, n_steps = pl.program_id(0), pl.num_programs(0)
    tm = x_ref.shape[0]

    def issue_tile(tile, buf):
        base = tile * tm

        def issue(r, carry):
            for k, s_ref in enumerate((s1_ref, s2_ref)):
                pltpu.make_async_copy(y_hbm.at[pl.ds(s_ref[base + r], 1), :],
                                      ybuf_ref.at[buf, k, pl.ds(r, 1), :], sems.at[buf]).start()
            return carry

        lax.fori_loop(0, tm, issue, 0, unroll=DMA_ISSUE_UNROLL)

    @pl.when(i == 0)
    def _():
        issue_tile(0, 0)

    @pl.when(i + 1 < n_steps)
    def _():
        issue_tile(i + 1, (i + 1) % 2)

    b = i % 2
    for k in range(TOP_K):
        _row_copies_wait(y_hbm, ybuf_ref.at[b, k], sems.at[b])
    y = wt_ref[:, 0:1] * ybuf_ref[b, 0] + wt_ref[:, 1:2] * ybuf_ref[b, 1]
    o_ref[...] = _post_norm(x_ref[...], y, mod_ref, 5, lng_ref, lnb_ref)


def _combine(x, mods, slot_of, weights_t, ys, ln_g, ln_b):
    n = x.shape[0]
    tm = TM_COMBINE
    row = pl.BlockSpec((tm, D_MODEL), lambda i, s1, s2: (i, 0))
    return pl.pallas_call(
        _combine_kernel,
        grid_spec=pltpu.PrefetchScalarGridSpec(
            num_scalar_prefetch=2,
            grid=(n // tm,),
            in_specs=[row,
                      pl.BlockSpec((None, 6, D_MODEL), lambda i, s1, s2: ((i * tm) // GROUP_TOKENS, 0, 0)),
                      pl.BlockSpec((tm, TOP_K), lambda i, s1, s2: (i, 0)),
                      pl.BlockSpec(memory_space=pl.ANY),
                      pl.BlockSpec((1, D_MODEL), lambda i, s1, s2: (0, 0)),
                      pl.BlockSpec((1, D_MODEL), lambda i, s1, s2: (0, 0))],
            out_specs=row,
            scratch_shapes=[pltpu.VMEM((2, TOP_K, tm, D_MODEL), F32), pltpu.SemaphoreType.DMA((2,))],
        ),
        out_shape=jax.ShapeDtypeStruct((n, D_MODEL), F32),
        compiler_params=_cparams("arbitrary"),
        name="moe_combine",
    )(slot_of[0], slot_of[1], x, mods, weights_t, ys, ln_g.reshape(1, -1), ln_b.reshape(1, -1))


def _moe_layer(x, mods, w_router, w13, w2, ln_g, ln_b):
    n = x.shape[0]
    n_tiles = (TOP_K * n) // TM_EXPERT + N_EXPERTS
    expert_idx, weights = _route(x, mods, w_router)
    slot_of, tile_expert, tile_active = _dispatch_plan(expert_idx, n_tiles)
    xs = _dispatch(x, mods, slot_of, n_tiles * TM_EXPERT)
    ys = _expert_ffn(xs, tile_expert, tile_active, w13, w2)
    return _combine(x, mods, slot_of, weights.T, ys, ln_g, ln_b)


def kernel(x_prompt, x_sample, cache_mla_ckv, cache_mla_krope, state_gla, c, c_ctx, ada_w, ada_b, ln_g, ln_b, gmlp_w_in, gmlp_b_in, gmlp_v_g, gmlp_v_b, gmlp_w_s, gmlp_b_s, gmlp_w_out, mla_w_dqkv, mla_q_norm, mla_kv_norm, mla_w_uq, mla_w_ukv, mla_w_o, gla_w_in, gla_wa1, gla_wa2, gla_ba, gla_norm, gla_w_o, ffn_w13, ffn_w2, moe_router, moe_w13, moe_w2):
    n_ctx_seq, ctx_len, _ = x_prompt.shape
    n_lat_seq, lat_len, _ = x_sample.shape
    n_ctx = n_ctx_seq * ctx_len
    assert n_ctx == GROUP_TOKENS and lat_len == GROUP_TOKENS
    x = jnp.concatenate([x_prompt.reshape(n_ctx, D_MODEL), x_sample.reshape(-1, D_MODEL)], axis=0)

    cond = jnp.concatenate([c_ctx[None, :], c, jnp.zeros((ADA_ROWS - 1 - n_lat_seq, D_MODEL), F32)], axis=0)
    n_groups = 1 + n_lat_seq
    mods_all = _adaln_all(cond, ada_w, ada_b)[:, :n_groups].reshape(DEPTH, n_groups, 6, D_MODEL)

    new_ckv, new_krope, new_gla = [], [], []
    for i in range(DEPTH):
        mods = mods_all[i]
        kind, j = i % 3, i // 3
        lng, lnb = ln_g[i, 0], ln_b[i, 0]
        if kind == 0:
            x = _gmlp_layer(x, mods, gmlp_w_in[j], gmlp_b_in[j], gmlp_v_g[j], gmlp_v_b[j], gmlp_w_s[j],
                            gmlp_b_s[j], gmlp_w_out[j], lng, lnb)
        elif kind == 1:
            swap = _rope_swap_perm()
            wd = mla_w_dqkv[j]
            r0 = MLA_Q_RANK + MLA_KV_RANK
            wd = jnp.concatenate([wd, wd[:, r0:][:, swap]], axis=1).astype(BF16)
            wuq = mla_w_uq[j].reshape(MLA_Q_RANK, MLA_HEADS, MLA_D_QK)
            wuq_rope = wuq[:, :, MLA_D_NOPE:]
            wuq = jnp.concatenate([wuq[:, :, :MLA_D_NOPE].reshape(MLA_Q_RANK, -1),
                                   wuq_rope.reshape(MLA_Q_RANK, -1),
                                   wuq_rope[:, :, swap].reshape(MLA_Q_RANK, -1)], axis=1).astype(BF16)
            wukv = mla_w_ukv[j].reshape(MLA_KV_RANK, MLA_HEADS, MLA_D_NOPE + MLA_D_V)
            wukv = jnp.concatenate([wukv[:, :, :MLA_D_NOPE].reshape(MLA_KV_RANK, -1),
                                    wukv[:, :, MLA_D_NOPE:].reshape(MLA_KV_RANK, -1)], axis=1).astype(BF16)
            weights = (wd, mla_q_norm[j].reshape(1, -1), mla_kv_norm[j].reshape(1, -1), wuq, wukv)
            cos, sin = _rope_tables(lat_len)
            tables = (jnp.tile(cos, (1, MLA_HEADS)), jnp.tile(sin, (1, MLA_HEADS)), cos, sin)

            qc, kc, vc, ckv, krope = _mla_project(x, 0, n_ctx_seq, ctx_len, mods, weights, None)
            new_ckv.append(ckv.reshape(n_ctx_seq, ctx_len, MLA_KV_RANK))
            new_krope.append(krope.reshape(n_ctx_seq, ctx_len, MLA_D_ROPE))
            ql, kl, vl, _, _ = _mla_project(x, n_ctx, n_lat_seq, lat_len, mods, weights, tables)
            k_cache, v_cache = _mla_cache_keys(cache_mla_ckv[:, j], cache_mla_krope[:, j], wukv)
            a_ctx = _attention(qc, [kc], [vc]).reshape(n_ctx, -1)
            a_lat = _attention(ql, [k_cache, kl], [v_cache, vl]).reshape(n_lat_seq * lat_len, -1)
            x = _out_proj_layer(x, mods, a_ctx, a_lat, mla_w_o[j].astype(BF16), lng, lnb)
        else:
            proj = _gla_project(x, mods, gla_w_in[j], gla_wa1[j], gla_wa2[j], gla_ba[j])
            zero_state = jnp.zeros((n_ctx_seq, GLA_HEADS, GLA_DV, GLA_DK), F32)
            lat_state = jnp.swapaxes(state_gla[:, j], -1, -2)
            o_dir, s_ctx = [], []
            for d, reverse in enumerate((False, True)):
                oc, sc = _gla_scan(proj, 0, n_ctx_seq, ctx_len, zero_state, reverse)
                ol, _ = _gla_scan(proj, n_ctx, n_lat_seq, lat_len, lat_state[:, d], reverse)
                o_dir.append((oc, ol))
                s_ctx.append(jnp.swapaxes(sc, -1, -2))
            new_gla.append(jnp.stack(s_ctx, axis=1))
            x = _gla_out_layer(x, mods, o_dir[0], o_dir[1], proj, gla_norm[j], gla_w_o[j], lng, lnb)

        lng, lnb = ln_g[i, 1], ln_b[i, 1]
        f = i // 2
        if i % 2 == 0:
            x = _ffn_layer(x, mods, ffn_w13[f].astype(BF16), ffn_w2[f].astype(BF16), lng, lnb)
        else:
            x = _moe_layer(x, mods, moe_router[f], moe_w13[f].astype(BF16), moe_w2[f].astype(BF16), lng, lnb)

    y_prompt = x[:n_ctx].reshape(n_ctx_seq, ctx_len, D_MODEL)
    y_sample = x[n_ctx:].reshape(n_lat_seq, lat_len, D_MODEL)
    return (y_prompt, y_sample, jnp.stack(new_ckv, axis=1), jnp.stack(new_krope, axis=1),
            jnp.stack(new_gla, axis=1))
```

```python
import functools

import jax
import jax.numpy as jnp
from jax import lax
from jax.experimental import pallas as pl
from jax.experimental.pallas import tpu as pltpu

F32 = jnp.float32
BF16 = jnp.bfloat16
HIGHEST = lax.Precision.HIGHEST

D_MODEL = 1024
DEPTH = 4
GRID_W = 64
DEEPNORM_ALPHA = (2.0 * DEPTH) ** 0.25
LN_EPS = 1e-5
RMS_EPS = 1e-6
GROUP_TOKENS = 4096

CHUNK_A = 128
GMLP_WIDTH = 2 * D_MODEL
GMLP_GROUPS = 8
GMLP_GROUP_DIM = GMLP_WIDTH // GMLP_GROUPS

MLA_HEADS = 8
MLA_Q_RANK = D_MODEL // 2
MLA_KV_RANK = D_MODEL // 4
MLA_D_NOPE = 128
MLA_D_ROPE = 64
MLA_D_V = 128
MLA_D_QK = MLA_D_NOPE + MLA_D_ROPE
MLA_D_VP = 2 * MLA_D_V
LOG2_E = 1.4426950408889634
ROPE_BASE = 10000.0

GLA_HEADS = 4
GLA_DK = 128
GLA_DV = 256
GLA_DECAY_RANK = 16
GLA_GATE_NORM = 16.0
GLA_CHUNK = 64
GLA_DQK = GLA_HEADS * GLA_DK
GLA_DVT = GLA_HEADS * GLA_DV
GLA_PROJ_W = 2 * GLA_DQK + 2 * GLA_DVT + 2 * GLA_DQK

D_FF = 2816
N_EXPERTS = 8
TOP_K = 2
D_FF_EXPERT = 3584

VMEM_LIMIT_BYTES = 56 * 1024 * 1024
ADA_ROWS = 16
ADA_TN = 1536
TM_GMLP = 256
TM_FFN = 512
TM_PROJ = 512
TQ_ATTN = 1024
TQ_SUB_ATTN = 256
TM_OUT = 512
TB_GLA = 512
TM_ROUTE = 512
TM_EXPERT = 1024
TF_EXPERT = 512
TM_COMBINE = 512
DMA_ISSUE_UNROLL = 8


def _cparams(*sem):
    return pltpu.CompilerParams(dimension_semantics=sem, vmem_limit_bytes=VMEM_LIMIT_BYTES)


def _layer_norm(y, g, b):
    mu = jnp.mean(y, axis=-1, keepdims=True)
    d = y - mu
    var = jnp.mean(d * d, axis=-1, keepdims=True)
    return d * lax.rsqrt(var + LN_EPS) * g + b


def _rms_norm(y, g):
    return y * lax.rsqrt(jnp.mean(y * y, axis=-1, keepdims=True) + RMS_EPS) * g


def _modulate(x, mod_ref, shift_row):
    return x * (1.0 + mod_ref[shift_row + 1:shift_row + 2, :]) + mod_ref[shift_row:shift_row + 1, :]


def _post_norm(x, out, mod_ref, gate_row, lng_ref, lnb_ref):
    y = DEEPNORM_ALPHA * x + mod_ref[gate_row:gate_row + 1, :] * out
    return _layer_norm(y, lng_ref[...], lnb_ref[...])


def _dot(a, b):
    return jnp.dot(a, b, preferred_element_type=F32)


def _dot_nt(a, b):
    return lax.dot_general(a, b, (((1,), (1,)), ((), ())), preferred_element_type=F32)


def _dot_tn(a, b):
    return lax.dot_general(a, b, (((0,), (0,)), ((), ())), preferred_element_type=F32)


def _mod_spec(tm):
    return pl.BlockSpec((None, 6, D_MODEL), lambda i, *_: ((i * tm) // GROUP_TOKENS, 0, 0))


def _full_spec(shape):
    nd = len(shape)
    return pl.BlockSpec(shape, lambda *_: (0,) * nd, pipeline_mode=pl.Buffered(1))


def _ada_kernel(c_ref, w_ref, b_ref, o_ref):
    c = c_ref[...]
    s = c * jax.nn.sigmoid(c)
    o_ref[...] = jnp.dot(s, w_ref[...], precision=HIGHEST, preferred_element_type=F32) + b_ref[...]


def _adaln_all(cond, ada_w, ada_b):
    n_out = 6 * D_MODEL
    return pl.pallas_call(
        _ada_kernel,
        grid=(DEPTH, n_out // ADA_TN),
        in_specs=[
            pl.BlockSpec((ADA_ROWS, D_MODEL), lambda l, j: (0, 0)),
            pl.BlockSpec((None, D_MODEL, ADA_TN), lambda l, j: (l, 0, j)),
            pl.BlockSpec((None, 1, ADA_TN), lambda l, j: (l, 0, j)),
        ],
        out_specs=pl.BlockSpec((None, ADA_ROWS, ADA_TN), lambda l, j: (l, 0, j)),
        out_shape=jax.ShapeDtypeStruct((DEPTH, ADA_ROWS, n_out), F32),
        compiler_params=_cparams("parallel", "parallel"),
        name="adaln",
    )(cond, ada_w, ada_b.reshape(DEPTH, 1, n_out))


def _gmlp_kernel(*refs, n_ctx_tiles):
    if n_ctx_tiles is None:
        x = refs[0][...]
        refs = refs[1:]
    else:
        x = jnp.where(pl.program_id(0) < n_ctx_tiles, refs[0][...], refs[1][...])
        refs = refs[2:]
    (mod_ref, win_ref, bin_ref, vg_ref, vb_ref, ws_ref, bs_ref, wout_ref, lng_ref, lnb_ref,
     o_ref, u_ref, vn_ref, p_ref) = refs
    hb = _modulate(x, mod_ref, 0).astype(BF16)
    v = jax.nn.gelu(_dot(hb, win_ref[:, GMLP_WIDTH:]) + bin_ref[:, GMLP_WIDTH:])
    vn_ref[...] = _layer_norm(v, vg_ref[...], vb_ref[...]).astype(BF16)
    u_ref[...] = jax.nn.gelu(_dot(hb, win_ref[:, :GMLP_WIDTH]) + bin_ref[:, :GMLP_WIDTH])
    for c in range(x.shape[0] // CHUNK_A):
        rows = slice(c * CHUNK_A, (c + 1) * CHUNK_A)
        for g in range(GMLP_GROUPS):
            cols = slice(g * GMLP_GROUP_DIM, (g + 1) * GMLP_GROUP_DIM)
            mixed = _dot(ws_ref[g], vn_ref[rows, cols]) + bs_ref[g]
            p_ref[rows, cols] = (u_ref[rows, cols] * mixed).astype(BF16)
    out = _dot(p_ref[...], wout_ref[...])
    o_ref[...] = _post_norm(x, out, mod_ref, 2, lng_ref, lnb_ref)


def _gmlp_layer(x, mods, w_in, b_in, v_g, v_b, w_s, b_s, w_out, ln_g, ln_b):
    tm = TM_GMLP
    row = pl.BlockSpec((tm, D_MODEL), lambda i: (i, 0))
    if isinstance(x, tuple):
        n_ctx_tiles = x[0].shape[0] // tm
        n = x[0].shape[0] + x[1].shape[0]
        x_specs = _ctx_lat_specs(tm, D_MODEL, n_ctx_tiles)
    else:
        n_ctx_tiles = None
        n = x.shape[0]
        x, x_specs = (x,), [row]
    return pl.pallas_call(
        functools.partial(_gmlp_kernel, n_ctx_tiles=n_ctx_tiles),
        grid=(n // tm,),
        in_specs=x_specs + [
            _mod_spec(tm),
            _full_spec((D_MODEL, 2 * GMLP_WIDTH)), _full_spec((1, 2 * GMLP_WIDTH)),
            _full_spec((1, GMLP_WIDTH)), _full_spec((1, GMLP_WIDTH)),
            _full_spec((GMLP_GROUPS, CHUNK_A, CHUNK_A)), _full_spec((GMLP_GROUPS, CHUNK_A, 1)),
            _full_spec((GMLP_WIDTH, D_MODEL)),
            _full_spec((1, D_MODEL)), _full_spec((1, D_MODEL)),
        ],
        out_specs=row,
        out_shape=jax.ShapeDtypeStruct((n, D_MODEL), F32),
        scratch_shapes=[pltpu.VMEM((tm, GMLP_WIDTH), F32), pltpu.VMEM((tm, GMLP_WIDTH), BF16),
                        pltpu.VMEM((tm, GMLP_WIDTH), BF16)],
        compiler_params=_cparams("parallel"),
        name="gmlp",
    )(*x, mods, w_in.astype(BF16), b_in.reshape(1, -1), v_g.reshape(1, -1), v_b.reshape(1, -1),
      w_s.astype(BF16), b_s.reshape(GMLP_GROUPS, CHUNK_A, 1), w_out.astype(BF16),
      ln_g.reshape(1, -1), ln_b.reshape(1, -1))


def _ffn_kernel(x_ref, mod_ref, w13_ref, w2_ref, lng_ref, lnb_ref, o_ref):
    x = x_ref[...]
    hb = _modulate(x, mod_ref, 3).astype(BF16)
    a = _dot(hb, w13_ref[:, :D_FF])
    b = _dot(hb, w13_ref[:, D_FF:])
    out = _dot((a * jax.nn.sigmoid(a) * b).astype(BF16), w2_ref[...])
    o_ref[...] = _post_norm(x, out, mod_ref, 5, lng_ref, lnb_ref)


def _ffn_layer(x, mods, w13, w2, ln_g, ln_b):
    n = x.shape[0]
    tm = TM_FFN
    row = pl.BlockSpec((tm, D_MODEL), lambda i: (i, 0))
    return pl.pallas_call(
        _ffn_kernel,
        grid=(n // tm,),
        in_specs=[row, _mod_spec(tm), _full_spec(w13.shape), _full_spec(w2.shape),
                  _full_spec((1, D_MODEL)), _full_spec((1, D_MODEL))],
        out_specs=row,
        out_shape=jax.ShapeDtypeStruct((n, D_MODEL), F32),
        compiler_params=_cparams("parallel"),
        name="ffn",
    )(x, mods, w13, w2, ln_g.reshape(1, -1), ln_b.reshape(1, -1))


def _rope_swap_perm():
    j = jnp.arange(MLA_D_ROPE)
    half = MLA_D_ROPE // 4
    return jnp.where((j % (2 * half)) < half, j + half, j - half)


def _rope_tables(n_tokens):
    n_freq = MLA_D_ROPE // 4
    t = jnp.arange(n_tokens)
    pos_r = (t // GRID_W).astype(F32)[:, None]
    pos_c = (t % GRID_W).astype(F32)[:, None]
    inv = ROPE_BASE ** (-jnp.arange(n_freq, dtype=F32) / n_freq)
    ang_r, ang_c = pos_r * inv, pos_c * inv
    cos = jnp.concatenate([jnp.cos(ang_r), jnp.cos(ang_r), jnp.cos(ang_c), jnp.cos(ang_c)], axis=-1)
    sin = jnp.concatenate([-jnp.sin(ang_r), jnp.sin(ang_r), -jnp.sin(ang_c), jnp.sin(ang_c)], axis=-1)
    return cos, sin


def _write_heads(q_ref, k_ref, v_ref, q_nope, q_rope, kv, k_rope):
    scale = MLA_D_QK ** -0.5 * LOG2_E
    k_rope = k_rope.astype(BF16)
    ones = jnp.ones((kv.shape[0], MLA_D_VP - MLA_D_V), BF16)
    for h in range(MLA_HEADS):
        nope = slice(h * MLA_D_NOPE, (h + 1) * MLA_D_NOPE)
        if q_ref is not None:
            q_ref[h, :, :MLA_D_NOPE] = (q_nope[:, nope] * scale).astype(BF16)
            q_ref[h, :, MLA_D_NOPE:] = (q_rope[:, h * MLA_D_ROPE:(h + 1) * MLA_D_ROPE] * scale).astype(BF16)
        k_ref[h, :, :MLA_D_NOPE] = kv[:, nope].astype(BF16)
        k_ref[h, :, MLA_D_NOPE:] = k_rope
        vcol = MLA_HEADS * MLA_D_NOPE + h * MLA_D_V
        v_ref[h, :, :MLA_D_V] = kv[:, vcol:vcol + MLA_D_V].astype(BF16)
        v_ref[h, :, MLA_D_V:] = ones


def _mla_proj_kernel(*refs, rope):
    if rope:
        (x_ref, mod_ref, wd_ref, qg_ref, kvg_ref, wuq_ref, wukv_ref, cq_ref, sq_ref, ck_ref, sk_ref,
         q_ref, k_ref, v_ref, ckv_ref, kr_ref) = refs
    else:
        (x_ref, mod_ref, wd_ref, qg_ref, kvg_ref, wuq_ref, wukv_ref,
         q_ref, k_ref, v_ref, ckv_ref, kr_ref) = refs
    hb = _modulate(x_ref[...], mod_ref, 0).astype(BF16)
    c = _dot(hb, wd_ref[...])
    r0 = MLA_Q_RANK + MLA_KV_RANK
    c_q = _rms_norm(c[:, :MLA_Q_RANK], qg_ref[...]).astype(BF16)
    c_kv = _rms_norm(c[:, MLA_Q_RANK:r0], kvg_ref[...])
    k_rope = c[:, r0:r0 + MLA_D_ROPE]
    ckv_ref[...] = c_kv
    kr_ref[...] = k_rope
    q = _dot(c_q, wuq_ref[...])
    n_nope = MLA_HEADS * MLA_D_NOPE
    n_rope = MLA_HEADS * MLA_D_ROPE
    q_rope = q[:, n_nope:n_nope + n_rope]
    if rope:
        q_rope = q_rope * cq_ref[...] + q[:, n_nope + n_rope:] * sq_ref[...]
        k_rope = k_rope * ck_ref[...] + c[:, r0 + MLA_D_ROPE:] * sk_ref[...]
    kv = _dot(c_kv.astype(BF16), wukv_ref[...])
    _write_heads(q_ref, k_ref, v_ref, q[:, :n_nope], q_rope, kv, k_rope)


def _mla_project(x, row_offset, n_seq, seq_len, mods, weights, tables):
    wd, qg, kvg, wuq, wukv = weights
    tm = min(TM_PROJ, seq_len)
    per_seq = seq_len // tm
    n = n_seq * seq_len
    off = row_offset // tm
    rope = tables is not None
    in_specs = [
        pl.BlockSpec((tm, D_MODEL), lambda i: (off + i, 0)),
        pl.BlockSpec((None, 6, D_MODEL), lambda i: ((row_offset + i * tm) // GROUP_TOKENS, 0, 0)),
        _full_spec(wd.shape), _full_spec(qg.shape), _full_spec(kvg.shape),
        _full_spec(wuq.shape), _full_spec(wukv.shape),
    ]
    args = [x, mods, wd, qg, kvg, wuq, wukv]
    if rope:
        pos = lambda i: (i % per_seq, 0)
        in_specs += [pl.BlockSpec((tm, MLA_HEADS * MLA_D_ROPE), pos)] * 2 + [pl.BlockSpec((tm, MLA_D_ROPE), pos)] * 2
        args += list(tables)
    head_spec = lambda d: pl.BlockSpec((None, MLA_HEADS, tm, d), lambda i: (i // per_seq, 0, i % per_seq, 0))
    head_shape = lambda d: jax.ShapeDtypeStruct((n_seq, MLA_HEADS, seq_len, d), BF16)
    return pl.pallas_call(
        functools.partial(_mla_proj_kernel, rope=rope),
        grid=(n // tm,),
        in_specs=in_specs,
        out_specs=[head_spec(MLA_D_QK), head_spec(MLA_D_QK), head_spec(MLA_D_VP),
                   pl.BlockSpec((tm, MLA_KV_RANK), lambda i: (i, 0)),
                   pl.BlockSpec((tm, MLA_D_ROPE), lambda i: (i, 0))],
        out_shape=[head_shape(MLA_D_QK), head_shape(MLA_D_QK), head_shape(MLA_D_VP),
                   jax.ShapeDtypeStruct((n, MLA_KV_RANK), F32),
                   jax.ShapeDtypeStruct((n, MLA_D_ROPE), F32)],
        compiler_params=_cparams("parallel"),
        name="mla_proj_rope" if rope else "mla_proj",
    )(*args)


def _mla_cache_kernel(ckv_ref, kr_ref, wukv_ref, k_ref, v_ref):
    kv = _dot(ckv_ref[...].astype(BF16), wukv_ref[...])
    _write_heads(None, k_ref, v_ref, None, None, kv, kr_ref[...])


def _mla_cache_keys(ckv, krope, wukv):
    b, l, _ = ckv.shape
    head_spec = lambda d: pl.BlockSpec((None, MLA_HEADS, l, d), lambda i: (i, 0, 0, 0))
    return pl.pallas_call(
        _mla_cache_kernel,
        grid=(b,),
        in_specs=[pl.BlockSpec((None, l, MLA_KV_RANK), lambda i: (i, 0, 0)),
                  pl.BlockSpec((None, l, MLA_D_ROPE), lambda i: (i, 0, 0)),
                  _full_spec(wukv.shape)],
        out_specs=[head_spec(MLA_D_QK), head_spec(MLA_D_VP)],
        out_shape=[jax.ShapeDtypeStruct((b, MLA_HEADS, l, MLA_D_QK), BF16),
                   jax.ShapeDtypeStruct((b, MLA_HEADS, l, MLA_D_VP), BF16)],
        compiler_params=_cparams("parallel"),
        name="mla_cache_keys",
    )(ckv, krope, wukv)


def _attn_kernel(*refs, n_seg):
    q_ref = refs[0]
    k_refs = refs[1:1 + n_seg]
    v_refs = refs[1 + n_seg:1 + 2 * n_seg]
    o_ref = refs[1 + 2 * n_seg]
    s_ref, p_ref = refs[2 + 2 * n_seg:]
    n_buf, sub, _ = s_ref.shape
    n_sub = q_ref.shape[0] // sub
    seg_cols = []
    col = 0
    for k_ref in k_refs:
        seg_cols.append(slice(col, col + k_ref.shape[0]))
        col += k_ref.shape[0]

    def scores(i):
        q = q_ref[i * sub:(i + 1) * sub, :]
        for k_ref, cols in zip(k_refs, seg_cols):
            s_ref[i % n_buf, :, cols] = _dot_nt(q, k_ref[...])

    def finish(i):
        s = s_ref[i % n_buf]
        m = jnp.max(s, axis=-1, keepdims=True)
        p_ref[i % n_buf] = jnp.exp2((s - m).astype(BF16))
        acc = sum(_dot(p_ref[i % n_buf, :, cols], v_ref[...]) for v_ref, cols in zip(v_refs, seg_cols))
        o_ref[i * sub:(i + 1) * sub, :] = (acc[:, :MLA_D_V] / acc[:, MLA_D_V:MLA_D_V + 1]).astype(BF16)

    scores(0)
    for i in range(n_sub):
        if i + 1 < n_sub:
            scores(i + 1)
        finish(i)


def _attention(q, ks, vs):
    b, h, t, _ = q.shape
    tq = min(TQ_ATTN, t)
    sub = min(TQ_SUB_ATTN, tq)
    n_buf = min(2, tq // sub)
    n_keys = sum(a.shape[2] for a in ks)
    seg = lambda a: pl.BlockSpec((None, None) + a.shape[2:], lambda bi, hi, qi: (bi, hi, 0, 0))
    return pl.pallas_call(
        functools.partial(_attn_kernel, n_seg=len(ks)),
        grid=(b, h, t // tq),
        in_specs=[pl.BlockSpec((None, None, tq, MLA_D_QK), lambda bi, hi, qi: (bi, hi, qi, 0))]
        + [seg(a) for a in ks] + [seg(a) for a in vs],
        out_specs=pl.BlockSpec((None, tq, MLA_D_V), lambda bi, hi, qi: (bi, qi, hi)),
        out_shape=jax.ShapeDtypeStruct((b, t, h * MLA_D_V), BF16),
        scratch_shapes=[pltpu.VMEM((n_buf, sub, n_keys), F32), pltpu.VMEM((n_buf, sub, n_keys), BF16)],
        compiler_params=_cparams("parallel", "parallel", "arbitrary"),
        name="mla_attention",
    )(q, *ks, *vs)


def _ctx_lat_specs(tm, width, n_ctx_tiles, col=0):
    return [pl.BlockSpec((tm, width), lambda i: (jnp.minimum(i, n_ctx_tiles - 1), col)),
            pl.BlockSpec((tm, width), lambda i: (jnp.maximum(i - n_ctx_tiles, 0), col))]


def _out_proj_kernel(x_ref, mod_ref, ac_ref, al_ref, w_ref, lng_ref, lnb_ref, o_ref, *, n_ctx_tiles):
    def finish(a_ref):
        out = _dot(a_ref[...], w_ref[...])
        o_ref[...] = _post_norm(x_ref[...], out, mod_ref, 2, lng_ref, lnb_ref)

    is_ctx = pl.program_id(0) < n_ctx_tiles
    pl.when(is_ctx)(lambda: finish(ac_ref))
    pl.when(jnp.logical_not(is_ctx))(lambda: finish(al_ref))


def _out_proj_layer(x, mods, a_ctx, a_lat, w_o, ln_g, ln_b):
    n = x.shape[0]
    tm = TM_OUT
    n_ctx_tiles = a_ctx.shape[0] // tm
    row = pl.BlockSpec((tm, D_MODEL), lambda i: (i, 0))
    return pl.pallas_call(
        functools.partial(_out_proj_kernel, n_ctx_tiles=n_ctx_tiles),
        grid=(n // tm,),
        in_specs=[row, _mod_spec(tm)] + _ctx_lat_specs(tm, a_ctx.shape[1], n_ctx_tiles)
        + [_full_spec(w_o.shape), _full_spec((1, D_MODEL)), _full_spec((1, D_MODEL))],
        out_specs=row,
        out_shape=jax.ShapeDtypeStruct((n, D_MODEL), F32),
        compiler_params=_cparams("parallel"),
        name="mla_out_proj",
    )(x, mods, a_ctx, a_lat, w_o, ln_g.reshape(1, -1), ln_b.reshape(1, -1))


def _gla_proj_kernel(x_ref, mod_ref, win_ref, wa2_ref, ba_ref, o_ref):
    h = _modulate(x_ref[...], mod_ref, 0).astype(BF16)
    n_qkvr = 2 * GLA_DQK + 2 * GLA_DVT
    proj = _dot(h, win_ref[...])
    o_ref[:, :GLA_DQK] = proj[:, :GLA_DQK] * (GLA_DK ** -0.5)
    o_ref[:, GLA_DQK:n_qkvr] = proj[:, GLA_DQK:n_qkvr]
    logit = _dot(proj[:, n_qkvr:].astype(BF16), wa2_ref[...]) + ba_ref[...]
    log_sig = jnp.minimum(logit, 0.0) - jnp.log(1.0 + jnp.exp(-jnp.abs(logit)))
    o_ref[:, n_qkvr:] = log_sig / GLA_GATE_NORM


def _gla_project(x, mods, w_in, wa1, wa2, ba):
    n = x.shape[0]
    tm = TM_PROJ
    lane = 128
    low_w = 2 * GLA_DECAY_RANK
    w_ext = jnp.concatenate([w_in, wa1[0], wa1[1], jnp.zeros((D_MODEL, lane - low_w), F32)], axis=1)
    zeros = jnp.zeros((GLA_DECAY_RANK, GLA_DQK), F32)
    wa2_blk = jnp.concatenate([jnp.concatenate([wa2[0], zeros], axis=1),
                               jnp.concatenate([zeros, wa2[1]], axis=1),
                               jnp.zeros((lane - low_w, 2 * GLA_DQK), F32)], axis=0)
    return pl.pallas_call(
        _gla_proj_kernel,
        grid=(n // tm,),
        in_specs=[pl.BlockSpec((tm, D_MODEL), lambda i: (i, 0)), _mod_spec(tm),
                  _full_spec(w_ext.shape), _full_spec(wa2_blk.shape), _full_spec((1, 2 * GLA_DQK))],
        out_specs=pl.BlockSpec((tm, GLA_PROJ_W), lambda i: (i, 0)),
        out_shape=jax.ShapeDtypeStruct((n, GLA_PROJ_W), F32),
        compiler_params=_cparams("parallel"),
        name="gla_proj",
    )(x, mods, w_ext.astype(BF16), wa2_blk.astype(BF16), ba.reshape(1, -1))


def _chunk_cumsum(g, reverse):
    n = g.shape[0]
    pos = lax.broadcasted_iota(jnp.int32, g.shape, 0) % GLA_CHUNK
    b = g
    shift = 1
    while shift < GLA_CHUNK:
        if reverse:
            moved = pltpu.roll(b, n - shift, 0)
            keep = pos < GLA_CHUNK - shift
        else:
            moved = pltpu.roll(b, shift, 0)
            keep = pos >= shift
        b = b + jnp.where(keep, moved, 0.0)
        shift *= 2
    return b


def _gla_scan_kernel(qf_ref, kf_ref, vf_ref, gf_ref, qb_ref, kb_ref, vb_ref, gb_ref, s0_ref,
                     of_ref, ob_ref, sfin_ref, st_ref):
    i = pl.program_id(2)

    @pl.when(i == 0)
    def _():
        st_ref[...] = s0_ref[...]

    n_chunks = qf_ref.shape[0] // GLA_CHUNK
    ii = lax.broadcasted_iota(jnp.int32, (GLA_CHUNK, GLA_CHUNK), 0)
    jj = lax.broadcasted_iota(jnp.int32, (GLA_CHUNK, GLA_CHUNK), 1)
    dirs = ((0, False, qf_ref, kf_ref, vf_ref, of_ref, _chunk_cumsum(gf_ref[...], False), jj <= ii),
            (1, True, qb_ref, kb_ref, vb_ref, ob_ref, _chunk_cumsum(gb_ref[...], True), jj >= ii))
    for step in range(n_chunks):
        for d, reverse, q_ref, k_ref, v_ref, o_ref, b_all, mask in dirs:
            c = n_chunks - 1 - step if reverse else step
            rows = slice(c * GLA_CHUNK, (c + 1) * GLA_CHUNK)
            b = b_all[rows]
            b_tot = b[:1] if reverse else b[GLA_CHUNK - 1:]
            q = q_ref[rows, :]
            k = k_ref[rows, :]
            v = v_ref[rows, :].astype(BF16)
            qe = (q * jnp.exp(b)).astype(BF16)
            ke = (k * jnp.exp(-b)).astype(BF16)
            kd = (k * jnp.exp(b_tot - b)).astype(BF16)
            a = jnp.where(mask, _dot_nt(qe, ke), 0.0).astype(BF16)
            st = st_ref[d]
            o_ref[rows, :] = _dot(a, v) + _dot_nt(qe, st.astype(BF16))
            st_ref[d] = st * jnp.exp(b_tot) + _dot_tn(v, kd)

    @pl.when(i == pl.num_programs(2) - 1)
    def _():
        sfin_ref[...] = st_ref[...]


def _gla_scan(proj, row_offset, n_seq, seq_len, s0_t):
    tb = min(TB_GLA, seq_len)
    nb = seq_len // tb
    off = row_offset // tb
    v_col0 = 2 * GLA_DQK // GLA_DV
    g_col0 = (2 * GLA_DQK + 2 * GLA_DVT) // GLA_DK

    def in_specs(reverse):
        blk = (lambda s, i: off + s * nb + nb - 1 - i) if reverse else (lambda s, i: off + s * nb + i)
        g_col = g_col0 + (GLA_HEADS if reverse else 0)
        return [pl.BlockSpec((tb, GLA_DK), lambda s, h, i: (blk(s, i), h)),
                pl.BlockSpec((tb, GLA_DK), lambda s, h, i: (blk(s, i), GLA_HEADS + h)),
                pl.BlockSpec((tb, GLA_DV), lambda s, h, i: (blk(s, i), v_col0 + h)),
                pl.BlockSpec((tb, GLA_DK), lambda s, h, i: (blk(s, i), g_col + h))]

    state_spec = pl.BlockSpec((None, 2, None, GLA_DV, GLA_DK), lambda s, h, i: (s, 0, h, 0, 0))
    o_shape = jax.ShapeDtypeStruct((n_seq * seq_len, GLA_DVT), F32)
    return pl.pallas_call(
        _gla_scan_kernel,
        grid=(n_seq, GLA_HEADS, nb),
        in_specs=in_specs(False) + in_specs(True) + [state_spec],
        out_specs=[pl.BlockSpec((tb, GLA_DV), lambda s, h, i: (s * nb + i, h)),
                   pl.BlockSpec((tb, GLA_DV), lambda s, h, i: (s * nb + nb - 1 - i, h)),
                   state_spec],
        out_shape=[o_shape, o_shape, jax.ShapeDtypeStruct((n_seq, 2, GLA_HEADS, GLA_DV, GLA_DK), F32)],
        scratch_shapes=[pltpu.VMEM((2, GLA_DV, GLA_DK), F32)],
        compiler_params=_cparams("parallel", "parallel", "arbitrary"),
        name="gla_scan",
    )(*([proj] * 8), s0_t)


def _gla_out_kernel(x_ref, mod_ref, ofc_ref, ofl_ref, obc_ref, obl_ref, r_ref, ng_ref, wo_ref, lng_ref, lnb_ref,
                    o_ref, p_ref, *, n_ctx_tiles):
    def finish(of_ref, ob_ref):
        for h in range(GLA_HEADS):
            cols = slice(h * GLA_DV, (h + 1) * GLA_DV)
            o = _rms_norm(of_ref[:, cols] + ob_ref[:, cols], ng_ref[...])
            r = r_ref[:, cols]
            p_ref[:, cols] = (o * (r * jax.nn.sigmoid(r))).astype(BF16)
        out = _dot(p_ref[...], wo_ref[...])
        o_ref[...] = _post_norm(x_ref[...], out, mod_ref, 2, lng_ref, lnb_ref)

    is_ctx = pl.program_id(0) < n_ctx_tiles
    pl.when(is_ctx)(lambda: finish(ofc_ref, obc_ref))
    pl.when(jnp.logical_not(is_ctx))(lambda: finish(ofl_ref, obl_ref))


def _gla_out_layer(x, mods, o_fwd, o_bwd, proj, norm_g, w_o, ln_g, ln_b):
    n = x.shape[0]
    tm = TM_OUT
    n_ctx_tiles = o_fwd[0].shape[0] // tm
    row = pl.BlockSpec((tm, D_MODEL), lambda i: (i, 0))
    r_block = (2 * GLA_DQK + GLA_DVT) // GLA_DVT
    return pl.pallas_call(
        functools.partial(_gla_out_kernel, n_ctx_tiles=n_ctx_tiles),
        grid=(n // tm,),
        in_specs=[row, _mod_spec(tm)] + _ctx_lat_specs(tm, GLA_DVT, n_ctx_tiles) * 2
        + [pl.BlockSpec((tm, GLA_DVT), lambda i: (i, r_block)),
           _full_spec((1, GLA_DV)), _full_spec(w_o.shape),
           _full_spec((1, D_MODEL)), _full_spec((1, D_MODEL))],
        out_specs=row,
        out_shape=jax.ShapeDtypeStruct((n, D_MODEL), F32),
        scratch_shapes=[pltpu.VMEM((tm, GLA_DVT), BF16)],
        compiler_params=_cparams("parallel"),
        name="gla_out_proj",
    )(x, mods, *o_fwd, *o_bwd, proj, norm_g.reshape(1, -1), w_o.astype(BF16),
      ln_g.reshape(1, -1), ln_b.reshape(1, -1))


def _router_kernel(x_ref, mod_ref, wr_ref, idx_ref, wt_ref):
    h = _modulate(x_ref[...], mod_ref, 3)
    logits = lax.dot_general(wr_ref[...], h, (((1,), (1,)), ((), ())), precision=HIGHEST,
                             preferred_element_type=F32)
    e = lax.broadcasted_iota(jnp.int32, logits.shape, 0)
    m1 = jnp.max(logits, axis=0, keepdims=True)
    i1 = jnp.min(jnp.where(logits == m1, e, N_EXPERTS), axis=0, keepdims=True)
    rest = jnp.where(e == i1, -jnp.inf, logits)
    m2 = jnp.max(rest, axis=0, keepdims=True)
    i2 = jnp.min(jnp.where(rest == m2, e, N_EXPERTS), axis=0, keepdims=True)
    z = jnp.exp(m2 - m1)
    idx_ref[0:1, :] = i1
    idx_ref[1:2, :] = i2
    wt_ref[0:1, :] = 1.0 / (1.0 + z)
    wt_ref[1:2, :] = z / (1.0 + z)


def _route(x, mods, w_router):
    n = x.shape[0]
    tm = TM_ROUTE
    row = pl.BlockSpec((tm, D_MODEL), lambda i: (i, 0))
    pair = pl.BlockSpec((TOP_K, tm), lambda i: (0, i))
    return pl.pallas_call(
        _router_kernel,
        grid=(n // tm,),
        in_specs=[row, _mod_spec(tm), _full_spec((N_EXPERTS, D_MODEL))],
        out_specs=[pair, pair],
        out_shape=[jax.ShapeDtypeStruct((TOP_K, n), jnp.int32),
                   jax.ShapeDtypeStruct((TOP_K, n), F32)],
        compiler_params=_cparams("parallel"),
        name="moe_router",
    )(x, mods, w_router.T)


def _dispatch_plan(expert_idx, n_tiles):
    n = expert_idx.shape[1]
    flat = expert_idx.reshape(-1)
    onehot = (flat[:, None] == jnp.arange(N_EXPERTS)[None, :]).astype(jnp.int32)
    running = jnp.cumsum(onehot, axis=0)
    rank = jnp.sum(running * onehot, axis=1) - 1
    counts = running[-1]
    tiles = (counts + TM_EXPERT - 1) // TM_EXPERT
    tile_end = jnp.cumsum(tiles)
    tile_start = tile_end - tiles
    slot = jnp.sum(onehot * tile_start[None, :], axis=1) * TM_EXPERT + rank
    tile_id = jnp.arange(n_tiles)
    tile_active = (tile_id < tile_end[-1]).astype(jnp.int32)
    last_tile = jnp.minimum(tile_id, tile_end[-1] - 1)
    tile_expert = jnp.sum((tile_end[None, :] <= last_tile[:, None]).astype(jnp.int32), axis=1)
    return slot.reshape(TOP_K, n).astype(jnp.int32), tile_expert.astype(jnp.int32), tile_active


def _row_copies_wait(hbm_ref, buf_ref, sem):
    pltpu.make_async_copy(hbm_ref.at[pl.ds(0, buf_ref.shape[0]), :], buf_ref, sem).wait()


def _dispatch_kernel(s1_ref, s2_ref, x_ref, mod_ref, zero_hbm, xs_hbm, hbuf_ref, sems):
    del zero_hbm
    i, n_steps = pl.program_id(0), pl.num_programs(0)
    tm = x_ref.shape[0]
    b = i % 2

    def wait_rows(buf):
        for _ in range(TOP_K):
            _row_copies_wait(xs_hbm, hbuf_ref.at[buf], sems.at[buf])

    @pl.when(i >= 2)
    def _():
        wait_rows(b)

    hbuf_ref[b] = _modulate(x_ref[...], mod_ref, 3)
    base = i * tm

    def issue(r, carry):
        for s_ref in (s1_ref, s2_ref):
            pltpu.make_async_copy(hbuf_ref.at[b, pl.ds(r, 1), :],
                                  xs_hbm.at[pl.ds(s_ref[base + r], 1), :], sems.at[b]).start()
        return carry

    lax.fori_loop(0, tm, issue, 0, unroll=DMA_ISSUE_UNROLL)

    @pl.when(i == n_steps - 1)
    def _():
        wait_rows(b)

    @pl.when(jnp.logical_and(i == n_steps - 1, i >= 1))
    def _():
        wait_rows(1 - b)


def _dispatch(x, mods, slot_of, n_slots):
    n = x.shape[0]
    tm = TM_ROUTE
    return pl.pallas_call(
        _dispatch_kernel,
        grid_spec=pltpu.PrefetchScalarGridSpec(
            num_scalar_prefetch=2,
            grid=(n // tm,),
            in_specs=[pl.BlockSpec((tm, D_MODEL), lambda i, s1, s2: (i, 0)),
                      pl.BlockSpec((None, 6, D_MODEL), lambda i, s1, s2: ((i * tm) // GROUP_TOKENS, 0, 0)),
                      pl.BlockSpec(memory_space=pl.ANY)],
            out_specs=pl.BlockSpec(memory_space=pl.ANY),
            scratch_shapes=[pltpu.VMEM((2, tm, D_MODEL), F32), pltpu.SemaphoreType.DMA((2,))],
        ),
        out_shape=jax.ShapeDtypeStruct((n_slots, D_MODEL), F32),
        input_output_aliases={4: 0},
        compiler_params=_cparams("arbitrary"),
        name="moe_dispatch",
    )(slot_of[0], slot_of[1], x, mods, jnp.zeros((n_slots, D_MODEL), F32))


def _expert_kernel(te_ref, ta_ref, x_ref, w1_ref, w3_ref, w2_ref, o_ref, xb_ref, acc_ref):
    i, j = pl.program_id(0), pl.program_id(1)
    last = pl.num_programs(1) - 1
    active = ta_ref[i] == 1

    @pl.when(jnp.logical_and(j == 0, active))
    def _():
        xb_ref[...] = x_ref[...].astype(BF16)

    @pl.when(active)
    def _():
        x = xb_ref[...]
        a = _dot(x, w1_ref[...].astype(BF16))
        b = _dot(x, w3_ref[...].astype(BF16))
        part = _dot((a * jax.nn.sigmoid(a) * b).astype(BF16), w2_ref[...].astype(BF16))

        @pl.when(j == 0)
        def _():
            acc_ref[...] = part

        @pl.when(j > 0)
        def _():
            acc_ref[...] += part

    @pl.when(jnp.logical_and(j == last, active))
    def _():
        o_ref[...] = acc_ref[...]

    @pl.when(jnp.logical_and(j == last, jnp.logical_not(active)))
    def _():
        o_ref[...] = jnp.zeros_like(o_ref)


def _expert_ffn(xs, tile_expert, tile_active, w13, w2):
    tm, tf = TM_EXPERT, TF_EXPERT
    nf = D_FF_EXPERT // tf
    n_slots = xs.shape[0]
    row = pl.BlockSpec((tm, D_MODEL), lambda i, j, te, ta: (i, 0))
    return pl.pallas_call(
        _expert_kernel,
        grid_spec=pltpu.PrefetchScalarGridSpec(
            num_scalar_prefetch=2,
            grid=(n_slots // tm, nf),
            in_specs=[
                row,
                pl.BlockSpec((None, D_MODEL, tf), lambda i, j, te, ta: (te[i], 0, j)),
                pl.BlockSpec((None, D_MODEL, tf), lambda i, j, te, ta: (te[i], 0, j + nf)),
                pl.BlockSpec((None, tf, D_MODEL), lambda i, j, te, ta: (te[i], j, 0)),
            ],
            out_specs=row,
            scratch_shapes=[pltpu.VMEM((tm, D_MODEL), BF16), pltpu.VMEM((tm, D_MODEL), F32)],
        ),
        out_shape=jax.ShapeDtypeStruct((n_slots, D_MODEL), F32),
        compiler_params=_cparams("parallel", "arbitrary"),
        name="moe_experts",
    )(tile_expert, tile_active, xs, w13, w13, w2)


def _combine_kernel(s1_ref, s2_ref, x_ref, mod_ref, wt_ref, y_hbm, lng_ref, lnb_ref, *refs, n_ctx_tiles):
    ybuf_ref, sems = refs[-2:]
    o_refs = refs[:-2]
    i, n_steps = pl.program_id(0), pl.num_programs(0)
    tm = x_ref.shape[0]

    def issue_tile(tile, buf):
        base = tile * tm

        def issue(r, carry):
            for k, s_ref in enumerate((s1_ref, s2_ref)):
                pltpu.make_async_copy(y_hbm.at[pl.ds(s_ref[base + r], 1), :],
                                      ybuf_ref.at[buf, k, pl.ds(r, 1), :], sems.at[buf]).start()
            return carry

        lax.fori_loop(0, tm, issue, 0, unroll=DMA_ISSUE_UNROLL)

    @pl.when(i == 0)
    def _():
        issue_tile(0, 0)

    @pl.when(i + 1 < n_steps)
    def _():
        issue_tile(i + 1, (i + 1) % 2)

    b = i % 2
    for k in range(TOP_K):
        _row_copies_wait(y_hbm, ybuf_ref.at[b, k], sems.at[b])
    y = wt_ref[:, 0:1] * ybuf_ref[b, 0] + wt_ref[:, 1:2] * ybuf_ref[b, 1]
    out = _post_norm(x_ref[...], y, mod_ref, 5, lng_ref, lnb_ref)
    if n_ctx_tiles is None:
        o_refs[0][...] = out
    else:
        is_ctx = i < n_ctx_tiles

        @pl.when(is_ctx)
        def _():
            o_refs[0][...] = out

        @pl.when(jnp.logical_not(is_ctx))
        def _():
            o_refs[1][...] = out


def _combine(x, mods, slot_of, weights_t, ys, ln_g, ln_b, n_ctx_split=None):
    n = x.shape[0]
    tm = TM_COMBINE
    row = pl.BlockSpec((tm, D_MODEL), lambda i, s1, s2: (i, 0))
    if n_ctx_split is None:
        n_ctx_tiles = None
        out_specs, out_shape = row, jax.ShapeDtypeStruct((n, D_MODEL), F32)
    else:
        n_ctx_tiles = n_ctx_split // tm
        out_specs = [pl.BlockSpec((tm, D_MODEL), lambda i, s1, s2: (jnp.minimum(i, n_ctx_tiles - 1), 0)),
                     pl.BlockSpec((tm, D_MODEL), lambda i, s1, s2: (jnp.maximum(i - n_ctx_tiles, 0), 0))]
        out_shape = [jax.ShapeDtypeStruct((n_ctx_split, D_MODEL), F32),
                     jax.ShapeDtypeStruct((n - n_ctx_split, D_MODEL), F32)]
    return pl.pallas_call(
        functools.partial(_combine_kernel, n_ctx_tiles=n_ctx_tiles),
        grid_spec=pltpu.PrefetchScalarGridSpec(
            num_scalar_prefetch=2,
            grid=(n // tm,),
            in_specs=[row,
                      pl.BlockSpec((None, 6, D_MODEL), lambda i, s1, s2: ((i * tm) // GROUP_TOKENS, 0, 0)),
                      pl.BlockSpec((tm, TOP_K), lambda i, s1, s2: (i, 0)),
                      pl.BlockSpec(memory_space=pl.ANY),
                      pl.BlockSpec((1, D_MODEL), lambda i, s1, s2: (0, 0)),
                      pl.BlockSpec((1, D_MODEL), lambda i, s1, s2: (0, 0))],
            out_specs=out_specs,
            scratch_shapes=[pltpu.VMEM((2, TOP_K, tm, D_MODEL), F32), pltpu.SemaphoreType.DMA((2,))],
        ),
        out_shape=out_shape,
        compiler_params=_cparams("arbitrary"),
        name="moe_combine",
    )(slot_of[0], slot_of[1], x, mods, weights_t, ys, ln_g.reshape(1, -1), ln_b.reshape(1, -1))


def _moe_layer(x, mods, w_router, w13, w2, ln_g, ln_b, n_ctx_split=None):
    n = x.shape[0]
    n_tiles = (TOP_K * n) // TM_EXPERT + N_EXPERTS
    expert_idx, weights = _route(x, mods, w_router)
    slot_of, tile_expert, tile_active = _dispatch_plan(expert_idx, n_tiles)
    xs = _dispatch(x, mods, slot_of, n_tiles * TM_EXPERT)
    ys = _expert_ffn(xs, tile_expert, tile_active, w13, w2)
    return _combine(x, mods, slot_of, weights.T, ys, ln_g, ln_b, n_ctx_split)


def kernel(x_prompt, x_sample, cache_mla_ckv, cache_mla_krope, state_gla, c, c_ctx, ada_w, ada_b, ln_g, ln_b, gmlp_w_in, gmlp_b_in, gmlp_v_g, gmlp_v_b, gmlp_w_s, gmlp_b_s, gmlp_w_out, mla_w_dqkv, mla_q_norm, mla_kv_norm, mla_w_uq, mla_w_ukv, mla_w_o, gla_w_in, gla_wa1, gla_wa2, gla_ba, gla_norm, gla_w_o, ffn_w13, ffn_w2, moe_router, moe_w13, moe_w2):
    n_ctx_seq, ctx_len, _ = x_prompt.shape
    n_lat_seq, lat_len, _ = x_sample.shape
    n_ctx = n_ctx_seq * ctx_len
    assert n_ctx == GROUP_TOKENS and lat_len == GROUP_TOKENS
    x = (x_prompt.reshape(n_ctx, D_MODEL), x_sample.reshape(-1, D_MODEL))

    cond = jnp.concatenate([c_ctx[None, :], c, jnp.zeros((ADA_ROWS - 1 - n_lat_seq, D_MODEL), F32)], axis=0)
    n_groups = 1 + n_lat_seq
    mods_all = _adaln_all(cond, ada_w, ada_b)[:, :n_groups].reshape(DEPTH, n_groups, 6, D_MODEL)

    new_ckv, new_krope, new_gla = [], [], []
    for i in range(DEPTH):
        mods = mods_all[i]
        kind, j = i % 3, i // 3
        lng, lnb = ln_g[i, 0], ln_b[i, 0]
        if kind == 0:
            x = _gmlp_layer(x, mods, gmlp_w_in[j], gmlp_b_in[j], gmlp_v_g[j], gmlp_v_b[j], gmlp_w_s[j],
                            gmlp_b_s[j], gmlp_w_out[j], lng, lnb)
        elif kind == 1:
            swap = _rope_swap_perm()
            wd = mla_w_dqkv[j]
            r0 = MLA_Q_RANK + MLA_KV_RANK
            wd = jnp.concatenate([wd, wd[:, r0:][:, swap]], axis=1).astype(BF16)
            wuq = mla_w_uq[j].reshape(MLA_Q_RANK, MLA_HEADS, MLA_D_QK)
            wuq_rope = wuq[:, :, MLA_D_NOPE:]
            wuq = jnp.concatenate([wuq[:, :, :MLA_D_NOPE].reshape(MLA_Q_RANK, -1),
                                   wuq_rope.reshape(MLA_Q_RANK, -1),
                                   wuq_rope[:, :, swap].reshape(MLA_Q_RANK, -1)], axis=1).astype(BF16)
            wukv = mla_w_ukv[j].reshape(MLA_KV_RANK, MLA_HEADS, MLA_D_NOPE + MLA_D_V)
            wukv = jnp.concatenate([wukv[:, :, :MLA_D_NOPE].reshape(MLA_KV_RANK, -1),
                                    wukv[:, :, MLA_D_NOPE:].reshape(MLA_KV_RANK, -1)], axis=1).astype(BF16)
            weights = (wd, mla_q_norm[j].reshape(1, -1), mla_kv_norm[j].reshape(1, -1), wuq, wukv)
            cos, sin = _rope_tables(lat_len)
            tables = (jnp.tile(cos, (1, MLA_HEADS)), jnp.tile(sin, (1, MLA_HEADS)), cos, sin)

            qc, kc, vc, ckv, krope = _mla_project(x, 0, n_ctx_seq, ctx_len, mods, weights, None)
            new_ckv.append(ckv.reshape(n_ctx_seq, ctx_len, MLA_KV_RANK))
            new_krope.append(krope.reshape(n_ctx_seq, ctx_len, MLA_D_ROPE))
            ql, kl, vl, _, _ = _mla_project(x, n_ctx, n_lat_seq, lat_len, mods, weights, tables)
            k_cache, v_cache = _mla_cache_keys(cache_mla_ckv[:, j], cache_mla_krope[:, j], wukv)
            a_ctx = _attention(qc, [kc], [vc]).reshape(n_ctx, -1)
            a_lat = _attention(ql, [k_cache, kl], [v_cache, vl]).reshape(n_lat_seq * lat_len, -1)
            x = _out_proj_layer(x, mods, a_ctx, a_lat, mla_w_o[j].astype(BF16), lng, lnb)
        else:
            proj = _gla_project(x, mods, gla_w_in[j], gla_wa1[j], gla_wa2[j], gla_ba[j])
            zero_state = jnp.zeros((n_ctx_seq, 2, GLA_HEADS, GLA_DV, GLA_DK), F32)
            lat_state = jnp.swapaxes(state_gla[:, j], -1, -2)
            ofc, obc, s_ctx = _gla_scan(proj, 0, n_ctx_seq, ctx_len, zero_state)
            ofl, obl, _ = _gla_scan(proj, n_ctx, n_lat_seq, lat_len, lat_state)
            new_gla.append(jnp.swapaxes(s_ctx, -1, -2))
            x = _gla_out_layer(x, mods, (ofc, ofl), (obc, obl), proj, gla_norm[j], gla_w_o[j], lng, lnb)

        lng, lnb = ln_g[i, 1], ln_b[i, 1]
        f = i // 2
        if i % 2 == 0:
            x = _ffn_layer(x, mods, ffn_w13[f].astype(BF16), ffn_w2[f].astype(BF16), lng, lnb)
        else:
            split = n_ctx if i == DEPTH - 1 else None
            x = _moe_layer(x, mods, moe_router[f], moe_w13[f], moe_w2[f], lng, lnb, split)

    y_prompt = x[0].reshape(n_ctx_seq, ctx_len, D_MODEL)
    y_sample = x[1].reshape(n_lat_seq, lat_len, D_MODEL)
    return (y_prompt, y_sample, jnp.stack(new_ckv, axis=1), jnp.stack(new_krope, axis=1),
            jnp.stack(new_gla, axis=1))
```

```python
import functools

import jax
import jax.numpy as jnp
from jax import lax
from jax.experimental import pallas as pl
from jax.experimental.pallas import tpu as pltpu

F32 = jnp.float32
BF16 = jnp.bfloat16
HIGHEST = lax.Precision.HIGHEST

D_MODEL = 1024
DEPTH = 4
GRID_W = 64
DEEPNORM_ALPHA = (2.0 * DEPTH) ** 0.25
LN_EPS = 1e-5
RMS_EPS = 1e-6
GROUP_TOKENS = 4096

CHUNK_A = 128
GMLP_WIDTH = 2 * D_MODEL
GMLP_GROUPS = 8
GMLP_GROUP_DIM = GMLP_WIDTH // GMLP_GROUPS

MLA_HEADS = 8
MLA_Q_RANK = D_MODEL // 2
MLA_KV_RANK = D_MODEL // 4
MLA_D_NOPE = 128
MLA_D_ROPE = 64
MLA_D_V = 128
MLA_D_QK = MLA_D_NOPE + MLA_D_ROPE
MLA_D_VP = 2 * MLA_D_V
LOG2_E = 1.4426950408889634
ROPE_BASE = 10000.0

GLA_HEADS = 4
GLA_DK = 128
GLA_DV = 256
GLA_DECAY_RANK = 16
GLA_GATE_NORM = 16.0
GLA_CHUNK = 64
GLA_DQK = GLA_HEADS * GLA_DK
GLA_DVT = GLA_HEADS * GLA_DV
GLA_PROJ_W = 2 * GLA_DQK + 2 * GLA_DVT + 2 * GLA_DQK

D_FF = 2816
N_EXPERTS = 8
TOP_K = 2
D_FF_EXPERT = 3584

VMEM_LIMIT_BYTES = 56 * 1024 * 1024
ADA_ROWS = 16
ADA_TN = 1536
TM_GMLP = 512
TM_FFN = 512
TM_PROJ = 512
TQ_ATTN = 2048
TQ_SUB_ATTN = 256
TM_OUT = 512
TB_GLA = 512
TM_ROUTE = 512
TM_EXPERT = 512
TF_EXPERT = 1792
TM_COMBINE = 512
DMA_ISSUE_UNROLL = 8


def _cparams(*sem):
    return pltpu.CompilerParams(dimension_semantics=sem, vmem_limit_bytes=VMEM_LIMIT_BYTES)


def _layer_norm(y, g, b):
    mu = jnp.mean(y, axis=-1, keepdims=True)
    d = y - mu
    var = jnp.mean(d * d, axis=-1, keepdims=True)
    return d * lax.rsqrt(var + LN_EPS) * g + b


def _rms_norm(y, g):
    return y * lax.rsqrt(jnp.mean(y * y, axis=-1, keepdims=True) + RMS_EPS) * g


def _modulate(x, mod_ref, shift_row):
    return x * (1.0 + mod_ref[shift_row + 1:shift_row + 2, :]) + mod_ref[shift_row:shift_row + 1, :]


def _post_norm(x, out, mod_ref, gate_row, lng_ref, lnb_ref):
    y = DEEPNORM_ALPHA * x + mod_ref[gate_row:gate_row + 1, :] * out
    return _layer_norm(y, lng_ref[...], lnb_ref[...])


def _dot(a, b):
    return jnp.dot(a, b, preferred_element_type=F32)


def _dot_nt(a, b):
    return lax.dot_general(a, b, (((1,), (1,)), ((), ())), preferred_element_type=F32)


def _dot_tn(a, b):
    return lax.dot_general(a, b, (((0,), (0,)), ((), ())), preferred_element_type=F32)


def _mod_spec(tm):
    return pl.BlockSpec((None, 6, D_MODEL), lambda i, *_: ((i * tm) // GROUP_TOKENS, 0, 0))


def _full_spec(shape):
    nd = len(shape)
    return pl.BlockSpec(shape, lambda *_: (0,) * nd, pipeline_mode=pl.Buffered(1))


def _ada_kernel(c_ref, w_ref, b_ref, o_ref):
    c = c_ref[...]
    s = c * jax.nn.sigmoid(c)
    o_ref[...] = jnp.dot(s, w_ref[...], precision=HIGHEST, preferred_element_type=F32) + b_ref[...]


def _adaln_all(cond, ada_w, ada_b):
    n_out = 6 * D_MODEL
    return pl.pallas_call(
        _ada_kernel,
        grid=(DEPTH, n_out // ADA_TN),
        in_specs=[
            pl.BlockSpec((ADA_ROWS, D_MODEL), lambda l, j: (0, 0)),
            pl.BlockSpec((None, D_MODEL, ADA_TN), lambda l, j: (l, 0, j)),
            pl.BlockSpec((None, 1, ADA_TN), lambda l, j: (l, 0, j)),
        ],
        out_specs=pl.BlockSpec((None, ADA_ROWS, ADA_TN), lambda l, j: (l, 0, j)),
        out_shape=jax.ShapeDtypeStruct((DEPTH, ADA_ROWS, n_out), F32),
        compiler_params=_cparams("parallel", "parallel"),
        name="adaln",
    )(cond, ada_w, ada_b.reshape(DEPTH, 1, n_out))


def _gmlp_kernel(*refs, n_ctx_tiles):
    if n_ctx_tiles is None:
        x = refs[0][...]
        refs = refs[1:]
    else:
        x = jnp.where(pl.program_id(0) < n_ctx_tiles, refs[0][...], refs[1][...])
        refs = refs[2:]
    (mod_ref, win_ref, bin_ref, vg_ref, vb_ref, ws_ref, bs_ref, wout_ref, lng_ref, lnb_ref,
     o_ref, u_ref, vn_ref, p_ref) = refs
    hb = _modulate(x, mod_ref, 0).astype(BF16)
    v = jax.nn.gelu(_dot(hb, win_ref[:, GMLP_WIDTH:]) + bin_ref[:, GMLP_WIDTH:])
    vn_ref[...] = _layer_norm(v, vg_ref[...], vb_ref[...]).astype(BF16)
    u_ref[...] = jax.nn.gelu(_dot(hb, win_ref[:, :GMLP_WIDTH]) + bin_ref[:, :GMLP_WIDTH])
    for c in range(x.shape[0] // CHUNK_A):
        rows = slice(c * CHUNK_A, (c + 1) * CHUNK_A)
        for g in range(GMLP_GROUPS):
            cols = slice(g * GMLP_GROUP_DIM, (g + 1) * GMLP_GROUP_DIM)
            mixed = _dot(ws_ref[g], vn_ref[rows, cols]) + bs_ref[g]
            p_ref[rows, cols] = (u_ref[rows, cols] * mixed).astype(BF16)
    out = _dot(p_ref[...], wout_ref[...])
    o_ref[...] = _post_norm(x, out, mod_ref, 2, lng_ref, lnb_ref)


def _gmlp_layer(x, mods, w_in, b_in, v_g, v_b, w_s, b_s, w_out, ln_g, ln_b):
    tm = TM_GMLP
    row = pl.BlockSpec((tm, D_MODEL), lambda i: (i, 0))
    if isinstance(x, tuple):
        n_ctx_tiles = x[0].shape[0] // tm
        n = x[0].shape[0] + x[1].shape[0]
        x_specs = _ctx_lat_specs(tm, D_MODEL, n_ctx_tiles)
    else:
        n_ctx_tiles = None
        n = x.shape[0]
        x, x_specs = (x,), [row]
    return pl.pallas_call(
        functools.partial(_gmlp_kernel, n_ctx_tiles=n_ctx_tiles),
        grid=(n // tm,),
        in_specs=x_specs + [
            _mod_spec(tm),
            _full_spec((D_MODEL, 2 * GMLP_WIDTH)), _full_spec((1, 2 * GMLP_WIDTH)),
            _full_spec((1, GMLP_WIDTH)), _full_spec((1, GMLP_WIDTH)),
            _full_spec((GMLP_GROUPS, CHUNK_A, CHUNK_A)), _full_spec((GMLP_GROUPS, CHUNK_A, 1)),
            _full_spec((GMLP_WIDTH, D_MODEL)),
            _full_spec((1, D_MODEL)), _full_spec((1, D_MODEL)),
        ],
        out_specs=row,
        out_shape=jax.ShapeDtypeStruct((n, D_MODEL), F32),
        scratch_shapes=[pltpu.VMEM((tm, GMLP_WIDTH), F32), pltpu.VMEM((tm, GMLP_WIDTH), BF16),
                        pltpu.VMEM((tm, GMLP_WIDTH), BF16)],
        compiler_params=_cparams("parallel"),
        name="gmlp",
    )(*x, mods, w_in.astype(BF16), b_in.reshape(1, -1), v_g.reshape(1, -1), v_b.reshape(1, -1),
      w_s.astype(BF16), b_s.reshape(GMLP_GROUPS, CHUNK_A, 1), w_out.astype(BF16),
      ln_g.reshape(1, -1), ln_b.reshape(1, -1))


def _ffn_kernel(x_ref, mod_ref, w13_ref, w2_ref, lng_ref, lnb_ref, o_ref):
    x = x_ref[...]
    hb = _modulate(x, mod_ref, 3).astype(BF16)
    a = _dot(hb, w13_ref[:, :D_FF])
    b = _dot(hb, w13_ref[:, D_FF:])
    out = _dot((a * jax.nn.sigmoid(a) * b).astype(BF16), w2_ref[...])
    o_ref[...] = _post_norm(x, out, mod_ref, 5, lng_ref, lnb_ref)


def _ffn_layer(x, mods, w13, w2, ln_g, ln_b):
    n = x.shape[0]
    tm = TM_FFN
    row = pl.BlockSpec((tm, D_MODEL), lambda i: (i, 0))
    return pl.pallas_call(
        _ffn_kernel,
        grid=(n // tm,),
        in_specs=[row, _mod_spec(tm), _full_spec(w13.shape), _full_spec(w2.shape),
                  _full_spec((1, D_MODEL)), _full_spec((1, D_MODEL))],
        out_specs=row,
        out_shape=jax.ShapeDtypeStruct((n, D_MODEL), F32),
        compiler_params=_cparams("parallel"),
        name="ffn",
    )(x, mods, w13, w2, ln_g.reshape(1, -1), ln_b.reshape(1, -1))


def _rope_swap_perm():
    j = jnp.arange(MLA_D_ROPE)
    half = MLA_D_ROPE // 4
    return jnp.where((j % (2 * half)) < half, j + half, j - half)


def _rope_tables(n_tokens):
    n_freq = MLA_D_ROPE // 4
    t = jnp.arange(n_tokens)
    pos_r = (t // GRID_W).astype(F32)[:, None]
    pos_c = (t % GRID_W).astype(F32)[:, None]
    inv = ROPE_BASE ** (-jnp.arange(n_freq, dtype=F32) / n_freq)
    ang_r, ang_c = pos_r * inv, pos_c * inv
    cos = jnp.concatenate([jnp.cos(ang_r), jnp.cos(ang_r), jnp.cos(ang_c), jnp.cos(ang_c)], axis=-1)
    sin = jnp.concatenate([-jnp.sin(ang_r), jnp.sin(ang_r), -jnp.sin(ang_c), jnp.sin(ang_c)], axis=-1)
    return cos, sin


def _write_heads(q_ref, k_ref, v_ref, q_nope, q_rope, kv, k_rope):
    scale = MLA_D_QK ** -0.5 * LOG2_E
    k_rope = k_rope.astype(BF16)
    ones = jnp.ones((kv.shape[0], MLA_D_VP - MLA_D_V), BF16)
    for h in range(MLA_HEADS):
        nope = slice(h * MLA_D_NOPE, (h + 1) * MLA_D_NOPE)
        if q_ref is not None:
            q_ref[h, :, :MLA_D_NOPE] = (q_nope[:, nope] * scale).astype(BF16)
            q_ref[h, :, MLA_D_NOPE:] = (q_rope[:, h * MLA_D_ROPE:(h + 1) * MLA_D_ROPE] * scale).astype(BF16)
        k_ref[h, :, :MLA_D_NOPE] = kv[:, nope].astype(BF16)
        k_ref[h, :, MLA_D_NOPE:] = k_rope
        vcol = MLA_HEADS * MLA_D_NOPE + h * MLA_D_V
        v_ref[h, :, :MLA_D_V] = kv[:, vcol:vcol + MLA_D_V].astype(BF16)
        v_ref[h, :, MLA_D_V:] = ones


def _mla_proj_kernel(*refs, rope):
    if rope:
        (x_ref, mod_ref, wd_ref, qg_ref, kvg_ref, wuq_ref, wukv_ref, cq_ref, sq_ref, ck_ref, sk_ref,
         q_ref, k_ref, v_ref, ckv_ref, kr_ref) = refs
    else:
        (x_ref, mod_ref, wd_ref, qg_ref, kvg_ref, wuq_ref, wukv_ref,
         q_ref, k_ref, v_ref, ckv_ref, kr_ref) = refs
    hb = _modulate(x_ref[...], mod_ref, 0).astype(BF16)
    c = _dot(hb, wd_ref[...])
    r0 = MLA_Q_RANK + MLA_KV_RANK
    c_q = _rms_norm(c[:, :MLA_Q_RANK], qg_ref[...]).astype(BF16)
    c_kv = _rms_norm(c[:, MLA_Q_RANK:r0], kvg_ref[...])
    k_rope = c[:, r0:r0 + MLA_D_ROPE]
    ckv_ref[...] = c_kv
    kr_ref[...] = k_rope
    q = _dot(c_q, wuq_ref[...])
    n_nope = MLA_HEADS * MLA_D_NOPE
    n_rope = MLA_HEADS * MLA_D_ROPE
    q_rope = q[:, n_nope:n_nope + n_rope]
    if rope:
        q_rope = q_rope * cq_ref[...] + q[:, n_nope + n_rope:] * sq_ref[...]
        k_rope = k_rope * ck_ref[...] + c[:, r0 + MLA_D_ROPE:] * sk_ref[...]
    kv = _dot(c_kv.astype(BF16), wukv_ref[...])
    _write_heads(q_ref, k_ref, v_ref, q[:, :n_nope], q_rope, kv, k_rope)


def _mla_project(x, row_offset, n_seq, seq_len, mods, weights, tables):
    wd, qg, kvg, wuq, wukv = weights
    tm = min(TM_PROJ, seq_len)
    per_seq = seq_len // tm
    n = n_seq * seq_len
    off = row_offset // tm
    rope = tables is not None
    in_specs = [
        pl.BlockSpec((tm, D_MODEL), lambda i: (off + i, 0)),
        pl.BlockSpec((None, 6, D_MODEL), lambda i: ((row_offset + i * tm) // GROUP_TOKENS, 0, 0)),
        _full_spec(wd.shape), _full_spec(qg.shape), _full_spec(kvg.shape),
        _full_spec(wuq.shape), _full_spec(wukv.shape),
    ]
    args = [x, mods, wd, qg, kvg, wuq, wukv]
    if rope:
        pos = lambda i: (i % per_seq, 0)
        in_specs += [pl.BlockSpec((tm, MLA_HEADS * MLA_D_ROPE), pos)] * 2 + [pl.BlockSpec((tm, MLA_D_ROPE), pos)] * 2
        args += list(tables)
    head_spec = lambda d: pl.BlockSpec((None, MLA_HEADS, tm, d), lambda i: (i // per_seq, 0, i % per_seq, 0))
    head_shape = lambda d: jax.ShapeDtypeStruct((n_seq, MLA_HEADS, seq_len, d), BF16)
    return pl.pallas_call(
        functools.partial(_mla_proj_kernel, rope=rope),
        grid=(n // tm,),
        in_specs=in_specs,
        out_specs=[head_spec(MLA_D_QK), head_spec(MLA_D_QK), head_spec(MLA_D_VP),
                   pl.BlockSpec((tm, MLA_KV_RANK), lambda i: (i, 0)),
                   pl.BlockSpec((tm, MLA_D_ROPE), lambda i: (i, 0))],
        out_shape=[head_shape(MLA_D_QK), head_shape(MLA_D_QK), head_shape(MLA_D_VP),
                   jax.ShapeDtypeStruct((n, MLA_KV_RANK), F32),
                   jax.ShapeDtypeStruct((n, MLA_D_ROPE), F32)],
        compiler_params=_cparams("parallel"),
        name="mla_proj_rope" if rope else "mla_proj",
    )(*args)


def _mla_cache_kernel(ckv_ref, kr_ref, wukv_ref, k_ref, v_ref):
    kv = _dot(ckv_ref[...].astype(BF16), wukv_ref[...])
    _write_heads(None, k_ref, v_ref, None, None, kv, kr_ref[...])


def _mla_cache_keys(ckv, krope, wukv):
    b, l, _ = ckv.shape
    head_spec = lambda d: pl.BlockSpec((None, MLA_HEADS, l, d), lambda i: (i, 0, 0, 0))
    return pl.pallas_call(
        _mla_cache_kernel,
        grid=(b,),
        in_specs=[pl.BlockSpec((None, l, MLA_KV_RANK), lambda i: (i, 0, 0)),
                  pl.BlockSpec((None, l, MLA_D_ROPE), lambda i: (i, 0, 0)),
                  _full_spec(wukv.shape)],
        out_specs=[head_spec(MLA_D_QK), head_spec(MLA_D_VP)],
        out_shape=[jax.ShapeDtypeStruct((b, MLA_HEADS, l, MLA_D_QK), BF16),
                   jax.ShapeDtypeStruct((b, MLA_HEADS, l, MLA_D_VP), BF16)],
        compiler_params=_cparams("parallel"),
        name="mla_cache_keys",
    )(ckv, krope, wukv)


def _attn_kernel(*refs, n_seg):
    q_ref = refs[0]
    k_refs = refs[1:1 + n_seg]
    v_refs = refs[1 + n_seg:1 + 2 * n_seg]
    o_ref = refs[1 + 2 * n_seg]
    s_ref, p_ref = refs[2 + 2 * n_seg:]
    n_buf, sub, _ = s_ref.shape
    n_sub = q_ref.shape[0] // sub
    seg_cols = []
    col = 0
    for k_ref in k_refs:
        seg_cols.append(slice(col, col + k_ref.shape[0]))
        col += k_ref.shape[0]

    def scores(i):
        q = q_ref[i * sub:(i + 1) * sub, :]
        for k_ref, cols in zip(k_refs, seg_cols):
            s_ref[i % n_buf, :, cols] = _dot_nt(q, k_ref[...])

    def finish(i):
        s = s_ref[i % n_buf]
        m = jnp.max(s, axis=-1, keepdims=True)
        p_ref[i % n_buf] = jnp.exp2((s - m).astype(BF16))
        acc = sum(_dot(p_ref[i % n_buf, :, cols], v_ref[...]) for v_ref, cols in zip(v_refs, seg_cols))
        o_ref[i * sub:(i + 1) * sub, :] = (acc[:, :MLA_D_V] / acc[:, MLA_D_V:MLA_D_V + 1]).astype(BF16)

    scores(0)
    for i in range(n_sub):
        if i + 1 < n_sub:
            scores(i + 1)
        finish(i)


def _attention(q, ks, vs):
    b, h, t, _ = q.shape
    tq = min(TQ_ATTN, t)
    sub = min(TQ_SUB_ATTN, tq)
    n_buf = min(2, tq // sub)
    n_keys = sum(a.shape[2] for a in ks)
    seg = lambda a: pl.BlockSpec((None, None) + a.shape[2:], lambda bi, hi, qi: (bi, hi, 0, 0))
    return pl.pallas_call(
        functools.partial(_attn_kernel, n_seg=len(ks)),
        grid=(b, h, t // tq),
        in_specs=[pl.BlockSpec((None, None, tq, MLA_D_QK), lambda bi, hi, qi: (bi, hi, qi, 0))]
        + [seg(a) for a in ks] + [seg(a) for a in vs],
        out_specs=pl.BlockSpec((None, tq, MLA_D_V), lambda bi, hi, qi: (bi, qi, hi)),
        out_shape=jax.ShapeDtypeStruct((b, t, h * MLA_D_V), BF16),
        scratch_shapes=[pltpu.VMEM((n_buf, sub, n_keys), F32), pltpu.VMEM((n_buf, sub, n_keys), BF16)],
        compiler_params=_cparams("parallel", "parallel", "arbitrary"),
        name="mla_attention",
    )(q, *ks, *vs)


def _ctx_lat_specs(tm, width, n_ctx_tiles, col=0):
    return [pl.BlockSpec((tm, width), lambda i: (jnp.minimum(i, n_ctx_tiles - 1), col)),
            pl.BlockSpec((tm, width), lambda i: (jnp.maximum(i - n_ctx_tiles, 0), col))]


def _out_proj_kernel(x_ref, mod_ref, ac_ref, al_ref, w_ref, lng_ref, lnb_ref, o_ref, *, n_ctx_tiles):
    def finish(a_ref):
        out = _dot(a_ref[...], w_ref[...])
        o_ref[...] = _post_norm(x_ref[...], out, mod_ref, 2, lng_ref, lnb_ref)

    is_ctx = pl.program_id(0) < n_ctx_tiles
    pl.when(is_ctx)(lambda: finish(ac_ref))
    pl.when(jnp.logical_not(is_ctx))(lambda: finish(al_ref))


def _out_proj_layer(x, mods, a_ctx, a_lat, w_o, ln_g, ln_b):
    n = x.shape[0]
    tm = TM_OUT
    n_ctx_tiles = a_ctx.shape[0] // tm
    row = pl.BlockSpec((tm, D_MODEL), lambda i: (i, 0))
    return pl.pallas_call(
        functools.partial(_out_proj_kernel, n_ctx_tiles=n_ctx_tiles),
        grid=(n // tm,),
        in_specs=[row, _mod_spec(tm)] + _ctx_lat_specs(tm, a_ctx.shape[1], n_ctx_tiles)
        + [_full_spec(w_o.shape), _full_spec((1, D_MODEL)), _full_spec((1, D_MODEL))],
        out_specs=row,
        out_shape=jax.ShapeDtypeStruct((n, D_MODEL), F32),
        compiler_params=_cparams("parallel"),
        name="mla_out_proj",
    )(x, mods, a_ctx, a_lat, w_o, ln_g.reshape(1, -1), ln_b.reshape(1, -1))


def _gla_proj_kernel(x_ref, mod_ref, win_ref, wa2_ref, ba_ref, o_ref):
    h = _modulate(x_ref[...], mod_ref, 0).astype(BF16)
    n_qkvr = 2 * GLA_DQK + 2 * GLA_DVT
    proj = _dot(h, win_ref[...])
    o_ref[:, :GLA_DQK] = proj[:, :GLA_DQK] * (GLA_DK ** -0.5)
    o_ref[:, GLA_DQK:n_qkvr] = proj[:, GLA_DQK:n_qkvr]
    logit = _dot(proj[:, n_qkvr:].astype(BF16), wa2_ref[...]) + ba_ref[...]
    log_sig = jnp.minimum(logit, 0.0) - jnp.log(1.0 + jnp.exp(-jnp.abs(logit)))
    o_ref[:, n_qkvr:] = log_sig / GLA_GATE_NORM


def _gla_project(x, mods, w_in, wa1, wa2, ba):
    n = x.shape[0]
    tm = TM_PROJ
    lane = 128
    low_w = 2 * GLA_DECAY_RANK
    w_ext = jnp.concatenate([w_in, wa1[0], wa1[1], jnp.zeros((D_MODEL, lane - low_w), F32)], axis=1)
    zeros = jnp.zeros((GLA_DECAY_RANK, GLA_DQK), F32)
    wa2_blk = jnp.concatenate([jnp.concatenate([wa2[0], zeros], axis=1),
                               jnp.concatenate([zeros, wa2[1]], axis=1),
                               jnp.zeros((lane - low_w, 2 * GLA_DQK), F32)], axis=0)
    return pl.pallas_call(
        _gla_proj_kernel,
        grid=(n // tm,),
        in_specs=[pl.BlockSpec((tm, D_MODEL), lambda i: (i, 0)), _mod_spec(tm),
                  _full_spec(w_ext.shape), _full_spec(wa2_blk.shape), _full_spec((1, 2 * GLA_DQK))],
        out_specs=pl.BlockSpec((tm, GLA_PROJ_W), lambda i: (i, 0)),
        out_shape=jax.ShapeDtypeStruct((n, GLA_PROJ_W), F32),
        compiler_params=_cparams("parallel"),
        name="gla_proj",
    )(x, mods, w_ext.astype(BF16), wa2_blk.astype(BF16), ba.reshape(1, -1))


def _chunk_cumsum(g, reverse):
    n = g.shape[0]
    pos = lax.broadcasted_iota(jnp.int32, g.shape, 0) % GLA_CHUNK
    b = g
    shift = 1
    while shift < GLA_CHUNK:
        if reverse:
            moved = pltpu.roll(b, n - shift, 0)
            keep = pos < GLA_CHUNK - shift
        else:
            moved = pltpu.roll(b, shift, 0)
            keep = pos >= shift
        b = b + jnp.where(keep, moved, 0.0)
        shift *= 2
    return b


def _gla_scan_kernel(qf_ref, kf_ref, vf_ref, gf_ref, qb_ref, kb_ref, vb_ref, gb_ref, s0_ref,
                     of_ref, ob_ref, sfin_ref, st_ref):
    i = pl.program_id(2)

    @pl.when(i == 0)
    def _():
        st_ref[...] = s0_ref[...]

    n_chunks = qf_ref.shape[0] // GLA_CHUNK
    ii = lax.broadcasted_iota(jnp.int32, (GLA_CHUNK, GLA_CHUNK), 0)
    jj = lax.broadcasted_iota(jnp.int32, (GLA_CHUNK, GLA_CHUNK), 1)
    dirs = ((0, False, qf_ref, kf_ref, vf_ref, of_ref, _chunk_cumsum(gf_ref[...], False), jj <= ii),
            (1, True, qb_ref, kb_ref, vb_ref, ob_ref, _chunk_cumsum(gb_ref[...], True), jj >= ii))
    for step in range(n_chunks):
        for d, reverse, q_ref, k_ref, v_ref, o_ref, b_all, mask in dirs:
            c = n_chunks - 1 - step if reverse else step
            rows = slice(c * GLA_CHUNK, (c + 1) * GLA_CHUNK)
            b = b_all[rows]
            b_tot = b[:1] if reverse else b[GLA_CHUNK - 1:]
            q = q_ref[rows, :]
            k = k_ref[rows, :]
            v = v_ref[rows, :].astype(BF16)
            qe = (q * jnp.exp(b)).astype(BF16)
            ke = (k * jnp.exp(-b)).astype(BF16)
            kd = (k * jnp.exp(b_tot - b)).astype(BF16)
            a = jnp.where(mask, _dot_nt(qe, ke), 0.0).astype(BF16)
            st = st_ref[d]
            o_ref[rows, :] = _dot(a, v) + _dot_nt(qe, st.astype(BF16))
            st_ref[d] = st * jnp.exp(b_tot) + _dot_tn(v, kd)

    @pl.when(i == pl.num_programs(2) - 1)
    def _():
        sfin_ref[...] = st_ref[...]


def _gla_scan(proj, row_offset, n_seq, seq_len, s0_t):
    tb = min(TB_GLA, seq_len)
    nb = seq_len // tb
    off = row_offset // tb
    v_col0 = 2 * GLA_DQK // GLA_DV
    g_col0 = (2 * GLA_DQK + 2 * GLA_DVT) // GLA_DK

    def in_specs(reverse):
        blk = (lambda s, i: off + s * nb + nb - 1 - i) if reverse else (lambda s, i: off + s * nb + i)
        g_col = g_col0 + (GLA_HEADS if reverse else 0)
        return [pl.BlockSpec((tb, GLA_DK), lambda s, h, i: (blk(s, i), h)),
                pl.BlockSpec((tb, GLA_DK), lambda s, h, i: (blk(s, i), GLA_HEADS + h)),
                pl.BlockSpec((tb, GLA_DV), lambda s, h, i: (blk(s, i), v_col0 + h)),
                pl.BlockSpec((tb, GLA_DK), lambda s, h, i: (blk(s, i), g_col + h))]

    state_spec = pl.BlockSpec((None, 2, None, GLA_DV, GLA_DK), lambda s, h, i: (s, 0, h, 0, 0))
    o_shape = jax.ShapeDtypeStruct((n_seq * seq_len, GLA_DVT), F32)
    return pl.pallas_call(
        _gla_scan_kernel,
        grid=(n_seq, GLA_HEADS, nb),
        in_specs=in_specs(False) + in_specs(True) + [state_spec],
        out_specs=[pl.BlockSpec((tb, GLA_DV), lambda s, h, i: (s * nb + i, h)),
                   pl.BlockSpec((tb, GLA_DV), lambda s, h, i: (s * nb + nb - 1 - i, h)),
                   state_spec],
        out_shape=[o_shape, o_shape, jax.ShapeDtypeStruct((n_seq, 2, GLA_HEADS, GLA_DV, GLA_DK), F32)],
        scratch_shapes=[pltpu.VMEM((2, GLA_DV, GLA_DK), F32)],
        compiler_params=_cparams("parallel", "parallel", "arbitrary"),
        name="gla_scan",
    )(*([proj] * 8), s0_t)


def _gla_out_kernel(x_ref, mod_ref, ofc_ref, ofl_ref, obc_ref, obl_ref, r_ref, ng_ref, wo_ref, lng_ref, lnb_ref,
                    o_ref, p_ref, *, n_ctx_tiles):
    def finish(of_ref, ob_ref):
        for h in range(GLA_HEADS):
            cols = slice(h * GLA_DV, (h + 1) * GLA_DV)
            o = _rms_norm(of_ref[:, cols] + ob_ref[:, cols], ng_ref[...])
            r = r_ref[:, cols]
            p_ref[:, cols] = (o * (r * jax.nn.sigmoid(r))).astype(BF16)
        out = _dot(p_ref[...], wo_ref[...])
        o_ref[...] = _post_norm(x_ref[...], out, mod_ref, 2, lng_ref, lnb_ref)

    is_ctx = pl.program_id(0) < n_ctx_tiles
    pl.when(is_ctx)(lambda: finish(ofc_ref, obc_ref))
    pl.when(jnp.logical_not(is_ctx))(lambda: finish(ofl_ref, obl_ref))


def _gla_out_layer(x, mods, o_fwd, o_bwd, proj, norm_g, w_o, ln_g, ln_b):
    n = x.shape[0]
    tm = TM_OUT
    n_ctx_tiles = o_fwd[0].shape[0] // tm
    row = pl.BlockSpec((tm, D_MODEL), lambda i: (i, 0))
    r_block = (2 * GLA_DQK + GLA_DVT) // GLA_DVT
    return pl.pallas_call(
        functools.partial(_gla_out_kernel, n_ctx_tiles=n_ctx_tiles),
        grid=(n // tm,),
        in_specs=[row, _mod_spec(tm)] + _ctx_lat_specs(tm, GLA_DVT, n_ctx_tiles) * 2
        + [pl.BlockSpec((tm, GLA_DVT), lambda i: (i, r_block)),
           _full_spec((1, GLA_DV)), _full_spec(w_o.shape),
           _full_spec((1, D_MODEL)), _full_spec((1, D_MODEL))],
        out_specs=row,
        out_shape=jax.ShapeDtypeStruct((n, D_MODEL), F32),
        scratch_shapes=[pltpu.VMEM((tm, GLA_DVT), BF16)],
        compiler_params=_cparams("parallel"),
        name="gla_out_proj",
    )(x, mods, *o_fwd, *o_bwd, proj, norm_g.reshape(1, -1), w_o.astype(BF16),
      ln_g.reshape(1, -1), ln_b.reshape(1, -1))


def _router_kernel(x_ref, mod_ref, wr_ref, idx_ref, wt_ref):
    h = _modulate(x_ref[...], mod_ref, 3)
    logits = lax.dot_general(wr_ref[...], h, (((1,), (1,)), ((), ())), precision=HIGHEST,
                             preferred_element_type=F32)
    e = lax.broadcasted_iota(jnp.int32, logits.shape, 0)
    m1 = jnp.max(logits, axis=0, keepdims=True)
    i1 = jnp.min(jnp.where(logits == m1, e, N_EXPERTS), axis=0, keepdims=True)
    rest = jnp.where(e == i1, -jnp.inf, logits)
    m2 = jnp.max(rest, axis=0, keepdims=True)
    i2 = jnp.min(jnp.where(rest == m2, e, N_EXPERTS), axis=0, keepdims=True)
    z = jnp.exp(m2 - m1)
    idx_ref[0:1, :] = i1
    idx_ref[1:2, :] = i2
    wt_ref[0:1, :] = 1.0 / (1.0 + z)
    wt_ref[1:2, :] = z / (1.0 + z)


def _route(x, mods, w_router):
    n = x.shape[0]
    tm = TM_ROUTE
    row = pl.BlockSpec((tm, D_MODEL), lambda i: (i, 0))
    pair = pl.BlockSpec((TOP_K, tm), lambda i: (0, i))
    return pl.pallas_call(
        _router_kernel,
        grid=(n // tm,),
        in_specs=[row, _mod_spec(tm), _full_spec((N_EXPERTS, D_MODEL))],
        out_specs=[pair, pair],
        out_shape=[jax.ShapeDtypeStruct((TOP_K, n), jnp.int32),
                   jax.ShapeDtypeStruct((TOP_K, n), F32)],
        compiler_params=_cparams("parallel"),
        name="moe_router",
    )(x, mods, w_router.T)


def _dispatch_plan(expert_idx, n_tiles):
    n = expert_idx.shape[1]
    flat = expert_idx.reshape(-1)
    onehot = (flat[:, None] == jnp.arange(N_EXPERTS)[None, :]).astype(jnp.int32)
    running = jnp.cumsum(onehot, axis=0)
    rank = jnp.sum(running * onehot, axis=1) - 1
    counts = running[-1]
    tiles = (counts + TM_EXPERT - 1) // TM_EXPERT
    tile_end = jnp.cumsum(tiles)
    tile_start = tile_end - tiles
    slot = jnp.sum(onehot * tile_start[None, :], axis=1) * TM_EXPERT + rank
    tile_id = jnp.arange(n_tiles)
    tile_active = (tile_id < tile_end[-1]).astype(jnp.int32)
    last_tile = jnp.minimum(tile_id, tile_end[-1] - 1)
    tile_expert = jnp.sum((tile_end[None, :] <= last_tile[:, None]).astype(jnp.int32), axis=1)
    return slot.reshape(TOP_K, n).astype(jnp.int32), tile_expert.astype(jnp.int32), tile_active


def _row_copies_wait(hbm_ref, buf_ref, sem):
    pltpu.make_async_copy(hbm_ref.at[pl.ds(0, buf_ref.shape[0]), :], buf_ref, sem).wait()


def _dispatch_kernel(s1_ref, s2_ref, x_ref, mod_ref, zero_hbm, xs_hbm, hbuf_ref, sems):
    del zero_hbm
    i, n_steps = pl.program_id(0), pl.num_programs(0)
    tm = x_ref.shape[0]
    b = i % 2

    def wait_rows(buf):
        for _ in range(TOP_K):
            _row_copies_wait(xs_hbm, hbuf_ref.at[buf], sems.at[buf])

    @pl.when(i >= 2)
    def _():
        wait_rows(b)

    hbuf_ref[b] = _modulate(x_ref[...], mod_ref, 3)
    base = i * tm

    def issue(r, carry):
        for s_ref in (s1_ref, s2_ref):
            pltpu.make_async_copy(hbuf_ref.at[b, pl.ds(r, 1), :],
                                  xs_hbm.at[pl.ds(s_ref[base + r], 1), :], sems.at[b]).start()
        return carry

    lax.fori_loop(0, tm, issue, 0, unroll=DMA_ISSUE_UNROLL)

    @pl.when(i == n_steps - 1)
    def _():
        wait_rows(b)

    @pl.when(jnp.logical_and(i == n_steps - 1, i >= 1))
    def _():
        wait_rows(1 - b)


def _dispatch(x, mods, slot_of, n_slots):
    n = x.shape[0]
    tm = TM_ROUTE
    return pl.pallas_call(
        _dispatch_kernel,
        grid_spec=pltpu.PrefetchScalarGridSpec(
            num_scalar_prefetch=2,
            grid=(n // tm,),
            in_specs=[pl.BlockSpec((tm, D_MODEL), lambda i, s1, s2: (i, 0)),
                      pl.BlockSpec((None, 6, D_MODEL), lambda i, s1, s2: ((i * tm) // GROUP_TOKENS, 0, 0)),
                      pl.BlockSpec(memory_space=pl.ANY)],
            out_specs=pl.BlockSpec(memory_space=pl.ANY),
            scratch_shapes=[pltpu.VMEM((2, tm, D_MODEL), F32), pltpu.SemaphoreType.DMA((2,))],
        ),
        out_shape=jax.ShapeDtypeStruct((n_slots, D_MODEL), F32),
        input_output_aliases={4: 0},
        compiler_params=_cparams("arbitrary"),
        name="moe_dispatch",
    )(slot_of[0], slot_of[1], x, mods, jnp.zeros((n_slots, D_MODEL), F32))


def _expert_kernel(te_ref, ta_ref, x_ref, w1_ref, w3_ref, w2_ref, o_ref, xb_ref, acc_ref):
    i, j = pl.program_id(0), pl.program_id(1)
    last = pl.num_programs(1) - 1
    active = ta_ref[i] == 1

    @pl.when(jnp.logical_and(j == 0, active))
    def _():
        xb_ref[...] = x_ref[...].astype(BF16)

    @pl.when(active)
    def _():
        x = xb_ref[...]
        a = _dot(x, w1_ref[...])
        b = _dot(x, w3_ref[...])
        part = _dot((a * jax.nn.sigmoid(a) * b).astype(BF16), w2_ref[...])

        @pl.when(j == 0)
        def _():
            acc_ref[...] = part

        @pl.when(j > 0)
        def _():
            acc_ref[...] += part

    @pl.when(jnp.logical_and(j == last, active))
    def _():
        o_ref[...] = acc_ref[...]

    @pl.when(jnp.logical_and(j == last, jnp.logical_not(active)))
    def _():
        o_ref[...] = jnp.zeros_like(o_ref)


def _expert_ffn(xs, tile_expert, tile_active, w13, w2, layer):
    tm, tf = TM_EXPERT, TF_EXPERT
    nf = D_FF_EXPERT // tf
    n_slots = xs.shape[0]
    row = pl.BlockSpec((tm, D_MODEL), lambda i, j, te, ta: (i, 0))
    return pl.pallas_call(
        _expert_kernel,
        grid_spec=pltpu.PrefetchScalarGridSpec(
            num_scalar_prefetch=2,
            grid=(n_slots // tm, nf),
            in_specs=[
                row,
                pl.BlockSpec((None, None, D_MODEL, tf), lambda i, j, te, ta: (layer, te[i], 0, j)),
                pl.BlockSpec((None, None, D_MODEL, tf), lambda i, j, te, ta: (layer, te[i], 0, j + nf)),
                pl.BlockSpec((None, None, tf, D_MODEL), lambda i, j, te, ta: (layer, te[i], j, 0)),
            ],
            out_specs=row,
            scratch_shapes=[pltpu.VMEM((tm, D_MODEL), BF16), pltpu.VMEM((tm, D_MODEL), F32)],
        ),
        out_shape=jax.ShapeDtypeStruct((n_slots, D_MODEL), F32),
        compiler_params=_cparams("parallel", "arbitrary"),
        name="moe_experts",
    )(tile_expert, tile_active, xs, w13, w13, w2)


def _combine_kernel(s1_ref, s2_ref, x_ref, mod_ref, wt_ref, y_hbm, lng_ref, lnb_ref, *refs, n_ctx_tiles):
    ybuf_ref, sems = refs[-2:]
    o_refs = refs[:-2]
    i, n_steps = pl.program_id(0), pl.num_programs(0)
    tm = x_ref.shape[0]

    def issue_tile(tile, buf):
        base = tile * tm

        def issue(r, carry):
            for k, s_ref in enumerate((s1_ref, s2_ref)):
                pltpu.make_async_copy(y_hbm.at[pl.ds(s_ref[base + r], 1), :],
                                      ybuf_ref.at[buf, k, pl.ds(r, 1), :], sems.at[buf]).start()
            return carry

        lax.fori_loop(0, tm, issue, 0, unroll=DMA_ISSUE_UNROLL)

    @pl.when(i == 0)
    def _():
        issue_tile(0, 0)

    @pl.when(i + 1 < n_steps)
    def _():
        issue_tile(i + 1, (i + 1) % 2)

    b = i % 2
    for k in range(TOP_K):
        _row_copies_wait(y_hbm, ybuf_ref.at[b, k], sems.at[b])
    y = wt_ref[:, 0:1] * ybuf_ref[b, 0] + wt_ref[:, 1:2] * ybuf_ref[b, 1]
    out = _post_norm(x_ref[...], y, mod_ref, 5, lng_ref, lnb_ref)
    if n_ctx_tiles is None:
        o_refs[0][...] = out
    else:
        is_ctx = i < n_ctx_tiles

        @pl.when(is_ctx)
        def _():
            o_refs[0][...] = out

        @pl.when(jnp.logical_not(is_ctx))
        def _():
            o_refs[1][...] = out


def _combine(x, mods, slot_of, weights_t, ys, ln_g, ln_b, n_ctx_split=None):
    n = x.shape[0]
    tm = TM_COMBINE
    row = pl.BlockSpec((tm, D_MODEL), lambda i, s1, s2: (i, 0))
    if n_ctx_split is None:
        n_ctx_tiles = None
        out_specs, out_shape = row, jax.ShapeDtypeStruct((n, D_MODEL), F32)
    else:
        n_ctx_tiles = n_ctx_split // tm
        out_specs = [pl.BlockSpec((tm, D_MODEL), lambda i, s1, s2: (jnp.minimum(i, n_ctx_tiles - 1), 0)),
                     pl.BlockSpec((tm, D_MODEL), lambda i, s1, s2: (jnp.maximum(i - n_ctx_tiles, 0), 0))]
        out_shape = [jax.ShapeDtypeStruct((n_ctx_split, D_MODEL), F32),
                     jax.ShapeDtypeStruct((n - n_ctx_split, D_MODEL), F32)]
    return pl.pallas_call(
        functools.partial(_combine_kernel, n_ctx_tiles=n_ctx_tiles),
        grid_spec=pltpu.PrefetchScalarGridSpec(
            num_scalar_prefetch=2,
            grid=(n // tm,),
            in_specs=[row,
                      pl.BlockSpec((None, 6, D_MODEL), lambda i, s1, s2: ((i * tm) // GROUP_TOKENS, 0, 0)),
                      pl.BlockSpec((tm, TOP_K), lambda i, s1, s2: (i, 0)),
                      pl.BlockSpec(memory_space=pl.ANY),
                      pl.BlockSpec((1, D_MODEL), lambda i, s1, s2: (0, 0)),
                      pl.BlockSpec((1, D_MODEL), lambda i, s1, s2: (0, 0))],
            out_specs=out_specs,
            scratch_shapes=[pltpu.VMEM((2, TOP_K, tm, D_MODEL), F32), pltpu.SemaphoreType.DMA((2,))],
        ),
        out_shape=out_shape,
        compiler_params=_cparams("arbitrary"),
        name="moe_combine",
    )(slot_of[0], slot_of[1], x, mods, weights_t, ys, ln_g.reshape(1, -1), ln_b.reshape(1, -1))


def _moe_layer(x, mods, w_router, w13, w2, layer, ln_g, ln_b, n_ctx_split=None):
    n = x.shape[0]
    n_tiles = (TOP_K * n) // TM_EXPERT + N_EXPERTS
    expert_idx, weights = _route(x, mods, w_router)
    slot_of, tile_expert, tile_active = _dispatch_plan(expert_idx, n_tiles)
    xs = _dispatch(x, mods, slot_of, n_tiles * TM_EXPERT)
    ys = _expert_ffn(xs, tile_expert, tile_active, w13, w2, layer)
    return _combine(x, mods, slot_of, weights.T, ys, ln_g, ln_b, n_ctx_split)


def kernel(x_prompt, x_sample, cache_mla_ckv, cache_mla_krope, state_gla, c, c_ctx, ada_w, ada_b, ln_g, ln_b, gmlp_w_in, gmlp_b_in, gmlp_v_g, gmlp_v_b, gmlp_w_s, gmlp_b_s, gmlp_w_out, mla_w_dqkv, mla_q_norm, mla_kv_norm, mla_w_uq, mla_w_ukv, mla_w_o, gla_w_in, gla_wa1, gla_wa2, gla_ba, gla_norm, gla_w_o, ffn_w13, ffn_w2, moe_router, moe_w13, moe_w2):
    n_ctx_seq, ctx_len, _ = x_prompt.shape
    n_lat_seq, lat_len, _ = x_sample.shape
    n_ctx = n_ctx_seq * ctx_len
    assert n_ctx == GROUP_TOKENS and lat_len == GROUP_TOKENS
    x = (x_prompt.reshape(n_ctx, D_MODEL), x_sample.reshape(-1, D_MODEL))

    cond = jnp.concatenate([c_ctx[None, :], c, jnp.zeros((ADA_ROWS - 1 - n_lat_seq, D_MODEL), F32)], axis=0)
    n_groups = 1 + n_lat_seq
    mods_all = _adaln_all(cond, ada_w, ada_b)[:, :n_groups].reshape(DEPTH, n_groups, 6, D_MODEL)

    moe_w13_b, moe_w2_b = moe_w13.astype(BF16), moe_w2.astype(BF16)

    new_ckv, new_krope, new_gla = [], [], []
    for i in range(DEPTH):
        mods = mods_all[i]
        kind, j = i % 3, i // 3
        lng, lnb = ln_g[i, 0], ln_b[i, 0]
        if kind == 0:
            x = _gmlp_layer(x, mods, gmlp_w_in[j], gmlp_b_in[j], gmlp_v_g[j], gmlp_v_b[j], gmlp_w_s[j],
                            gmlp_b_s[j], gmlp_w_out[j], lng, lnb)
        elif kind == 1:
            swap = _rope_swap_perm()
            wd = mla_w_dqkv[j]
            r0 = MLA_Q_RANK + MLA_KV_RANK
            wd = jnp.concatenate([wd, wd[:, r0:][:, swap]], axis=1).astype(BF16)
            wuq = mla_w_uq[j].reshape(MLA_Q_RANK, MLA_HEADS, MLA_D_QK)
            wuq_rope = wuq[:, :, MLA_D_NOPE:]
            wuq = jnp.concatenate([wuq[:, :, :MLA_D_NOPE].reshape(MLA_Q_RANK, -1),
                                   wuq_rope.reshape(MLA_Q_RANK, -1),
                                   wuq_rope[:, :, swap].reshape(MLA_Q_RANK, -1)], axis=1).astype(BF16)
            wukv = mla_w_ukv[j].reshape(MLA_KV_RANK, MLA_HEADS, MLA_D_NOPE + MLA_D_V)
            wukv = jnp.concatenate([wukv[:, :, :MLA_D_NOPE].reshape(MLA_KV_RANK, -1),
                                    wukv[:, :, MLA_D_NOPE:].reshape(MLA_KV_RANK, -1)], axis=1).astype(BF16)
            weights = (wd, mla_q_norm[j].reshape(1, -1), mla_kv_norm[j].reshape(1, -1), wuq, wukv)
            cos, sin = _rope_tables(lat_len)
            tables = (jnp.tile(cos, (1, MLA_HEADS)), jnp.tile(sin, (1, MLA_HEADS)), cos, sin)

            qc, kc, vc, ckv, krope = _mla_project(x, 0, n_ctx_seq, ctx_len, mods, weights, None)
            new_ckv.append(ckv.reshape(n_ctx_seq, ctx_len, MLA_KV_RANK))
            new_krope.append(krope.reshape(n_ctx_seq, ctx_len, MLA_D_ROPE))
            ql, kl, vl, _, _ = _mla_project(x, n_ctx, n_lat_seq, lat_len, mods, weights, tables)
            k_cache, v_cache = _mla_cache_keys(cache_mla_ckv[:, j], cache_mla_krope[:, j], wukv)
            a_ctx = _attention(qc, [kc], [vc]).reshape(n_ctx, -1)
            a_lat = _attention(ql, [k_cache, kl], [v_cache, vl]).reshape(n_lat_seq * lat_len, -1)
            x = _out_proj_layer(x, mods, a_ctx, a_lat, mla_w_o[j].astype(BF16), lng, lnb)
        else:
            proj = _gla_project(x, mods, gla_w_in[j], gla_wa1[j], gla_wa2[j], gla_ba[j])
            zero_state = jnp.zeros((n_ctx_seq, 2, GLA_HEADS, GLA_DV, GLA_DK), F32)
            lat_state = jnp.swapaxes(state_gla[:, j], -1, -2)
            ofc, obc, s_ctx = _gla_scan(proj, 0, n_ctx_seq, ctx_len, zero_state)
            ofl, obl, _ = _gla_scan(proj, n_ctx, n_lat_seq, lat_len, lat_state)
            new_gla.append(jnp.swapaxes(s_ctx, -1, -2))
            x = _gla_out_layer(x, mods, (ofc, ofl), (obc, obl), proj, gla_norm[j], gla_w_o[j], lng, lnb)

        lng, lnb = ln_g[i, 1], ln_b[i, 1]
        f = i // 2
        if i % 2 == 0:
            x = _ffn_layer(x, mods, ffn_w13[f].astype(BF16), ffn_w2[f].astype(BF16), lng, lnb)
        else:
            split = n_ctx if i == DEPTH - 1 else None
            x = _moe_layer(x, mods, moe_router[f], moe_w13_b, moe_w2_b, f, lng, lnb, split)

    y_prompt = x[0].reshape(n_ctx_seq, ctx_len, D_MODEL)
    y_sample = x[1].reshape(n_lat_seq, lat_len, D_MODEL)
    return (y_prompt, y_sample, jnp.stack(new_ckv, axis=1), jnp.stack(new_krope, axis=1),
            jnp.stack(new_gla, axis=1))
```

```python
import functools

import jax
import jax.numpy as jnp
from jax import lax
from jax.experimental import pallas as pl
from jax.experimental.pallas import tpu as pltpu

F32 = jnp.float32
BF16 = jnp.bfloat16
HIGHEST = lax.Precision.HIGHEST

D_MODEL = 1024
DEPTH = 4
GRID_W = 64
DEEPNORM_ALPHA = (2.0 * DEPTH) ** 0.25
LN_EPS = 1e-5
RMS_EPS = 1e-6
GROUP_TOKENS = 4096

CHUNK_A = 128
GMLP_WIDTH = 2 * D_MODEL
GMLP_GROUPS = 8
GMLP_GROUP_DIM = GMLP_WIDTH // GMLP_GROUPS

MLA_HEADS = 8
MLA_Q_RANK = D_MODEL // 2
MLA_KV_RANK = D_MODEL // 4
MLA_D_NOPE = 128
MLA_D_ROPE = 64
MLA_D_V = 128
MLA_D_QK = MLA_D_NOPE + MLA_D_ROPE
MLA_D_VP = 2 * MLA_D_V
LOG2_E = 1.4426950408889634
ROPE_BASE = 10000.0

GLA_HEADS = 4
GLA_DK = 128
GLA_DV = 256
GLA_DECAY_RANK = 16
GLA_GATE_NORM = 16.0
GLA_CHUNK = 64
GLA_DQK = GLA_HEADS * GLA_DK
GLA_DVT = GLA_HEADS * GLA_DV
GLA_QKVR_W = 2 * GLA_DQK + 2 * GLA_DVT

D_FF = 2816
N_EXPERTS = 8
TOP_K = 2
D_FF_EXPERT = 3584

VMEM_LIMIT_BYTES = 56 * 1024 * 1024
ADA_ROWS = 16
ADA_TN = 1536
TM_GMLP = 512
GMLP_COL_CHUNK = 512
TM_FFN = 512
TM_PROJ = 512
TQ_ATTN = 2048
TQ_SUB_ATTN = 256
TM_OUT = 512
TB_GLA = 512
TM_ROUTE = 512
TM_EXPERT = 512
TF_EXPERT = 1792
TM_COMBINE = 512
DMA_ISSUE_UNROLL = 8


def _cparams(*sem):
    return pltpu.CompilerParams(dimension_semantics=sem, vmem_limit_bytes=VMEM_LIMIT_BYTES)


def _layer_norm(y, g, b):
    mu = jnp.mean(y, axis=-1, keepdims=True)
    d = y - mu
    var = jnp.mean(d * d, axis=-1, keepdims=True)
    return d * lax.rsqrt(var + LN_EPS) * g + b


def _rms_norm(y, g):
    return y * lax.rsqrt(jnp.mean(y * y, axis=-1, keepdims=True) + RMS_EPS) * g


def _modulate(x, mod_ref, shift_row):
    return x * (1.0 + mod_ref[shift_row + 1:shift_row + 2, :]) + mod_ref[shift_row:shift_row + 1, :]


def _post_norm(x, out, mod_ref, gate_row, lng_ref, lnb_ref):
    y = DEEPNORM_ALPHA * x + mod_ref[gate_row:gate_row + 1, :] * out
    return _layer_norm(y, lng_ref[...], lnb_ref[...])


def _dot(a, b):
    return jnp.dot(a, b, preferred_element_type=F32)


def _dot_nt(a, b):
    return lax.dot_general(a, b, (((1,), (1,)), ((), ())), preferred_element_type=F32)


def _dot_tn(a, b):
    return lax.dot_general(a, b, (((0,), (0,)), ((), ())), preferred_element_type=F32)


def _mod_spec(tm):
    return pl.BlockSpec((None, 6, D_MODEL), lambda i, *_: ((i * tm) // GROUP_TOKENS, 0, 0))


def _full_spec(shape):
    nd = len(shape)
    return pl.BlockSpec(shape, lambda *_: (0,) * nd, pipeline_mode=pl.Buffered(1))


def _ada_kernel(c_ref, w_ref, b_ref, o_ref):
    c = c_ref[...]
    s = c * jax.nn.sigmoid(c)
    o_ref[...] = jnp.dot(s, w_ref[...], precision=HIGHEST, preferred_element_type=F32) + b_ref[...]


def _adaln_all(cond, ada_w, ada_b):
    n_out = 6 * D_MODEL
    return pl.pallas_call(
        _ada_kernel,
        grid=(DEPTH, n_out // ADA_TN),
        in_specs=[
            pl.BlockSpec((ADA_ROWS, D_MODEL), lambda l, j: (0, 0)),
            pl.BlockSpec((None, D_MODEL, ADA_TN), lambda l, j: (l, 0, j)),
            pl.BlockSpec((None, 1, ADA_TN), lambda l, j: (l, 0, j)),
        ],
        out_specs=pl.BlockSpec((None, ADA_ROWS, ADA_TN), lambda l, j: (l, 0, j)),
        out_shape=jax.ShapeDtypeStruct((DEPTH, ADA_ROWS, n_out), F32),
        compiler_params=_cparams("parallel", "parallel"),
        name="adaln",
    )(cond, ada_w, ada_b.reshape(DEPTH, 1, n_out))


def _gmlp_kernel(*refs, n_ctx_tiles):
    if n_ctx_tiles is None:
        x = refs[0][...]
        refs = refs[1:]
    else:
        x = jnp.where(pl.program_id(0) < n_ctx_tiles, refs[0][...], refs[1][...])
        refs = refs[2:]
    (mod_ref, win_ref, bin_ref, vg_ref, vb_ref, ws_ref, bs_ref, wout_ref, lng_ref, lnb_ref,
     o_ref, u_ref, v_ref, vn_ref, p_ref) = refs
    hb = _modulate(x, mod_ref, 0).astype(BF16)
    n_chunks = GMLP_WIDTH // GMLP_COL_CHUNK
    total = jnp.zeros((x.shape[0], 1), F32)
    total_sq = jnp.zeros((x.shape[0], 1), F32)
    for j in range(n_chunks):
        cols = slice(j * GMLP_COL_CHUNK, (j + 1) * GMLP_COL_CHUNK)
        wcols = slice(GMLP_WIDTH + j * GMLP_COL_CHUNK, GMLP_WIDTH + (j + 1) * GMLP_COL_CHUNK)
        v = jax.nn.gelu(_dot(hb, win_ref[:, wcols]) + bin_ref[:, wcols])
        v_ref[:, cols] = v
        total = total + jnp.sum(v, axis=-1, keepdims=True)
        total_sq = total_sq + jnp.sum(v * v, axis=-1, keepdims=True)
    mean = total * (1.0 / GMLP_WIDTH)
    rstd = lax.rsqrt(total_sq * (1.0 / GMLP_WIDTH) - mean * mean + LN_EPS)
    for j in range(n_chunks):
        cols = slice(j * GMLP_COL_CHUNK, (j + 1) * GMLP_COL_CHUNK)
        u_ref[:, cols] = jax.nn.gelu(_dot(hb, win_ref[:, cols]) + bin_ref[:, cols])
        vn_ref[:, cols] = ((v_ref[:, cols] - mean) * rstd * vg_ref[:, cols] + vb_ref[:, cols]).astype(BF16)
    for c in range(x.shape[0] // CHUNK_A):
        rows = slice(c * CHUNK_A, (c + 1) * CHUNK_A)
        for g in range(GMLP_GROUPS):
            cols = slice(g * GMLP_GROUP_DIM, (g + 1) * GMLP_GROUP_DIM)
            mixed = _dot(ws_ref[g], vn_ref[rows, cols]) + bs_ref[g]
            p_ref[rows, cols] = (u_ref[rows, cols] * mixed).astype(BF16)
    out = _dot(p_ref[...], wout_ref[...])
    o_ref[...] = _post_norm(x, out, mod_ref, 2, lng_ref, lnb_ref)


def _gmlp_layer(x, mods, w_in, b_in, v_g, v_b, w_s, b_s, w_out, ln_g, ln_b):
    tm = TM_GMLP
    row = pl.BlockSpec((tm, D_MODEL), lambda i: (i, 0))
    if isinstance(x, tuple):
        n_ctx_tiles = x[0].shape[0] // tm
        n = x[0].shape[0] + x[1].shape[0]
        x_specs = _ctx_lat_specs(tm, D_MODEL, n_ctx_tiles)
    else:
        n_ctx_tiles = None
        n = x.shape[0]
        x, x_specs = (x,), [row]
    return pl.pallas_call(
        functools.partial(_gmlp_kernel, n_ctx_tiles=n_ctx_tiles),
        grid=(n // tm,),
        in_specs=x_specs + [
            _mod_spec(tm),
            _full_spec((D_MODEL, 2 * GMLP_WIDTH)), _full_spec((1, 2 * GMLP_WIDTH)),
            _full_spec((1, GMLP_WIDTH)), _full_spec((1, GMLP_WIDTH)),
            _full_spec((GMLP_GROUPS, CHUNK_A, CHUNK_A)), _full_spec((GMLP_GROUPS, CHUNK_A, 1)),
            _full_spec((GMLP_WIDTH, D_MODEL)),
            _full_spec((1, D_MODEL)), _full_spec((1, D_MODEL)),
        ],
        out_specs=row,
        out_shape=jax.ShapeDtypeStruct((n, D_MODEL), F32),
        scratch_shapes=[pltpu.VMEM((tm, GMLP_WIDTH), F32), pltpu.VMEM((tm, GMLP_WIDTH), F32),
                        pltpu.VMEM((tm, GMLP_WIDTH), BF16), pltpu.VMEM((tm, GMLP_WIDTH), BF16)],
        compiler_params=_cparams("parallel"),
        name="gmlp",
    )(*x, mods, w_in.astype(BF16), b_in.reshape(1, -1), v_g.reshape(1, -1), v_b.reshape(1, -1),
      w_s.astype(BF16), b_s.reshape(GMLP_GROUPS, CHUNK_A, 1), w_out.astype(BF16),
      ln_g.reshape(1, -1), ln_b.reshape(1, -1))


def _ffn_kernel(x_ref, mod_ref, w13_ref, w2_ref, lng_ref, lnb_ref, o_ref):
    x = x_ref[...]
    hb = _modulate(x, mod_ref, 3).astype(BF16)
    a = _dot(hb, w13_ref[:, :D_FF])
    b = _dot(hb, w13_ref[:, D_FF:])
    out = _dot((a * jax.nn.sigmoid(a) * b).astype(BF16), w2_ref[...])
    o_ref[...] = _post_norm(x, out, mod_ref, 5, lng_ref, lnb_ref)


def _ffn_layer(x, mods, w13, w2, ln_g, ln_b):
    n = x.shape[0]
    tm = TM_FFN
    row = pl.BlockSpec((tm, D_MODEL), lambda i: (i, 0))
    return pl.pallas_call(
        _ffn_kernel,
        grid=(n // tm,),
        in_specs=[row, _mod_spec(tm), _full_spec(w13.shape), _full_spec(w2.shape),
                  _full_spec((1, D_MODEL)), _full_spec((1, D_MODEL))],
        out_specs=row,
        out_shape=jax.ShapeDtypeStruct((n, D_MODEL), F32),
        compiler_params=_cparams("parallel"),
        name="ffn",
    )(x, mods, w13, w2, ln_g.reshape(1, -1), ln_b.reshape(1, -1))


def _rope_swap_perm():
    j = jnp.arange(MLA_D_ROPE)
    half = MLA_D_ROPE // 4
    return jnp.where((j % (2 * half)) < half, j + half, j - half)


def _rope_tables(n_tokens):
    n_freq = MLA_D_ROPE // 4
    t = jnp.arange(n_tokens)
    pos_r = (t // GRID_W).astype(F32)[:, None]
    pos_c = (t % GRID_W).astype(F32)[:, None]
    inv = ROPE_BASE ** (-jnp.arange(n_freq, dtype=F32) / n_freq)
    ang_r, ang_c = pos_r * inv, pos_c * inv
    cos = jnp.concatenate([jnp.cos(ang_r), jnp.cos(ang_r), jnp.cos(ang_c), jnp.cos(ang_c)], axis=-1)
    sin = jnp.concatenate([-jnp.sin(ang_r), jnp.sin(ang_r), -jnp.sin(ang_c), jnp.sin(ang_c)], axis=-1)
    return cos, sin


def _write_heads(q_ref, k_ref, v_ref, q_nope, q_rope, kv, k_rope):
    scale = MLA_D_QK ** -0.5 * LOG2_E
    k_rope = k_rope.astype(BF16)
    ones = jnp.ones((kv.shape[0], MLA_D_VP - MLA_D_V), BF16)
    for h in range(MLA_HEADS):
        nope = slice(h * MLA_D_NOPE, (h + 1) * MLA_D_NOPE)
        if q_ref is not None:
            q_ref[h, :, :MLA_D_NOPE] = (q_nope[:, nope] * scale).astype(BF16)
            q_ref[h, :, MLA_D_NOPE:] = (q_rope[:, h * MLA_D_ROPE:(h + 1) * MLA_D_ROPE] * scale).astype(BF16)
        k_ref[h, :, :MLA_D_NOPE] = kv[:, nope].astype(BF16)
        k_ref[h, :, MLA_D_NOPE:] = k_rope
        vcol = MLA_HEADS * MLA_D_NOPE + h * MLA_D_V
        v_ref[h, :, :MLA_D_V] = kv[:, vcol:vcol + MLA_D_V].astype(BF16)
        v_ref[h, :, MLA_D_V:] = ones


def _mla_proj_kernel(*refs, rope):
    if rope:
        (x_ref, mod_ref, wd_ref, qg_ref, kvg_ref, wuq_ref, wukv_ref, cq_ref, sq_ref, ck_ref, sk_ref,
         q_ref, k_ref, v_ref, ckv_ref, kr_ref) = refs
    else:
        (x_ref, mod_ref, wd_ref, qg_ref, kvg_ref, wuq_ref, wukv_ref,
         q_ref, k_ref, v_ref, ckv_ref, kr_ref) = refs
    hb = _modulate(x_ref[...], mod_ref, 0).astype(BF16)
    c = _dot(hb, wd_ref[...])
    r0 = MLA_Q_RANK + MLA_KV_RANK
    c_q = _rms_norm(c[:, :MLA_Q_RANK], qg_ref[...]).astype(BF16)
    c_kv = _rms_norm(c[:, MLA_Q_RANK:r0], kvg_ref[...])
    k_rope = c[:, r0:r0 + MLA_D_ROPE]
    ckv_ref[...] = c_kv
    kr_ref[...] = k_rope
    q = _dot(c_q, wuq_ref[...])
    n_nope = MLA_HEADS * MLA_D_NOPE
    n_rope = MLA_HEADS * MLA_D_ROPE
    q_rope = q[:, n_nope:n_nope + n_rope]
    if rope:
        q_rope = q_rope * cq_ref[...] + q[:, n_nope + n_rope:] * sq_ref[...]
        k_rope = k_rope * ck_ref[...] + c[:, r0 + MLA_D_ROPE:] * sk_ref[...]
    kv = _dot(c_kv.astype(BF16), wukv_ref[...])
    _write_heads(q_ref, k_ref, v_ref, q[:, :n_nope], q_rope, kv, k_rope)


def _mla_project(x, row_offset, n_seq, seq_len, mods, weights, tables):
    wd, qg, kvg, wuq, wukv = weights
    tm = min(TM_PROJ, seq_len)
    per_seq = seq_len // tm
    n = n_seq * seq_len
    off = row_offset // tm
    rope = tables is not None
    in_specs = [
        pl.BlockSpec((tm, D_MODEL), lambda i: (off + i, 0)),
        pl.BlockSpec((None, 6, D_MODEL), lambda i: ((row_offset + i * tm) // GROUP_TOKENS, 0, 0)),
        _full_spec(wd.shape), _full_spec(qg.shape), _full_spec(kvg.shape),
        _full_spec(wuq.shape), _full_spec(wukv.shape),
    ]
    args = [x, mods, wd, qg, kvg, wuq, wukv]
    if rope:
        pos = lambda i: (i % per_seq, 0)
        in_specs += [pl.BlockSpec((tm, MLA_HEADS * MLA_D_ROPE), pos)] * 2 + [pl.BlockSpec((tm, MLA_D_ROPE), pos)] * 2
        args += list(tables)
    head_spec = lambda d: pl.BlockSpec((None, MLA_HEADS, tm, d), lambda i: (i // per_seq, 0, i % per_seq, 0))
    head_shape = lambda d: jax.ShapeDtypeStruct((n_seq, MLA_HEADS, seq_len, d), BF16)
    return pl.pallas_call(
        functools.partial(_mla_proj_kernel, rope=rope),
        grid=(n // tm,),
        in_specs=in_specs,
        out_specs=[head_spec(MLA_D_QK), head_spec(MLA_D_QK), head_spec(MLA_D_VP),
                   pl.BlockSpec((tm, MLA_KV_RANK), lambda i: (i, 0)),
                   pl.BlockSpec((tm, MLA_D_ROPE), lambda i: (i, 0))],
        out_shape=[head_shape(MLA_D_QK), head_shape(MLA_D_QK), head_shape(MLA_D_VP),
                   jax.ShapeDtypeStruct((n, MLA_KV_RANK), F32),
                   jax.ShapeDtypeStruct((n, MLA_D_ROPE), F32)],
        compiler_params=_cparams("parallel"),
        name="mla_proj_rope" if rope else "mla_proj",
    )(*args)


def _mla_cache_kernel(ckv_ref, kr_ref, wukv_ref, k_ref, v_ref):
    kv = _dot(ckv_ref[...].astype(BF16), wukv_ref[...])
    _write_heads(None, k_ref, v_ref, None, None, kv, kr_ref[...])


def _mla_cache_keys(ckv, krope, wukv):
    b, l, _ = ckv.shape
    head_spec = lambda d: pl.BlockSpec((None, MLA_HEADS, l, d), lambda i: (i, 0, 0, 0))
    return pl.pallas_call(
        _mla_cache_kernel,
        grid=(b,),
        in_specs=[pl.BlockSpec((None, l, MLA_KV_RANK), lambda i: (i, 0, 0)),
                  pl.BlockSpec((None, l, MLA_D_ROPE), lambda i: (i, 0, 0)),
                  _full_spec(wukv.shape)],
        out_specs=[head_spec(MLA_D_QK), head_spec(MLA_D_VP)],
        out_shape=[jax.ShapeDtypeStruct((b, MLA_HEADS, l, MLA_D_QK), BF16),
                   jax.ShapeDtypeStruct((b, MLA_HEADS, l, MLA_D_VP), BF16)],
        compiler_params=_cparams("parallel"),
        name="mla_cache_keys",
    )(ckv, krope, wukv)


def _attn_kernel(*refs, n_seg):
    q_ref = refs[0]
    k_refs = refs[1:1 + n_seg]
    v_refs = refs[1 + n_seg:1 + 2 * n_seg]
    o_ref = refs[1 + 2 * n_seg]
    s_ref, p_ref = refs[2 + 2 * n_seg:]
    n_buf, sub, _ = s_ref.shape
    n_sub = q_ref.shape[0] // sub
    seg_cols = []
    col = 0
    for k_ref in k_refs:
        seg_cols.append(slice(col, col + k_ref.shape[0]))
        col += k_ref.shape[0]

    def scores(i):
        q = q_ref[i * sub:(i + 1) * sub, :]
        for k_ref, cols in zip(k_refs, seg_cols):
            s_ref[i % n_buf, :, cols] = _dot_nt(q, k_ref[...])

    def finish(i):
        s = s_ref[i % n_buf]
        m = jnp.max(s, axis=-1, keepdims=True)
        p_ref[i % n_buf] = jnp.exp2((s - m).astype(BF16))
        acc = sum(_dot(p_ref[i % n_buf, :, cols], v_ref[...]) for v_ref, cols in zip(v_refs, seg_cols))
        o_ref[i * sub:(i + 1) * sub, :] = (acc[:, :MLA_D_V] / acc[:, MLA_D_V:MLA_D_V + 1]).astype(BF16)

    scores(0)
    for i in range(n_sub):
        if i + 1 < n_sub:
            scores(i + 1)
        finish(i)


def _attention(q, ks, vs):
    b, h, t, _ = q.shape
    tq = min(TQ_ATTN, t)
    sub = min(TQ_SUB_ATTN, tq)
    n_buf = min(2, tq // sub)
    n_keys = sum(a.shape[2] for a in ks)
    seg = lambda a: pl.BlockSpec((None, None) + a.shape[2:], lambda bi, hi, qi: (bi, hi, 0, 0))
    return pl.pallas_call(
        functools.partial(_attn_kernel, n_seg=len(ks)),
        grid=(b, h, t // tq),
        in_specs=[pl.BlockSpec((None, None, tq, MLA_D_QK), lambda bi, hi, qi: (bi, hi, qi, 0))]
        + [seg(a) for a in ks] + [seg(a) for a in vs],
        out_specs=pl.BlockSpec((None, tq, MLA_D_V), lambda bi, hi, qi: (bi, qi, hi)),
        out_shape=jax.ShapeDtypeStruct((b, t, h * MLA_D_V), BF16),
        scratch_shapes=[pltpu.VMEM((n_buf, sub, n_keys), F32), pltpu.VMEM((n_buf, sub, n_keys), BF16)],
        compiler_params=_cparams("parallel", "parallel", "arbitrary"),
        name="mla_attention",
    )(q, *ks, *vs)


def _ctx_lat_specs(tm, width, n_ctx_tiles, col=0):
    return [pl.BlockSpec((tm, width), lambda i: (jnp.minimum(i, n_ctx_tiles - 1), col)),
            pl.BlockSpec((tm, width), lambda i: (jnp.maximum(i - n_ctx_tiles, 0), col))]


def _out_proj_kernel(x_ref, mod_ref, ac_ref, al_ref, w_ref, lng_ref, lnb_ref, o_ref, *, n_ctx_tiles):
    def finish(a_ref):
        out = _dot(a_ref[...], w_ref[...])
        o_ref[...] = _post_norm(x_ref[...], out, mod_ref, 2, lng_ref, lnb_ref)

    is_ctx = pl.program_id(0) < n_ctx_tiles
    pl.when(is_ctx)(lambda: finish(ac_ref))
    pl.when(jnp.logical_not(is_ctx))(lambda: finish(al_ref))


def _out_proj_layer(x, mods, a_ctx, a_lat, w_o, ln_g, ln_b):
    n = x.shape[0]
    tm = TM_OUT
    n_ctx_tiles = a_ctx.shape[0] // tm
    row = pl.BlockSpec((tm, D_MODEL), lambda i: (i, 0))
    return pl.pallas_call(
        functools.partial(_out_proj_kernel, n_ctx_tiles=n_ctx_tiles),
        grid=(n // tm,),
        in_specs=[row, _mod_spec(tm)] + _ctx_lat_specs(tm, a_ctx.shape[1], n_ctx_tiles)
        + [_full_spec(w_o.shape), _full_spec((1, D_MODEL)), _full_spec((1, D_MODEL))],
        out_specs=row,
        out_shape=jax.ShapeDtypeStruct((n, D_MODEL), F32),
        compiler_params=_cparams("parallel"),
        name="mla_out_proj",
    )(x, mods, a_ctx, a_lat, w_o, ln_g.reshape(1, -1), ln_b.reshape(1, -1))


def _gla_proj_kernel(x_ref, mod_ref, win_ref, wa2_ref, ba_ref, qkvr_ref, g_ref):
    h = _modulate(x_ref[...], mod_ref, 0).astype(BF16)
    proj = _dot(h, win_ref[...])
    qkvr_ref[:, :GLA_DQK] = (proj[:, :GLA_DQK] * (GLA_DK ** -0.5)).astype(BF16)
    qkvr_ref[:, GLA_DQK:] = proj[:, GLA_DQK:GLA_QKVR_W].astype(BF16)
    logit = _dot(proj[:, GLA_QKVR_W:].astype(BF16), wa2_ref[...]) + ba_ref[...]
    log_sig = jnp.minimum(logit, 0.0) - jnp.log(1.0 + jnp.exp(-jnp.abs(logit)))
    g_ref[...] = log_sig / GLA_GATE_NORM


def _gla_project(x, mods, w_in, wa1, wa2, ba):
    n = x.shape[0]
    tm = TM_PROJ
    lane = 128
    low_w = 2 * GLA_DECAY_RANK
    w_ext = jnp.concatenate([w_in, wa1[0], wa1[1], jnp.zeros((D_MODEL, lane - low_w), F32)], axis=1)
    zeros = jnp.zeros((GLA_DECAY_RANK, GLA_DQK), F32)
    wa2_blk = jnp.concatenate([jnp.concatenate([wa2[0], zeros], axis=1),
                               jnp.concatenate([zeros, wa2[1]], axis=1),
                               jnp.zeros((lane - low_w, 2 * GLA_DQK), F32)], axis=0)
    return pl.pallas_call(
        _gla_proj_kernel,
        grid=(n // tm,),
        in_specs=[pl.BlockSpec((tm, D_MODEL), lambda i: (i, 0)), _mod_spec(tm),
                  _full_spec(w_ext.shape), _full_spec(wa2_blk.shape), _full_spec((1, 2 * GLA_DQK))],
        out_specs=[pl.BlockSpec((tm, GLA_QKVR_W), lambda i: (i, 0)),
                   pl.BlockSpec((tm, 2 * GLA_DQK), lambda i: (i, 0))],
        out_shape=[jax.ShapeDtypeStruct((n, GLA_QKVR_W), BF16), jax.ShapeDtypeStruct((n, 2 * GLA_DQK), F32)],
        compiler_params=_cparams("parallel"),
        name="gla_proj",
    )(x, mods, w_ext.astype(BF16), wa2_blk.astype(BF16), ba.reshape(1, -1))


def _chunk_cumsum(g, reverse):
    n = g.shape[0]
    pos = lax.broadcasted_iota(jnp.int32, g.shape, 0) % GLA_CHUNK
    b = g
    shift = 1
    while shift < GLA_CHUNK:
        if reverse:
            moved = pltpu.roll(b, n - shift, 0)
            keep = pos < GLA_CHUNK - shift
        else:
            moved = pltpu.roll(b, shift, 0)
            keep = pos >= shift
        b = b + jnp.where(keep, moved, 0.0)
        shift *= 2
    return b


def _gla_scan_kernel(qf_ref, kf_ref, vf_ref, gf_ref, qb_ref, kb_ref, vb_ref, gb_ref, s0_ref,
                     of_ref, ob_ref, sfin_ref, st_ref):
    i = pl.program_id(2)

    @pl.when(i == 0)
    def _():
        st_ref[...] = s0_ref[...]

    n_chunks = qf_ref.shape[0] // GLA_CHUNK
    ii = lax.broadcasted_iota(jnp.int32, (GLA_CHUNK, GLA_CHUNK), 0)
    jj = lax.broadcasted_iota(jnp.int32, (GLA_CHUNK, GLA_CHUNK), 1)
    dirs = ((0, False, qf_ref, kf_ref, vf_ref, of_ref, gf_ref, jj <= ii),
            (1, True, qb_ref, kb_ref, vb_ref, ob_ref, gb_ref, jj >= ii))
    for d, reverse, q_ref, k_ref, v_ref, o_ref, g_ref, mask in dirs:
        b_all = _chunk_cumsum(g_ref[...], reverse)
        qes, intra, updates, decays = [], [], [], []
        for c in range(n_chunks):
            rows = slice(c * GLA_CHUNK, (c + 1) * GLA_CHUNK)
            b = b_all[rows]
            b_tot = b[:1] if reverse else b[GLA_CHUNK - 1:]
            q = q_ref[rows, :].astype(F32)
            k = k_ref[rows, :].astype(F32)
            v = v_ref[rows, :]
            qe = (q * jnp.exp(b)).astype(BF16)
            ke = (k * jnp.exp(-b)).astype(BF16)
            kd = (k * jnp.exp(b_tot - b)).astype(BF16)
            a = jnp.where(mask, _dot_nt(qe, ke), 0.0).astype(BF16)
            qes.append(qe)
            intra.append(_dot(a, v))
            updates.append(_dot_tn(v, kd))
            decays.append(jnp.exp(b_tot))
        st = st_ref[d]
        entering = [None] * n_chunks
        for c in (reversed(range(n_chunks)) if reverse else range(n_chunks)):
            entering[c] = st.astype(BF16)
            st = st * decays[c] + updates[c]
        st_ref[d] = st
        for c in range(n_chunks):
            rows = slice(c * GLA_CHUNK, (c + 1) * GLA_CHUNK)
            o_ref[rows, :] = (intra[c] + _dot_nt(qes[c], entering[c])).astype(o_ref.dtype)

    @pl.when(i == pl.num_programs(2) - 1)
    def _():
        sfin_ref[...] = st_ref[...]


def _gla_scan(qkvr, g, row_offset, n_seq, seq_len, s0_t):
    tb = min(TB_GLA, seq_len)
    nb = seq_len // tb
    off = row_offset // tb
    v_col0 = 2 * GLA_DQK // GLA_DV

    def in_specs(reverse):
        blk = (lambda s, i: off + s * nb + nb - 1 - i) if reverse else (lambda s, i: off + s * nb + i)
        g_col = GLA_HEADS if reverse else 0
        return [pl.BlockSpec((tb, GLA_DK), lambda s, h, i: (blk(s, i), h)),
                pl.BlockSpec((tb, GLA_DK), lambda s, h, i: (blk(s, i), GLA_HEADS + h)),
                pl.BlockSpec((tb, GLA_DV), lambda s, h, i: (blk(s, i), v_col0 + h)),
                pl.BlockSpec((tb, GLA_DK), lambda s, h, i: (blk(s, i), g_col + h))]

    state_spec = pl.BlockSpec((None, 2, None, GLA_DV, GLA_DK), lambda s, h, i: (s, 0, h, 0, 0))
    o_shape = jax.ShapeDtypeStruct((n_seq * seq_len, GLA_DVT), BF16)
    return pl.pallas_call(
        _gla_scan_kernel,
        grid=(n_seq, GLA_HEADS, nb),
        in_specs=in_specs(False) + in_specs(True) + [state_spec],
        out_specs=[pl.BlockSpec((tb, GLA_DV), lambda s, h, i: (s * nb + i, h)),
                   pl.BlockSpec((tb, GLA_DV), lambda s, h, i: (s * nb + nb - 1 - i, h)),
                   state_spec],
        out_shape=[o_shape, o_shape, jax.ShapeDtypeStruct((n_seq, 2, GLA_HEADS, GLA_DV, GLA_DK), F32)],
        scratch_shapes=[pltpu.VMEM((2, GLA_DV, GLA_DK), F32)],
        compiler_params=_cparams("parallel", "parallel", "arbitrary"),
        name="gla_scan",
    )(qkvr, qkvr, qkvr, g, qkvr, qkvr, qkvr, g, s0_t)


def _gla_out_kernel(x_ref, mod_ref, ofc_ref, ofl_ref, obc_ref, obl_ref, r_ref, ng_ref, wo_ref, lng_ref, lnb_ref,
                    o_ref, p_ref, *, n_ctx_tiles):
    def finish(of_ref, ob_ref):
        for h in range(GLA_HEADS):
            cols = slice(h * GLA_DV, (h + 1) * GLA_DV)
            o = _rms_norm(of_ref[:, cols].astype(F32) + ob_ref[:, cols].astype(F32), ng_ref[...])
            r = r_ref[:, cols].astype(F32)
            p_ref[:, cols] = (o * (r * jax.nn.sigmoid(r))).astype(BF16)
        out = _dot(p_ref[...], wo_ref[...])
        o_ref[...] = _post_norm(x_ref[...], out, mod_ref, 2, lng_ref, lnb_ref)

    is_ctx = pl.program_id(0) < n_ctx_tiles
    pl.when(is_ctx)(lambda: finish(ofc_ref, obc_ref))
    pl.when(jnp.logical_not(is_ctx))(lambda: finish(ofl_ref, obl_ref))


def _gla_out_layer(x, mods, o_fwd, o_bwd, proj, norm_g, w_o, ln_g, ln_b):
    n = x.shape[0]
    tm = TM_OUT
    n_ctx_tiles = o_fwd[0].shape[0] // tm
    row = pl.BlockSpec((tm, D_MODEL), lambda i: (i, 0))
    r_block = (2 * GLA_DQK + GLA_DVT) // GLA_DVT
    return pl.pallas_call(
        functools.partial(_gla_out_kernel, n_ctx_tiles=n_ctx_tiles),
        grid=(n // tm,),
        in_specs=[row, _mod_spec(tm)] + _ctx_lat_specs(tm, GLA_DVT, n_ctx_tiles) * 2
        + [pl.BlockSpec((tm, GLA_DVT), lambda i: (i, r_block)),
           _full_spec((1, GLA_DV)), _full_spec(w_o.shape),
           _full_spec((1, D_MODEL)), _full_spec((1, D_MODEL))],
        out_specs=row,
        out_shape=jax.ShapeDtypeStruct((n, D_MODEL), F32),
        scratch_shapes=[pltpu.VMEM((tm, GLA_DVT), BF16)],
        compiler_params=_cparams("parallel"),
        name="gla_out_proj",
    )(x, mods, *o_fwd, *o_bwd, proj, norm_g.reshape(1, -1), w_o.astype(BF16),
      ln_g.reshape(1, -1), ln_b.reshape(1, -1))


def _router_kernel(x_ref, mod_ref, wr_ref, idx_ref, wt_ref):
    h = _modulate(x_ref[...], mod_ref, 3)
    logits = lax.dot_general(wr_ref[...], h, (((1,), (1,)), ((), ())), precision=HIGHEST,
                             preferred_element_type=F32)
    e = lax.broadcasted_iota(jnp.int32, logits.shape, 0)
    m1 = jnp.max(logits, axis=0, keepdims=True)
    i1 = jnp.min(jnp.where(logits == m1, e, N_EXPERTS), axis=0, keepdims=True)
    rest = jnp.where(e == i1, -jnp.inf, logits)
    m2 = jnp.max(rest, axis=0, keepdims=True)
    i2 = jnp.min(jnp.where(rest == m2, e, N_EXPERTS), axis=0, keepdims=True)
    z = jnp.exp(m2 - m1)
    idx_ref[0:1, :] = i1
    idx_ref[1:2, :] = i2
    wt_ref[0:1, :] = 1.0 / (1.0 + z)
    wt_ref[1:2, :] = z / (1.0 + z)


def _route(x, mods, w_router):
    n = x.shape[0]
    tm = TM_ROUTE
    row = pl.BlockSpec((tm, D_MODEL), lambda i: (i, 0))
    pair = pl.BlockSpec((TOP_K, tm), lambda i: (0, i))
    return pl.pallas_call(
        _router_kernel,
        grid=(n // tm,),
        in_specs=[row, _mod_spec(tm), _full_spec((N_EXPERTS, D_MODEL))],
        out_specs=[pair, pair],
        out_shape=[jax.ShapeDtypeStruct((TOP_K, n), jnp.int32),
                   jax.ShapeDtypeStruct((TOP_K, n), F32)],
        compiler_params=_cparams("parallel"),
        name="moe_router",
    )(x, mods, w_router.T)


def _dispatch_plan(expert_idx, n_tiles):
    n = expert_idx.shape[1]
    flat = expert_idx.reshape(-1)
    onehot = (flat[:, None] == jnp.arange(N_EXPERTS)[None, :]).astype(jnp.int32)
    running = jnp.cumsum(onehot, axis=0)
    rank = jnp.sum(running * onehot, axis=1) - 1
    counts = running[-1]
    tiles = (counts + TM_EXPERT - 1) // TM_EXPERT
    tile_end = jnp.cumsum(tiles)
    tile_start = tile_end - tiles
    slot = jnp.sum(onehot * tile_start[None, :], axis=1) * TM_EXPERT + rank
    tile_id = jnp.arange(n_tiles)
    tile_active = (tile_id < tile_end[-1]).astype(jnp.int32)
    last_tile = jnp.minimum(tile_id, tile_end[-1] - 1)
    tile_expert = jnp.sum((tile_end[None, :] <= last_tile[:, None]).astype(jnp.int32), axis=1)
    idle = tile_end[-1] + jnp.arange(n_tiles - (TOP_K * n) // TM_EXPERT)
    fill_tiles = jnp.concatenate([jnp.where(tiles > 0, tile_end - 1, -1), jnp.where(idle < n_tiles, idle, -1)])
    return (slot.reshape(TOP_K, n).astype(jnp.int32), tile_expert.astype(jnp.int32), tile_active,
            fill_tiles.astype(jnp.int32))


def _row_copies_wait(hbm_ref, buf_ref, sem):
    pltpu.make_async_copy(hbm_ref.at[pl.ds(0, buf_ref.shape[0]), :], buf_ref, sem).wait()


def _dispatch_kernel(s1_ref, s2_ref, fill_ref, x_ref, mod_ref, xs_hbm, hbuf_ref, zbuf_ref, sems, zsem):
    i, n_steps = pl.program_id(0), pl.num_programs(0)
    tm = x_ref.shape[0]
    b = i % 2

    def wait_rows(buf):
        for _ in range(TOP_K):
            _row_copies_wait(xs_hbm, hbuf_ref.at[buf], sems.at[buf])

    @pl.when(i == 0)
    def _():
        zbuf_ref[...] = jnp.zeros_like(zbuf_ref)
        rows = zbuf_ref.shape[0]

        def fill(k):
            start = pl.multiple_of(fill_ref[k] * rows, rows)
            return pltpu.make_async_copy(zbuf_ref, xs_hbm.at[pl.ds(start, rows), :], zsem)

        for k in range(fill_ref.shape[0]):
            pl.when(fill_ref[k] >= 0)(lambda k=k: fill(k).start())
        for k in range(fill_ref.shape[0]):
            pl.when(fill_ref[k] >= 0)(lambda k=k: fill(k).wait())

    @pl.when(i >= 2)
    def _():
        wait_rows(b)

    hbuf_ref[b] = _modulate(x_ref[...], mod_ref, 3)
    base = i * tm

    def issue(r, carry):
        for s_ref in (s1_ref, s2_ref):
            pltpu.make_async_copy(hbuf_ref.at[b, pl.ds(r, 1), :],
                                  xs_hbm.at[pl.ds(s_ref[base + r], 1), :], sems.at[b]).start()
        return carry

    lax.fori_loop(0, tm, issue, 0, unroll=DMA_ISSUE_UNROLL)

    @pl.when(i == n_steps - 1)
    def _():
        wait_rows(b)

    @pl.when(jnp.logical_and(i == n_steps - 1, i >= 1))
    def _():
        wait_rows(1 - b)


def _dispatch(x, mods, slot_of, fill_tiles, n_slots):
    n = x.shape[0]
    tm = TM_ROUTE
    return pl.pallas_call(
        _dispatch_kernel,
        grid_spec=pltpu.PrefetchScalarGridSpec(
            num_scalar_prefetch=3,
            grid=(n // tm,),
            in_specs=[pl.BlockSpec((tm, D_MODEL), lambda i, *_: (i, 0)),
                      pl.BlockSpec((None, 6, D_MODEL), lambda i, *_: ((i * tm) // GROUP_TOKENS, 0, 0))],
            out_specs=pl.BlockSpec(memory_space=pl.ANY),
            scratch_shapes=[pltpu.VMEM((2, tm, D_MODEL), F32), pltpu.VMEM((TM_EXPERT, D_MODEL), F32),
                            pltpu.SemaphoreType.DMA((2,)), pltpu.SemaphoreType.DMA(())],
        ),
        out_shape=jax.ShapeDtypeStruct((n_slots, D_MODEL), F32),
        compiler_params=_cparams("arbitrary"),
        name="moe_dispatch",
    )(slot_of[0], slot_of[1], fill_tiles, x, mods)


def _expert_kernel(te_ref, ta_ref, x_ref, w1_ref, w3_ref, w2_ref, o_ref, xb_ref, acc_ref):
    i, j = pl.program_id(0), pl.program_id(1)
    last = pl.num_programs(1) - 1
    active = ta_ref[i] == 1

    @pl.when(jnp.logical_and(j == 0, active))
    def _():
        xb_ref[...] = x_ref[...].astype(BF16)

    @pl.when(active)
    def _():
        x = xb_ref[...]
        a = _dot(x, w1_ref[...])
        b = _dot(x, w3_ref[...])
        part = _dot((a * jax.nn.sigmoid(a) * b).astype(BF16), w2_ref[...])

        @pl.when(j == 0)
        def _():
            acc_ref[...] = part

        @pl.when(j > 0)
        def _():
            acc_ref[...] += part

    @pl.when(jnp.logical_and(j == last, active))
    def _():
        o_ref[...] = acc_ref[...]

    @pl.when(jnp.logical_and(j == last, jnp.logical_not(active)))
    def _():
        o_ref[...] = jnp.zeros_like(o_ref)


def _expert_ffn(xs, tile_expert, tile_active, w13, w2, layer):
    tm, tf = TM_EXPERT, TF_EXPERT
    nf = D_FF_EXPERT // tf
    n_slots = xs.shape[0]
    row = pl.BlockSpec((tm, D_MODEL), lambda i, j, te, ta: (i, 0))
    return pl.pallas_call(
        _expert_kernel,
        grid_spec=pltpu.PrefetchScalarGridSpec(
            num_scalar_prefetch=2,
            grid=(n_slots // tm, nf),
            in_specs=[
                row,
                pl.BlockSpec((None, None, D_MODEL, tf), lambda i, j, te, ta: (layer, te[i], 0, j)),
                pl.BlockSpec((None, None, D_MODEL, tf), lambda i, j, te, ta: (layer, te[i], 0, j + nf)),
                pl.BlockSpec((None, None, tf, D_MODEL), lambda i, j, te, ta: (layer, te[i], j, 0)),
            ],
            out_specs=row,
            scratch_shapes=[pltpu.VMEM((tm, D_MODEL), BF16), pltpu.VMEM((tm, D_MODEL), F32)],
        ),
        out_shape=jax.ShapeDtypeStruct((n_slots, D_MODEL), F32),
        compiler_params=_cparams("parallel", "arbitrary"),
        name="moe_experts",
    )(tile_expert, tile_active, xs, w13, w13, w2)


def _combine_kernel(s1_ref, s2_ref, x_ref, mod_ref, wt_ref, y_hbm, lng_ref, lnb_ref, *refs, n_ctx_tiles):
    ybuf_ref, sems = refs[-2:]
    o_refs = refs[:-2]
    i, n_steps = pl.program_id(0), pl.num_programs(0)
    tm = x_ref.shape[0]

    def issue_tile(tile, buf):
        base = tile * tm

        def issue(r, carry):
            for k, s_ref in enumerate((s1_ref, s2_ref)):
                pltpu.make_async_copy(y_hbm.at[pl.ds(s_ref[base + r], 1), :],
                                      ybuf_ref.at[buf, k, pl.ds(r, 1), :], sems.at[buf]).start()
            return carry

        lax.fori_loop(0, tm, issue, 0, unroll=DMA_ISSUE_UNROLL)

    @pl.when(i == 0)
    def _():
        issue_tile(0, 0)

    @pl.when(i + 1 < n_steps)
    def _():
        issue_tile(i + 1, (i + 1) % 2)

    b = i % 2
    for k in range(TOP_K):
        _row_copies_wait(y_hbm, ybuf_ref.at[b, k], sems.at[b])
    y = wt_ref[:, 0:1] * ybuf_ref[b, 0] + wt_ref[:, 1:2] * ybuf_ref[b, 1]
    out = _post_norm(x_ref[...], y, mod_ref, 5, lng_ref, lnb_ref)
    if n_ctx_tiles is None:
        o_refs[0][...] = out
    else:
        is_ctx = i < n_ctx_tiles

        @pl.when(is_ctx)
        def _():
            o_refs[0][...] = out

        @pl.when(jnp.logical_not(is_ctx))
        def _():
            o_refs[1][...] = out


def _combine(x, mods, slot_of, weights_t, ys, ln_g, ln_b, n_ctx_split=None):
    n = x.shape[0]
    tm = TM_COMBINE
    row = pl.BlockSpec((tm, D_MODEL), lambda i, s1, s2: (i, 0))
    if n_ctx_split is None:
        n_ctx_tiles = None
        out_specs, out_shape = row, jax.ShapeDtypeStruct((n, D_MODEL), F32)
    else:
        n_ctx_tiles = n_ctx_split // tm
        out_specs = [pl.BlockSpec((tm, D_MODEL), lambda i, s1, s2: (jnp.minimum(i, n_ctx_tiles - 1), 0)),
                     pl.BlockSpec((tm, D_MODEL), lambda i, s1, s2: (jnp.maximum(i - n_ctx_tiles, 0), 0))]
        out_shape = [jax.ShapeDtypeStruct((n_ctx_split, D_MODEL), F32),
                     jax.ShapeDtypeStruct((n - n_ctx_split, D_MODEL), F32)]
    return pl.pallas_call(
        functools.partial(_combine_kernel, n_ctx_tiles=n_ctx_tiles),
        grid_spec=pltpu.PrefetchScalarGridSpec(
            num_scalar_prefetch=2,
            grid=(n // tm,),
            in_specs=[row,
                      pl.BlockSpec((None, 6, D_MODEL), lambda i, s1, s2: ((i * tm) // GROUP_TOKENS, 0, 0)),
                      pl.BlockSpec((tm, TOP_K), lambda i, s1, s2: (i, 0)),
                      pl.BlockSpec(memory_space=pl.ANY),
                      pl.BlockSpec((1, D_MODEL), lambda i, s1, s2: (0, 0)),
                      pl.BlockSpec((1, D_MODEL), lambda i, s1, s2: (0, 0))],
            out_specs=out_specs,
            scratch_shapes=[pltpu.VMEM((2, TOP_K, tm, D_MODEL), F32), pltpu.SemaphoreType.DMA((2,))],
        ),
        out_shape=out_shape,
        compiler_params=_cparams("arbitrary"),
        name="moe_combine",
    )(slot_of[0], slot_of[1], x, mods, weights_t, ys, ln_g.reshape(1, -1), ln_b.reshape(1, -1))


def _moe_layer(x, mods, w_router, w13, w2, layer, ln_g, ln_b, n_ctx_split=None):
    n = x.shape[0]
    n_tiles = (TOP_K * n) // TM_EXPERT + N_EXPERTS
    expert_idx, weights = _route(x, mods, w_router)
    slot_of, tile_expert, tile_active, fill_tiles = _dispatch_plan(expert_idx, n_tiles)
    xs = _dispatch(x, mods, slot_of, fill_tiles, n_tiles * TM_EXPERT)
    ys = _expert_ffn(xs, tile_expert, tile_active, w13, w2, layer)
    return _combine(x, mods, slot_of, weights.T, ys, ln_g, ln_b, n_ctx_split)


def kernel(x_prompt, x_sample, cache_mla_ckv, cache_mla_krope, state_gla, c, c_ctx, ada_w, ada_b, ln_g, ln_b, gmlp_w_in, gmlp_b_in, gmlp_v_g, gmlp_v_b, gmlp_w_s, gmlp_b_s, gmlp_w_out, mla_w_dqkv, mla_q_norm, mla_kv_norm, mla_w_uq, mla_w_ukv, mla_w_o, gla_w_in, gla_wa1, gla_wa2, gla_ba, gla_norm, gla_w_o, ffn_w13, ffn_w2, moe_router, moe_w13, moe_w2):
    n_ctx_seq, ctx_len, _ = x_prompt.shape
    n_lat_seq, lat_len, _ = x_sample.shape
    n_ctx = n_ctx_seq * ctx_len
    assert n_ctx == GROUP_TOKENS and lat_len == GROUP_TOKENS
    x = (x_prompt.reshape(n_ctx, D_MODEL), x_sample.reshape(-1, D_MODEL))

    cond = jnp.concatenate([c_ctx[None, :], c, jnp.zeros((ADA_ROWS - 1 - n_lat_seq, D_MODEL), F32)], axis=0)
    n_groups = 1 + n_lat_seq
    mods_all = _adaln_all(cond, ada_w, ada_b)[:, :n_groups].reshape(DEPTH, n_groups, 6, D_MODEL)

    moe_w13_b, moe_w2_b = moe_w13.astype(BF16), moe_w2.astype(BF16)

    new_ckv, new_krope, new_gla = [], [], []
    for i in range(DEPTH):
        mods = mods_all[i]
        kind, j = i % 3, i // 3
        lng, lnb = ln_g[i, 0], ln_b[i, 0]
        if kind == 0:
            x = _gmlp_layer(x, mods, gmlp_w_in[j], gmlp_b_in[j], gmlp_v_g[j], gmlp_v_b[j], gmlp_w_s[j],
                            gmlp_b_s[j], gmlp_w_out[j], lng, lnb)
        elif kind == 1:
            swap = _rope_swap_perm()
            wd = mla_w_dqkv[j]
            r0 = MLA_Q_RANK + MLA_KV_RANK
            wd = jnp.concatenate([wd, wd[:, r0:][:, swap]], axis=1).astype(BF16)
            wuq = mla_w_uq[j].reshape(MLA_Q_RANK, MLA_HEADS, MLA_D_QK)
            wuq_rope = wuq[:, :, MLA_D_NOPE:]
            wuq = jnp.concatenate([wuq[:, :, :MLA_D_NOPE].reshape(MLA_Q_RANK, -1),
                                   wuq_rope.reshape(MLA_Q_RANK, -1),
                                   wuq_rope[:, :, swap].reshape(MLA_Q_RANK, -1)], axis=1).astype(BF16)
            wukv = mla_w_ukv[j].reshape(MLA_KV_RANK, MLA_HEADS, MLA_D_NOPE + MLA_D_V)
            wukv = jnp.concatenate([wukv[:, :, :MLA_D_NOPE].reshape(MLA_KV_RANK, -1),
                                    wukv[:, :, MLA_D_NOPE:].reshape(MLA_KV_RANK, -1)], axis=1).astype(BF16)
            weights = (wd, mla_q_norm[j].reshape(1, -1), mla_kv_norm[j].reshape(1, -1), wuq, wukv)
            cos, sin = _rope_tables(lat_len)
            tables = (jnp.tile(cos, (1, MLA_HEADS)), jnp.tile(sin, (1, MLA_HEADS)), cos, sin)

            qc, kc, vc, ckv, krope = _mla_project(x, 0, n_ctx_seq, ctx_len, mods, weights, None)
            new_ckv.append(ckv.reshape(n_ctx_seq, ctx_len, MLA_KV_RANK))
            new_krope.append(krope.reshape(n_ctx_seq, ctx_len, MLA_D_ROPE))
            ql, kl, vl, _, _ = _mla_project(x, n_ctx, n_lat_seq, lat_len, mods, weights, tables)
            k_cache, v_cache = _mla_cache_keys(cache_mla_ckv[:, j], cache_mla_krope[:, j], wukv)
            a_ctx = _attention(qc, [kc], [vc]).reshape(n_ctx, -1)
            a_lat = _attention(ql, [k_cache, kl], [v_cache, vl]).reshape(n_lat_seq * lat_len, -1)
            x = _out_proj_layer(x, mods, a_ctx, a_lat, mla_w_o[j].astype(BF16), lng, lnb)
        else:
            qkvr, g = _gla_project(x, mods, gla_w_in[j], gla_wa1[j], gla_wa2[j], gla_ba[j])
            zero_state = jnp.zeros((n_ctx_seq, 2, GLA_HEADS, GLA_DV, GLA_DK), F32)
            lat_state = jnp.swapaxes(state_gla[:, j], -1, -2)
            ofc, obc, s_ctx = _gla_scan(qkvr, g, 0, n_ctx_seq, ctx_len, zero_state)
            ofl, obl, _ = _gla_scan(qkvr, g, n_ctx, n_lat_seq, lat_len, lat_state)
            new_gla.append(jnp.swapaxes(s_ctx, -1, -2))
            x = _gla_out_layer(x, mods, (ofc, ofl), (obc, obl), qkvr, gla_norm[j], gla_w_o[j], lng, lnb)

        lng, lnb = ln_g[i, 1], ln_b[i, 1]
        f = i // 2
        if i % 2 == 0:
            x = _ffn_layer(x, mods, ffn_w13[f].astype(BF16), ffn_w2[f].astype(BF16), lng, lnb)
        else:
            split = n_ctx if i == DEPTH - 1 else None
            x = _moe_layer(x, mods, moe_router[f], moe_w13_b, moe_w2_b, f, lng, lnb, split)

    y_prompt = x[0].reshape(n_ctx_seq, ctx_len, D_MODEL)
    y_sample = x[1].reshape(n_lat_seq, lat_len, D_MODEL)
    return (y_prompt, y_sample, jnp.stack(new_ckv, axis=1), jnp.stack(new_krope, axis=1),
            jnp.stack(new_gla, axis=1))
```

```python
import functools

import jax
import jax.numpy as jnp
from jax import lax
from jax.experimental import pallas as pl
from jax.experimental.pallas import tpu as pltpu

F32 = jnp.float32
BF16 = jnp.bfloat16
HIGHEST = lax.Precision.HIGHEST

D_MODEL = 1024
DEPTH = 4
GRID_W = 64
DEEPNORM_ALPHA = (2.0 * DEPTH) ** 0.25
LN_EPS = 1e-5
RMS_EPS = 1e-6
GROUP_TOKENS = 4096

CHUNK_A = 128
GMLP_WIDTH = 2 * D_MODEL
GMLP_GROUPS = 8
GMLP_GROUP_DIM = GMLP_WIDTH // GMLP_GROUPS

MLA_HEADS = 8
MLA_Q_RANK = D_MODEL // 2
MLA_KV_RANK = D_MODEL // 4
MLA_D_NOPE = 128
MLA_D_ROPE = 64
MLA_D_V = 128
MLA_D_QK = MLA_D_NOPE + MLA_D_ROPE
MLA_D_VP = 2 * MLA_D_V
LOG2_E = 1.4426950408889634
ROPE_BASE = 10000.0

GLA_HEADS = 4
GLA_DK = 128
GLA_DV = 256
GLA_DECAY_RANK = 16
GLA_GATE_NORM = 16.0
GLA_CHUNK = 64
GLA_DQK = GLA_HEADS * GLA_DK
GLA_DVT = GLA_HEADS * GLA_DV
GLA_QKVR_W = 2 * GLA_DQK + 2 * GLA_DVT

D_FF = 2816
N_EXPERTS = 8
TOP_K = 2
D_FF_EXPERT = 3584

VMEM_LIMIT_BYTES = 56 * 1024 * 1024
ADA_ROWS = 16
ADA_TN = 1536
TM_GMLP = 512
GMLP_COL_CHUNK = 512
TM_FFN = 512
TM_PROJ = 512
TQ_ATTN = 2048
TQ_SUB_ATTN = 256
TM_OUT = 512
TB_GLA = 512
TM_ROUTE = 512
TM_EXPERT = 512
TF_EXPERT = 1792
F32_SUBLANES = 8
LOCAL_ROWS = -(-(TOP_K * TM_ROUTE + N_EXPERTS * (F32_SUBLANES - 1)) // 16) * 16


def _cparams(*sem):
    return pltpu.CompilerParams(dimension_semantics=sem, vmem_limit_bytes=VMEM_LIMIT_BYTES)


def _layer_norm(y, g, b):
    mu = jnp.mean(y, axis=-1, keepdims=True)
    d = y - mu
    var = jnp.mean(d * d, axis=-1, keepdims=True)
    return d * lax.rsqrt(var + LN_EPS) * g + b


def _rms_norm(y, g):
    return y * lax.rsqrt(jnp.mean(y * y, axis=-1, keepdims=True) + RMS_EPS) * g


def _modulate(x, mod_ref, shift_row):
    return x * (1.0 + mod_ref[shift_row + 1:shift_row + 2, :]) + mod_ref[shift_row:shift_row + 1, :]


def _post_norm(x, out, mod_ref, gate_row, lng_ref, lnb_ref):
    y = DEEPNORM_ALPHA * x + mod_ref[gate_row:gate_row + 1, :] * out
    return _layer_norm(y, lng_ref[...], lnb_ref[...])


def _dot(a, b):
    return jnp.dot(a, b, preferred_element_type=F32)


def _dot_nt(a, b):
    return lax.dot_general(a, b, (((1,), (1,)), ((), ())), preferred_element_type=F32)


def _dot_tn(a, b):
    return lax.dot_general(a, b, (((0,), (0,)), ((), ())), preferred_element_type=F32)


def _mod_spec(tm):
    return pl.BlockSpec((None, 6, D_MODEL), lambda i, *_: ((i * tm) // GROUP_TOKENS, 0, 0))


def _full_spec(shape):
    nd = len(shape)
    return pl.BlockSpec(shape, lambda *_: (0,) * nd, pipeline_mode=pl.Buffered(1))


def _ada_kernel(c_ref, w_ref, b_ref, o_ref):
    c = c_ref[...]
    s = c * jax.nn.sigmoid(c)
    o_ref[...] = jnp.dot(s, w_ref[...], precision=HIGHEST, preferred_element_type=F32) + b_ref[...]


def _adaln_all(cond, ada_w, ada_b):
    n_out = 6 * D_MODEL
    return pl.pallas_call(
        _ada_kernel,
        grid=(DEPTH, n_out // ADA_TN),
        in_specs=[
            pl.BlockSpec((ADA_ROWS, D_MODEL), lambda l, j: (0, 0)),
            pl.BlockSpec((None, D_MODEL, ADA_TN), lambda l, j: (l, 0, j)),
            pl.BlockSpec((None, 1, ADA_TN), lambda l, j: (l, 0, j)),
        ],
        out_specs=pl.BlockSpec((None, ADA_ROWS, ADA_TN), lambda l, j: (l, 0, j)),
        out_shape=jax.ShapeDtypeStruct((DEPTH, ADA_ROWS, n_out), F32),
        compiler_params=_cparams("parallel", "parallel"),
        name="adaln",
    )(cond, ada_w, ada_b.reshape(DEPTH, 1, n_out))


def _gmlp_kernel(*refs, n_ctx_tiles):
    if n_ctx_tiles is None:
        x = refs[0][...]
        refs = refs[1:]
    else:
        x = jnp.where(pl.program_id(0) < n_ctx_tiles, refs[0][...], refs[1][...])
        refs = refs[2:]
    (mod_ref, win_ref, bin_ref, vg_ref, vb_ref, ws_ref, bs_ref, wout_ref, lng_ref, lnb_ref,
     o_ref, u_ref, v_ref, vn_ref, p_ref) = refs
    hb = _modulate(x, mod_ref, 0).astype(BF16)
    n_chunks = GMLP_WIDTH // GMLP_COL_CHUNK
    total = jnp.zeros((x.shape[0], 1), F32)
    total_sq = jnp.zeros((x.shape[0], 1), F32)
    for j in range(n_chunks):
        cols = slice(j * GMLP_COL_CHUNK, (j + 1) * GMLP_COL_CHUNK)
        wcols = slice(GMLP_WIDTH + j * GMLP_COL_CHUNK, GMLP_WIDTH + (j + 1) * GMLP_COL_CHUNK)
        v = jax.nn.gelu(_dot(hb, win_ref[:, wcols]) + bin_ref[:, wcols])
        v_ref[:, cols] = v
        total = total + jnp.sum(v, axis=-1, keepdims=True)
        total_sq = total_sq + jnp.sum(v * v, axis=-1, keepdims=True)
    mean = total * (1.0 / GMLP_WIDTH)
    rstd = lax.rsqrt(total_sq * (1.0 / GMLP_WIDTH) - mean * mean + LN_EPS)
    for j in range(n_chunks):
        cols = slice(j * GMLP_COL_CHUNK, (j + 1) * GMLP_COL_CHUNK)
        u_ref[:, cols] = jax.nn.gelu(_dot(hb, win_ref[:, cols]) + bin_ref[:, cols])
        vn_ref[:, cols] = ((v_ref[:, cols] - mean) * rstd * vg_ref[:, cols] + vb_ref[:, cols]).astype(BF16)
    for c in range(x.shape[0] // CHUNK_A):
        rows = slice(c * CHUNK_A, (c + 1) * CHUNK_A)
        for g in range(GMLP_GROUPS):
            cols = slice(g * GMLP_GROUP_DIM, (g + 1) * GMLP_GROUP_DIM)
            mixed = _dot(ws_ref[g], vn_ref[rows, cols]) + bs_ref[g]
            p_ref[rows, cols] = (u_ref[rows, cols] * mixed).astype(BF16)
    out = _dot(p_ref[...], wout_ref[...])
    o_ref[...] = _post_norm(x, out, mod_ref, 2, lng_ref, lnb_ref)


def _gmlp_layer(x, mods, w_in, b_in, v_g, v_b, w_s, b_s, w_out, ln_g, ln_b):
    tm = TM_GMLP
    row = pl.BlockSpec((tm, D_MODEL), lambda i: (i, 0))
    if isinstance(x, tuple):
        n_ctx_tiles = x[0].shape[0] // tm
        n = x[0].shape[0] + x[1].shape[0]
        x_specs = _ctx_lat_specs(tm, D_MODEL, n_ctx_tiles)
    else:
        n_ctx_tiles = None
        n = x.shape[0]
        x, x_specs = (x,), [row]
    return pl.pallas_call(
        functools.partial(_gmlp_kernel, n_ctx_tiles=n_ctx_tiles),
        grid=(n // tm,),
        in_specs=x_specs + [
            _mod_spec(tm),
            _full_spec((D_MODEL, 2 * GMLP_WIDTH)), _full_spec((1, 2 * GMLP_WIDTH)),
            _full_spec((1, GMLP_WIDTH)), _full_spec((1, GMLP_WIDTH)),
            _full_spec((GMLP_GROUPS, CHUNK_A, CHUNK_A)), _full_spec((GMLP_GROUPS, CHUNK_A, 1)),
            _full_spec((GMLP_WIDTH, D_MODEL)),
            _full_spec((1, D_MODEL)), _full_spec((1, D_MODEL)),
        ],
        out_specs=row,
        out_shape=jax.ShapeDtypeStruct((n, D_MODEL), F32),
        scratch_shapes=[pltpu.VMEM((tm, GMLP_WIDTH), F32), pltpu.VMEM((tm, GMLP_WIDTH), F32),
                        pltpu.VMEM((tm, GMLP_WIDTH), BF16), pltpu.VMEM((tm, GMLP_WIDTH), BF16)],
        compiler_params=_cparams("parallel"),
        name="gmlp",
    )(*x, mods, w_in.astype(BF16), b_in.reshape(1, -1), v_g.reshape(1, -1), v_b.reshape(1, -1),
      w_s.astype(BF16), b_s.reshape(GMLP_GROUPS, CHUNK_A, 1), w_out.astype(BF16),
      ln_g.reshape(1, -1), ln_b.reshape(1, -1))


def _ffn_kernel(x_ref, mod_ref, w13_ref, w2_ref, lng_ref, lnb_ref, o_ref):
    x = x_ref[...]
    hb = _modulate(x, mod_ref, 3).astype(BF16)
    a = _dot(hb, w13_ref[:, :D_FF])
    b = _dot(hb, w13_ref[:, D_FF:])
    out = _dot((a * jax.nn.sigmoid(a) * b).astype(BF16), w2_ref[...])
    o_ref[...] = _post_norm(x, out, mod_ref, 5, lng_ref, lnb_ref)


def _ffn_layer(x, mods, w13, w2, ln_g, ln_b):
    n = x.shape[0]
    tm = TM_FFN
    row = pl.BlockSpec((tm, D_MODEL), lambda i: (i, 0))
    return pl.pallas_call(
        _ffn_kernel,
        grid=(n // tm,),
        in_specs=[row, _mod_spec(tm), _full_spec(w13.shape), _full_spec(w2.shape),
                  _full_spec((1, D_MODEL)), _full_spec((1, D_MODEL))],
        out_specs=row,
        out_shape=jax.ShapeDtypeStruct((n, D_MODEL), F32),
        compiler_params=_cparams("parallel"),
        name="ffn",
    )(x, mods, w13, w2, ln_g.reshape(1, -1), ln_b.reshape(1, -1))


def _rope_swap_perm():
    j = jnp.arange(MLA_D_ROPE)
    half = MLA_D_ROPE // 4
    return jnp.where((j % (2 * half)) < half, j + half, j - half)


def _rope_tables(n_tokens):
    n_freq = MLA_D_ROPE // 4
    t = jnp.arange(n_tokens)
    pos_r = (t // GRID_W).astype(F32)[:, None]
    pos_c = (t % GRID_W).astype(F32)[:, None]
    inv = ROPE_BASE ** (-jnp.arange(n_freq, dtype=F32) / n_freq)
    ang_r, ang_c = pos_r * inv, pos_c * inv
    cos = jnp.concatenate([jnp.cos(ang_r), jnp.cos(ang_r), jnp.cos(ang_c), jnp.cos(ang_c)], axis=-1)
    sin = jnp.concatenate([-jnp.sin(ang_r), jnp.sin(ang_r), -jnp.sin(ang_c), jnp.sin(ang_c)], axis=-1)
    return cos, sin


def _write_heads(q_ref, k_ref, v_ref, q_nope, q_rope, kv, k_rope):
    scale = MLA_D_QK ** -0.5 * LOG2_E
    k_rope = k_rope.astype(BF16)
    ones = jnp.ones((kv.shape[0], MLA_D_VP - MLA_D_V), BF16)
    for h in range(MLA_HEADS):
        nope = slice(h * MLA_D_NOPE, (h + 1) * MLA_D_NOPE)
        if q_ref is not None:
            q_ref[h, :, :MLA_D_NOPE] = (q_nope[:, nope] * scale).astype(BF16)
            q_ref[h, :, MLA_D_NOPE:] = (q_rope[:, h * MLA_D_ROPE:(h + 1) * MLA_D_ROPE] * scale).astype(BF16)
        k_ref[h, :, :MLA_D_NOPE] = kv[:, nope].astype(BF16)
        k_ref[h, :, MLA_D_NOPE:] = k_rope
        vcol = MLA_HEADS * MLA_D_NOPE + h * MLA_D_V
        v_ref[h, :, :MLA_D_V] = kv[:, vcol:vcol + MLA_D_V].astype(BF16)
        v_ref[h, :, MLA_D_V:] = ones


def _mla_proj_kernel(*refs, rope):
    if rope:
        (x_ref, mod_ref, wd_ref, qg_ref, kvg_ref, wuq_ref, wukv_ref, cq_ref, sq_ref, ck_ref, sk_ref,
         q_ref, k_ref, v_ref, ckv_ref, kr_ref) = refs
    else:
        (x_ref, mod_ref, wd_ref, qg_ref, kvg_ref, wuq_ref, wukv_ref,
         q_ref, k_ref, v_ref, ckv_ref, kr_ref) = refs
    hb = _modulate(x_ref[...], mod_ref, 0).astype(BF16)
    c = _dot(hb, wd_ref[...])
    r0 = MLA_Q_RANK + MLA_KV_RANK
    c_q = _rms_norm(c[:, :MLA_Q_RANK], qg_ref[...]).astype(BF16)
    c_kv = _rms_norm(c[:, MLA_Q_RANK:r0], kvg_ref[...])
    k_rope = c[:, r0:r0 + MLA_D_ROPE]
    ckv_ref[...] = c_kv
    kr_ref[...] = k_rope
    q = _dot(c_q, wuq_ref[...])
    n_nope = MLA_HEADS * MLA_D_NOPE
    n_rope = MLA_HEADS * MLA_D_ROPE
    q_rope = q[:, n_nope:n_nope + n_rope]
    if rope:
        q_rope = q_rope * cq_ref[...] + q[:, n_nope + n_rope:] * sq_ref[...]
        k_rope = k_rope * ck_ref[...] + c[:, r0 + MLA_D_ROPE:] * sk_ref[...]
    kv = _dot(c_kv.astype(BF16), wukv_ref[...])
    _write_heads(q_ref, k_ref, v_ref, q[:, :n_nope], q_rope, kv, k_rope)


def _mla_project(x, row_offset, n_seq, seq_len, mods, weights, tables):
    wd, qg, kvg, wuq, wukv = weights
    tm = min(TM_PROJ, seq_len)
    per_seq = seq_len // tm
    n = n_seq * seq_len
    off = row_offset // tm
    rope = tables is not None
    in_specs = [
        pl.BlockSpec((tm, D_MODEL), lambda i: (off + i, 0)),
        pl.BlockSpec((None, 6, D_MODEL), lambda i: ((row_offset + i * tm) // GROUP_TOKENS, 0, 0)),
        _full_spec(wd.shape), _full_spec(qg.shape), _full_spec(kvg.shape),
        _full_spec(wuq.shape), _full_spec(wukv.shape),
    ]
    args = [x, mods, wd, qg, kvg, wuq, wukv]
    if rope:
        pos = lambda i: (i % per_seq, 0)
        in_specs += [pl.BlockSpec((tm, MLA_HEADS * MLA_D_ROPE), pos)] * 2 + [pl.BlockSpec((tm, MLA_D_ROPE), pos)] * 2
        args += list(tables)
    head_spec = lambda d: pl.BlockSpec((None, MLA_HEADS, tm, d), lambda i: (i // per_seq, 0, i % per_seq, 0))
    head_shape = lambda d: jax.ShapeDtypeStruct((n_seq, MLA_HEADS, seq_len, d), BF16)
    return pl.pallas_call(
        functools.partial(_mla_proj_kernel, rope=rope),
        grid=(n // tm,),
        in_specs=in_specs,
        out_specs=[head_spec(MLA_D_QK), head_spec(MLA_D_QK), head_spec(MLA_D_VP),
                   pl.BlockSpec((tm, MLA_KV_RANK), lambda i: (i, 0)),
                   pl.BlockSpec((tm, MLA_D_ROPE), lambda i: (i, 0))],
        out_shape=[head_shape(MLA_D_QK), head_shape(MLA_D_QK), head_shape(MLA_D_VP),
                   jax.ShapeDtypeStruct((n, MLA_KV_RANK), F32),
                   jax.ShapeDtypeStruct((n, MLA_D_ROPE), F32)],
        compiler_params=_cparams("parallel"),
        name="mla_proj_rope" if rope else "mla_proj",
    )(*args)


def _mla_cache_kernel(ckv_ref, kr_ref, wukv_ref, k_ref, v_ref):
    kv = _dot(ckv_ref[...].astype(BF16), wukv_ref[...])
    _write_heads(None, k_ref, v_ref, None, None, kv, kr_ref[...])


def _mla_cache_keys(ckv, krope, wukv):
    b, l, _ = ckv.shape
    head_spec = lambda d: pl.BlockSpec((None, MLA_HEADS, l, d), lambda i: (i, 0, 0, 0))
    return pl.pallas_call(
        _mla_cache_kernel,
        grid=(b,),
        in_specs=[pl.BlockSpec((None, l, MLA_KV_RANK), lambda i: (i, 0, 0)),
                  pl.BlockSpec((None, l, MLA_D_ROPE), lambda i: (i, 0, 0)),
                  _full_spec(wukv.shape)],
        out_specs=[head_spec(MLA_D_QK), head_spec(MLA_D_VP)],
        out_shape=[jax.ShapeDtypeStruct((b, MLA_HEADS, l, MLA_D_QK), BF16),
                   jax.ShapeDtypeStruct((b, MLA_HEADS, l, MLA_D_VP), BF16)],
        compiler_params=_cparams("parallel"),
        name="mla_cache_keys",
    )(ckv, krope, wukv)


def _attn_kernel(*refs, n_seg):
    q_ref = refs[0]
    k_refs = refs[1:1 + n_seg]
    v_refs = refs[1 + n_seg:1 + 2 * n_seg]
    o_ref = refs[1 + 2 * n_seg]
    s_ref, p_ref = refs[2 + 2 * n_seg:]
    n_buf, sub, _ = s_ref.shape
    n_sub = q_ref.shape[0] // sub
    seg_cols = []
    col = 0
    for k_ref in k_refs:
        seg_cols.append(slice(col, col + k_ref.shape[0]))
        col += k_ref.shape[0]

    def scores(i):
        q = q_ref[i * sub:(i + 1) * sub, :]
        for k_ref, cols in zip(k_refs, seg_cols):
            s_ref[i % n_buf, :, cols] = _dot_nt(q, k_ref[...])

    def finish(i):
        s = s_ref[i % n_buf]
        m = jnp.max(s, axis=-1, keepdims=True)
        p_ref[i % n_buf] = jnp.exp2((s - m).astype(BF16))
        acc = sum(_dot(p_ref[i % n_buf, :, cols], v_ref[...]) for v_ref, cols in zip(v_refs, seg_cols))
        o_ref[i * sub:(i + 1) * sub, :] = (acc[:, :MLA_D_V] / acc[:, MLA_D_V:MLA_D_V + 1]).astype(BF16)

    scores(0)
    for i in range(n_sub):
        if i + 1 < n_sub:
            scores(i + 1)
        finish(i)


def _attention(q, ks, vs):
    b, h, t, _ = q.shape
    tq = min(TQ_ATTN, t)
    sub = min(TQ_SUB_ATTN, tq)
    n_buf = min(2, tq // sub)
    n_keys = sum(a.shape[2] for a in ks)
    seg = lambda a: pl.BlockSpec((None, None) + a.shape[2:], lambda bi, hi, qi: (bi, hi, 0, 0))
    return pl.pallas_call(
        functools.partial(_attn_kernel, n_seg=len(ks)),
        grid=(b, h, t // tq),
        in_specs=[pl.BlockSpec((None, None, tq, MLA_D_QK), lambda bi, hi, qi: (bi, hi, qi, 0))]
        + [seg(a) for a in ks] + [seg(a) for a in vs],
        out_specs=pl.BlockSpec((None, tq, MLA_D_V), lambda bi, hi, qi: (bi, qi, hi)),
        out_shape=jax.ShapeDtypeStruct((b, t, h * MLA_D_V), BF16),
        scratch_shapes=[pltpu.VMEM((n_buf, sub, n_keys), F32), pltpu.VMEM((n_buf, sub, n_keys), BF16)],
        compiler_params=_cparams("parallel", "parallel", "arbitrary"),
        name="mla_attention",
    )(q, *ks, *vs)


def _ctx_lat_specs(tm, width, n_ctx_tiles, col=0):
    return [pl.BlockSpec((tm, width), lambda i: (jnp.minimum(i, n_ctx_tiles - 1), col)),
            pl.BlockSpec((tm, width), lambda i: (jnp.maximum(i - n_ctx_tiles, 0), col))]


def _out_proj_kernel(x_ref, mod_ref, ac_ref, al_ref, w_ref, lng_ref, lnb_ref, o_ref, *, n_ctx_tiles):
    def finish(a_ref):
        out = _dot(a_ref[...], w_ref[...])
        o_ref[...] = _post_norm(x_ref[...], out, mod_ref, 2, lng_ref, lnb_ref)

    is_ctx = pl.program_id(0) < n_ctx_tiles
    pl.when(is_ctx)(lambda: finish(ac_ref))
    pl.when(jnp.logical_not(is_ctx))(lambda: finish(al_ref))


def _out_proj_layer(x, mods, a_ctx, a_lat, w_o, ln_g, ln_b):
    n = x.shape[0]
    tm = TM_OUT
    n_ctx_tiles = a_ctx.shape[0] // tm
    row = pl.BlockSpec((tm, D_MODEL), lambda i: (i, 0))
    return pl.pallas_call(
        functools.partial(_out_proj_kernel, n_ctx_tiles=n_ctx_tiles),
        grid=(n // tm,),
        in_specs=[row, _mod_spec(tm)] + _ctx_lat_specs(tm, a_ctx.shape[1], n_ctx_tiles)
        + [_full_spec(w_o.shape), _full_spec((1, D_MODEL)), _full_spec((1, D_MODEL))],
        out_specs=row,
        out_shape=jax.ShapeDtypeStruct((n, D_MODEL), F32),
        compiler_params=_cparams("parallel"),
        name="mla_out_proj",
    )(x, mods, a_ctx, a_lat, w_o, ln_g.reshape(1, -1), ln_b.reshape(1, -1))


def _gla_proj_kernel(x_ref, mod_ref, win_ref, wa2_ref, ba_ref, qkvr_ref, g_ref):
    h = _modulate(x_ref[...], mod_ref, 0).astype(BF16)
    proj = _dot(h, win_ref[...])
    qkvr_ref[:, :GLA_DQK] = (proj[:, :GLA_DQK] * (GLA_DK ** -0.5)).astype(BF16)
    qkvr_ref[:, GLA_DQK:] = proj[:, GLA_DQK:GLA_QKVR_W].astype(BF16)
    logit = _dot(proj[:, GLA_QKVR_W:].astype(BF16), wa2_ref[...]) + ba_ref[...]
    log_sig = jnp.minimum(logit, 0.0) - jnp.log(1.0 + jnp.exp(-jnp.abs(logit)))
    g_ref[...] = log_sig / GLA_GATE_NORM


def _gla_project(x, mods, w_in, wa1, wa2, ba):
    n = x.shape[0]
    tm = TM_PROJ
    lane = 128
    low_w = 2 * GLA_DECAY_RANK
    w_ext = jnp.concatenate([w_in, wa1[0], wa1[1], jnp.zeros((D_MODEL, lane - low_w), F32)], axis=1)
    zeros = jnp.zeros((GLA_DECAY_RANK, GLA_DQK), F32)
    wa2_blk = jnp.concatenate([jnp.concatenate([wa2[0], zeros], axis=1),
                               jnp.concatenate([zeros, wa2[1]], axis=1),
                               jnp.zeros((lane - low_w, 2 * GLA_DQK), F32)], axis=0)
    return pl.pallas_call(
        _gla_proj_kernel,
        grid=(n // tm,),
        in_specs=[pl.BlockSpec((tm, D_MODEL), lambda i: (i, 0)), _mod_spec(tm),
                  _full_spec(w_ext.shape), _full_spec(wa2_blk.shape), _full_spec((1, 2 * GLA_DQK))],
        out_specs=[pl.BlockSpec((tm, GLA_QKVR_W), lambda i: (i, 0)),
                   pl.BlockSpec((tm, 2 * GLA_DQK), lambda i: (i, 0))],
        out_shape=[jax.ShapeDtypeStruct((n, GLA_QKVR_W), BF16), jax.ShapeDtypeStruct((n, 2 * GLA_DQK), F32)],
        compiler_params=_cparams("parallel"),
        name="gla_proj",
    )(x, mods, w_ext.astype(BF16), wa2_blk.astype(BF16), ba.reshape(1, -1))


def _chunk_cumsum(g, reverse):
    n = g.shape[0]
    pos = lax.broadcasted_iota(jnp.int32, g.shape, 0) % GLA_CHUNK
    b = g
    shift = 1
    while shift < GLA_CHUNK:
        if reverse:
            moved = pltpu.roll(b, n - shift, 0)
            keep = pos < GLA_CHUNK - shift
        else:
            moved = pltpu.roll(b, shift, 0)
            keep = pos >= shift
        b = b + jnp.where(keep, moved, 0.0)
        shift *= 2
    return b


def _gla_scan_kernel(qf_ref, kf_ref, vf_ref, gf_ref, qb_ref, kb_ref, vb_ref, gb_ref, s0_ref,
                     of_ref, ob_ref, sfin_ref, st_ref):
    i = pl.program_id(2)

    @pl.when(i == 0)
    def _():
        st_ref[...] = s0_ref[...]

    n_chunks = qf_ref.shape[0] // GLA_CHUNK
    ii = lax.broadcasted_iota(jnp.int32, (GLA_CHUNK, GLA_CHUNK), 0)
    jj = lax.broadcasted_iota(jnp.int32, (GLA_CHUNK, GLA_CHUNK), 1)
    dirs = ((0, False, qf_ref, kf_ref, vf_ref, of_ref, gf_ref, jj <= ii),
            (1, True, qb_ref, kb_ref, vb_ref, ob_ref, gb_ref, jj >= ii))
    for d, reverse, q_ref, k_ref, v_ref, o_ref, g_ref, mask in dirs:
        b_all = _chunk_cumsum(g_ref[...], reverse)
        qes, intra, updates, decays = [], [], [], []
        for c in range(n_chunks):
            rows = slice(c * GLA_CHUNK, (c + 1) * GLA_CHUNK)
            b = b_all[rows]
            b_tot = b[:1] if reverse else b[GLA_CHUNK - 1:]
            q = q_ref[rows, :].astype(F32)
            k = k_ref[rows, :].astype(F32)
            v = v_ref[rows, :]
            qe = (q * jnp.exp(b)).astype(BF16)
            ke = (k * jnp.exp(-b)).astype(BF16)
            kd = (k * jnp.exp(b_tot - b)).astype(BF16)
            a = jnp.where(mask, _dot_nt(qe, ke), 0.0).astype(BF16)
            qes.append(qe)
            intra.append(_dot(a, v))
            updates.append(_dot_tn(v, kd))
            decays.append(jnp.exp(b_tot))
        st = st_ref[d]
        entering = [None] * n_chunks
        for c in (reversed(range(n_chunks)) if reverse else range(n_chunks)):
            entering[c] = st.astype(BF16)
            st = st * decays[c] + updates[c]
        st_ref[d] = st
        for c in range(n_chunks):
            rows = slice(c * GLA_CHUNK, (c + 1) * GLA_CHUNK)
            o_ref[rows, :] = (intra[c] + _dot_nt(qes[c], entering[c])).astype(o_ref.dtype)

    @pl.when(i == pl.num_programs(2) - 1)
    def _():
        sfin_ref[...] = st_ref[...]


def _gla_scan(qkvr, g, row_offset, n_seq, seq_len, s0_t):
    tb = min(TB_GLA, seq_len)
    nb = seq_len // tb
    off = row_offset // tb
    v_col0 = 2 * GLA_DQK // GLA_DV

    def in_specs(reverse):
        blk = (lambda s, i: off + s * nb + nb - 1 - i) if reverse else (lambda s, i: off + s * nb + i)
        g_col = GLA_HEADS if reverse else 0
        return [pl.BlockSpec((tb, GLA_DK), lambda s, h, i: (blk(s, i), h)),
                pl.BlockSpec((tb, GLA_DK), lambda s, h, i: (blk(s, i), GLA_HEADS + h)),
                pl.BlockSpec((tb, GLA_DV), lambda s, h, i: (blk(s, i), v_col0 + h)),
                pl.BlockSpec((tb, GLA_DK), lambda s, h, i: (blk(s, i), g_col + h))]

    state_spec = pl.BlockSpec((None, 2, None, GLA_DV, GLA_DK), lambda s, h, i: (s, 0, h, 0, 0))
    o_shape = jax.ShapeDtypeStruct((n_seq * seq_len, GLA_DVT), BF16)
    return pl.pallas_call(
        _gla_scan_kernel,
        grid=(n_seq, GLA_HEADS, nb),
        in_specs=in_specs(False) + in_specs(True) + [state_spec],
        out_specs=[pl.BlockSpec((tb, GLA_DV), lambda s, h, i: (s * nb + i, h)),
                   pl.BlockSpec((tb, GLA_DV), lambda s, h, i: (s * nb + nb - 1 - i, h)),
                   state_spec],
        out_shape=[o_shape, o_shape, jax.ShapeDtypeStruct((n_seq, 2, GLA_HEADS, GLA_DV, GLA_DK), F32)],
        scratch_shapes=[pltpu.VMEM((2, GLA_DV, GLA_DK), F32)],
        compiler_params=_cparams("parallel", "parallel", "arbitrary"),
        name="gla_scan",
    )(qkvr, qkvr, qkvr, g, qkvr, qkvr, qkvr, g, s0_t)


def _gla_out_kernel(x_ref, mod_ref, ofc_ref, ofl_ref, obc_ref, obl_ref, r_ref, ng_ref, wo_ref, lng_ref, lnb_ref,
                    o_ref, p_ref, *, n_ctx_tiles):
    def finish(of_ref, ob_ref):
        for h in range(GLA_HEADS):
            cols = slice(h * GLA_DV, (h + 1) * GLA_DV)
            o = _rms_norm(of_ref[:, cols].astype(F32) + ob_ref[:, cols].astype(F32), ng_ref[...])
            r = r_ref[:, cols].astype(F32)
            p_ref[:, cols] = (o * (r * jax.nn.sigmoid(r))).astype(BF16)
        out = _dot(p_ref[...], wo_ref[...])
        o_ref[...] = _post_norm(x_ref[...], out, mod_ref, 2, lng_ref, lnb_ref)

    is_ctx = pl.program_id(0) < n_ctx_tiles
    pl.when(is_ctx)(lambda: finish(ofc_ref, obc_ref))
    pl.when(jnp.logical_not(is_ctx))(lambda: finish(ofl_ref, obl_ref))


def _gla_out_layer(x, mods, o_fwd, o_bwd, proj, norm_g, w_o, ln_g, ln_b):
    n = x.shape[0]
    tm = TM_OUT
    n_ctx_tiles = o_fwd[0].shape[0] // tm
    row = pl.BlockSpec((tm, D_MODEL), lambda i: (i, 0))
    r_block = (2 * GLA_DQK + GLA_DVT) // GLA_DVT
    return pl.pallas_call(
        functools.partial(_gla_out_kernel, n_ctx_tiles=n_ctx_tiles),
        grid=(n // tm,),
        in_specs=[row, _mod_spec(tm)] + _ctx_lat_specs(tm, GLA_DVT, n_ctx_tiles) * 2
        + [pl.BlockSpec((tm, GLA_DVT), lambda i: (i, r_block)),
           _full_spec((1, GLA_DV)), _full_spec(w_o.shape),
           _full_spec((1, D_MODEL)), _full_spec((1, D_MODEL))],
        out_specs=row,
        out_shape=jax.ShapeDtypeStruct((n, D_MODEL), F32),
        scratch_shapes=[pltpu.VMEM((tm, GLA_DVT), BF16)],
        compiler_params=_cparams("parallel"),
        name="gla_out_proj",
    )(x, mods, *o_fwd, *o_bwd, proj, norm_g.reshape(1, -1), w_o.astype(BF16),
      ln_g.reshape(1, -1), ln_b.reshape(1, -1))


def _router_kernel(x_ref, mod_ref, wr_ref, idx_ref, wt_ref):
    h = _modulate(x_ref[...], mod_ref, 3)
    logits = lax.dot_general(wr_ref[...], h, (((1,), (1,)), ((), ())), precision=HIGHEST,
                             preferred_element_type=F32)
    e = lax.broadcasted_iota(jnp.int32, logits.shape, 0)
    m1 = jnp.max(logits, axis=0, keepdims=True)
    i1 = jnp.min(jnp.where(logits == m1, e, N_EXPERTS), axis=0, keepdims=True)
    rest = jnp.where(e == i1, -jnp.inf, logits)
    m2 = jnp.max(rest, axis=0, keepdims=True)
    i2 = jnp.min(jnp.where(rest == m2, e, N_EXPERTS), axis=0, keepdims=True)
    z = jnp.exp(m2 - m1)
    idx_ref[0:1, :] = i1
    idx_ref[1:2, :] = i2
    wt_ref[0:1, :] = 1.0 / (1.0 + z)
    wt_ref[1:2, :] = z / (1.0 + z)


def _route(x, mods, w_router):
    n = x.shape[0]
    tm = TM_ROUTE
    row = pl.BlockSpec((tm, D_MODEL), lambda i: (i, 0))
    pair = pl.BlockSpec((TOP_K, tm), lambda i: (0, i))
    return pl.pallas_call(
        _router_kernel,
        grid=(n // tm,),
        in_specs=[row, _mod_spec(tm), _full_spec((N_EXPERTS, D_MODEL))],
        out_specs=[pair, pair],
        out_shape=[jax.ShapeDtypeStruct((TOP_K, n), jnp.int32),
                   jax.ShapeDtypeStruct((TOP_K, n), F32)],
        compiler_params=_cparams("parallel"),
        name="moe_router",
    )(x, mods, w_router.T)


def _dispatch_plan(expert_idx, n_tiles):
    n = expert_idx.shape[1]
    n_tok_tiles = n // TM_ROUTE
    per_tile = expert_idx.T.reshape(n_tok_tiles, TM_ROUTE * TOP_K)
    onehot = (per_tile[..., None] == jnp.arange(N_EXPERTS)).astype(jnp.int32)
    rank = jnp.sum((jnp.cumsum(onehot, axis=1) - onehot) * onehot, axis=-1)
    counts = jnp.sum(onehot, axis=1)
    seg_rows = (counts + F32_SUBLANES - 1) // F32_SUBLANES * F32_SUBLANES
    seg_local = jnp.cumsum(seg_rows, axis=1) - seg_rows
    pos = jnp.sum(onehot * seg_local[:, None, :], axis=-1) + rank
    tiles = (jnp.sum(seg_rows, axis=0) + TM_EXPERT - 1) // TM_EXPERT
    tile_end = jnp.cumsum(tiles)
    tile_start = tile_end - tiles
    seg_global = tile_start[None, :] * TM_EXPERT + jnp.cumsum(seg_rows, axis=0) - seg_rows
    tile_id = jnp.arange(n_tiles)
    tile_active = (tile_id < tile_end[-1]).astype(jnp.int32)
    last_tile = jnp.minimum(tile_id, tile_end[-1] - 1)
    tile_expert = jnp.sum((tile_end[None, :] <= last_tile[:, None]).astype(jnp.int32), axis=1)
    idle = tile_end[-1] + jnp.arange(n_tiles - (TOP_K * n) // TM_EXPERT)
    fill_tiles = jnp.concatenate([jnp.where(tiles > 0, tile_end - 1, -1), jnp.where(idle < n_tiles, idle, -1)])
    i32 = lambda a: a.astype(jnp.int32)
    return (i32(pos.reshape(n, TOP_K)), i32(seg_rows.reshape(-1)), i32(seg_local.reshape(-1)),
            i32(seg_global.reshape(-1)), i32(tile_expert), tile_active, i32(fill_tiles))


def _segment_copies(rows_ref, local_ref, global_ref, tile, make_copy, wait):
    for e in range(N_EXPERTS):
        seg = tile * N_EXPERTS + e
        rows, local, glob = rows_ref[seg], local_ref[seg], global_ref[seg]
        size = TM_ROUTE
        while size >= F32_SUBLANES:
            done = rows & (-2 * size)

            def piece(size=size, done=done):
                copy = make_copy(pl.multiple_of(local + done, F32_SUBLANES),
                                 pl.multiple_of(glob + done, F32_SUBLANES), size)
                copy.wait() if wait else copy.start()

            pl.when((rows & size) != 0)(piece)
            size //= 2


def _dispatch_kernel(rows_ref, local_ref, global_ref, fill_ref, x_ref, mod_ref, pos_ref, xs_hbm,
                     sbuf_ref, zbuf_ref, sems, zsem):
    i, n_steps = pl.program_id(0), pl.num_programs(0)
    b = i % 2

    def segment_copies(tile, buf, wait):
        def make_copy(local, glob, size):
            return pltpu.make_async_copy(sbuf_ref.at[buf, pl.ds(local, size), :],
                                         xs_hbm.at[pl.ds(glob, size), :], sems.at[buf])
        _segment_copies(rows_ref, local_ref, global_ref, tile, make_copy, wait)

    @pl.when(i == 0)
    def _():
        zbuf_ref[...] = jnp.zeros_like(zbuf_ref)
        rows = zbuf_ref.shape[0]

        def fill(k):
            start = pl.multiple_of(fill_ref[k] * rows, rows)
            return pltpu.make_async_copy(zbuf_ref, xs_hbm.at[pl.ds(start, rows), :], zsem)

        for k in range(fill_ref.shape[0]):
            pl.when(fill_ref[k] >= 0)(lambda k=k: fill(k).start())
        for k in range(fill_ref.shape[0]):
            pl.when(fill_ref[k] >= 0)(lambda k=k: fill(k).wait())

    @pl.when(i >= 2)
    def _():
        segment_copies(i - 2, b, wait=True)

    hb = _modulate(x_ref[...], mod_ref, 3).astype(BF16)
    r = lax.broadcasted_iota(jnp.int32, (sbuf_ref.shape[1], hb.shape[0]), 0)
    onehot = jnp.logical_or(r == pos_ref[0:1, :], r == pos_ref[1:2, :])
    sbuf_ref[b] = _dot(onehot.astype(BF16), hb)
    segment_copies(i, b, wait=False)

    @pl.when(i == n_steps - 1)
    def _():
        segment_copies(i, b, wait=True)

    @pl.when(jnp.logical_and(i == n_steps - 1, i >= 1))
    def _():
        segment_copies(i - 1, 1 - b, wait=True)


def _dispatch(x, mods, pos_t, segments, fill_tiles, n_slots):
    n = x.shape[0]
    tm = TM_ROUTE
    return pl.pallas_call(
        _dispatch_kernel,
        grid_spec=pltpu.PrefetchScalarGridSpec(
            num_scalar_prefetch=4,
            grid=(n // tm,),
            in_specs=[pl.BlockSpec((tm, D_MODEL), lambda i, *_: (i, 0)),
                      pl.BlockSpec((None, 6, D_MODEL), lambda i, *_: ((i * tm) // GROUP_TOKENS, 0, 0)),
                      pl.BlockSpec((TOP_K, tm), lambda i, *_: (0, i))],
            out_specs=pl.BlockSpec(memory_space=pl.ANY),
            scratch_shapes=[pltpu.VMEM((2, LOCAL_ROWS, D_MODEL), F32),
                            pltpu.VMEM((TM_EXPERT, D_MODEL), F32),
                            pltpu.SemaphoreType.DMA((2,)), pltpu.SemaphoreType.DMA(())],
        ),
        out_shape=jax.ShapeDtypeStruct((n_slots, D_MODEL), F32),
        compiler_params=_cparams("arbitrary"),
        name="moe_dispatch",
    )(*segments, fill_tiles, x, mods, pos_t)


def _expert_kernel(te_ref, ta_ref, x_ref, w1_ref, w3_ref, w2_ref, o_ref, xb_ref, acc_ref):
    i, j = pl.program_id(0), pl.program_id(1)
    last = pl.num_programs(1) - 1
    active = ta_ref[i] == 1

    @pl.when(jnp.logical_and(j == 0, active))
    def _():
        xb_ref[...] = x_ref[...].astype(BF16)

    @pl.when(active)
    def _():
        x = xb_ref[...]
        a = _dot(x, w1_ref[...])
        b = _dot(x, w3_ref[...])
        part = _dot((a * jax.nn.sigmoid(a) * b).astype(BF16), w2_ref[...])

        @pl.when(j == 0)
        def _():
            acc_ref[...] = part

        @pl.when(j > 0)
        def _():
            acc_ref[...] += part

    @pl.when(jnp.logical_and(j == last, active))
    def _():
        o_ref[...] = acc_ref[...]

    @pl.when(jnp.logical_and(j == last, jnp.logical_not(active)))
    def _():
        o_ref[...] = jnp.zeros_like(o_ref)


def _expert_ffn(xs, tile_expert, tile_active, w13, w2, layer):
    tm, tf = TM_EXPERT, TF_EXPERT
    nf = D_FF_EXPERT // tf
    n_slots = xs.shape[0]
    row = pl.BlockSpec((tm, D_MODEL), lambda i, j, te, ta: (i, 0))
    return pl.pallas_call(
        _expert_kernel,
        grid_spec=pltpu.PrefetchScalarGridSpec(
            num_scalar_prefetch=2,
            grid=(n_slots // tm, nf),
            in_specs=[
                row,
                pl.BlockSpec((None, None, D_MODEL, tf), lambda i, j, te, ta: (layer, te[i], 0, j)),
                pl.BlockSpec((None, None, D_MODEL, tf), lambda i, j, te, ta: (layer, te[i], 0, j + nf)),
                pl.BlockSpec((None, None, tf, D_MODEL), lambda i, j, te, ta: (layer, te[i], j, 0)),
            ],
            out_specs=row,
            scratch_shapes=[pltpu.VMEM((tm, D_MODEL), BF16), pltpu.VMEM((tm, D_MODEL), F32)],
        ),
        out_shape=jax.ShapeDtypeStruct((n_slots, D_MODEL), F32),
        compiler_params=_cparams("parallel", "arbitrary"),
        name="moe_experts",
    )(tile_expert, tile_active, xs, w13, w13, w2)


def _combine_kernel(rows_ref, local_ref, global_ref, x_ref, mod_ref, pos_ref, wt_ref, y_hbm, lng_ref, lnb_ref,
                    *refs, n_ctx_tiles):
    ybuf_ref, sems = refs[-2:]
    o_refs = refs[:-2]
    i, n_steps = pl.program_id(0), pl.num_programs(0)

    def segment_copies(tile, buf, wait):
        def make_copy(local, glob, size):
            return pltpu.make_async_copy(y_hbm.at[pl.ds(glob, size), :],
                                         ybuf_ref.at[buf, pl.ds(local, size), :], sems.at[buf])
        _segment_copies(rows_ref, local_ref, global_ref, tile, make_copy, wait)

    @pl.when(i == 0)
    def _():
        ybuf_ref[...] = jnp.zeros_like(ybuf_ref)
        segment_copies(0, 0, wait=False)

    @pl.when(i + 1 < n_steps)
    def _():
        segment_copies(i + 1, (i + 1) % 2, wait=False)

    b = i % 2
    segment_copies(i, b, wait=True)
    yb = ybuf_ref[b].astype(BF16)
    r = lax.broadcasted_iota(jnp.int32, (x_ref.shape[0], yb.shape[0]), 1)
    y = (wt_ref[:, 0:1] * _dot((r == pos_ref[:, 0:1]).astype(BF16), yb)
         + wt_ref[:, 1:2] * _dot((r == pos_ref[:, 1:2]).astype(BF16), yb))
    out = _post_norm(x_ref[...], y, mod_ref, 5, lng_ref, lnb_ref)
    if n_ctx_tiles is None:
        o_refs[0][...] = out
    else:
        is_ctx = i < n_ctx_tiles

        @pl.when(is_ctx)
        def _():
            o_refs[0][...] = out

        @pl.when(jnp.logical_not(is_ctx))
        def _():
            o_refs[1][...] = out


def _combine(x, mods, pos, segments, weights_t, ys, ln_g, ln_b, n_ctx_split=None):
    n = x.shape[0]
    tm = TM_ROUTE
    row = pl.BlockSpec((tm, D_MODEL), lambda i, *_: (i, 0))
    if n_ctx_split is None:
        n_ctx_tiles = None
        out_specs, out_shape = row, jax.ShapeDtypeStruct((n, D_MODEL), F32)
    else:
        n_ctx_tiles = n_ctx_split // tm
        out_specs = [pl.BlockSpec((tm, D_MODEL), lambda i, *_: (jnp.minimum(i, n_ctx_tiles - 1), 0)),
                     pl.BlockSpec((tm, D_MODEL), lambda i, *_: (jnp.maximum(i - n_ctx_tiles, 0), 0))]
        out_shape = [jax.ShapeDtypeStruct((n_ctx_split, D_MODEL), F32),
                     jax.ShapeDtypeStruct((n - n_ctx_split, D_MODEL), F32)]
    return pl.pallas_call(
        functools.partial(_combine_kernel, n_ctx_tiles=n_ctx_tiles),
        grid_spec=pltpu.PrefetchScalarGridSpec(
            num_scalar_prefetch=3,
            grid=(n // tm,),
            in_specs=[row,
                      pl.BlockSpec((None, 6, D_MODEL), lambda i, *_: ((i * tm) // GROUP_TOKENS, 0, 0)),
                      pl.BlockSpec((tm, TOP_K), lambda i, *_: (i, 0)),
                      pl.BlockSpec((tm, TOP_K), lambda i, *_: (i, 0)),
                      pl.BlockSpec(memory_space=pl.ANY),
                      pl.BlockSpec((1, D_MODEL), lambda i, *_: (0, 0)),
                      pl.BlockSpec((1, D_MODEL), lambda i, *_: (0, 0))],
            out_specs=out_specs,
            scratch_shapes=[pltpu.VMEM((2, LOCAL_ROWS, D_MODEL), F32), pltpu.SemaphoreType.DMA((2,))],
        ),
        out_shape=out_shape,
        compiler_params=_cparams("arbitrary"),
        name="moe_combine",
    )(*segments, x, mods, pos, weights_t, ys, ln_g.reshape(1, -1), ln_b.reshape(1, -1))


def _moe_layer(x, mods, w_router, w13, w2, layer, ln_g, ln_b, n_ctx_split=None):
    n = x.shape[0]
    seg_pad = (n // TM_ROUTE) * N_EXPERTS * (F32_SUBLANES - 1)
    n_tiles = -(-(TOP_K * n + seg_pad) // TM_EXPERT) + N_EXPERTS
    expert_idx, weights = _route(x, mods, w_router)
    pos, seg_rows, seg_local, seg_global, tile_expert, tile_active, fill_tiles = _dispatch_plan(expert_idx, n_tiles)
    segments = (seg_rows, seg_local, seg_global)
    xs = _dispatch(x, mods, pos.T, segments, fill_tiles, n_tiles * TM_EXPERT)
    ys = _expert_ffn(xs, tile_expert, tile_active, w13, w2, layer)
    return _combine(x, mods, pos, segments, weights.T, ys, ln_g, ln_b, n_ctx_split)


def kernel(x_prompt, x_sample, cache_mla_ckv, cache_mla_krope, state_gla, c, c_ctx, ada_w, ada_b, ln_g, ln_b, gmlp_w_in, gmlp_b_in, gmlp_v_g, gmlp_v_b, gmlp_w_s, gmlp_b_s, gmlp_w_out, mla_w_dqkv, mla_q_norm, mla_kv_norm, mla_w_uq, mla_w_ukv, mla_w_o, gla_w_in, gla_wa1, gla_wa2, gla_ba, gla_norm, gla_w_o, ffn_w13, ffn_w2, moe_router, moe_w13, moe_w2):
    n_ctx_seq, ctx_len, _ = x_prompt.shape
    n_lat_seq, lat_len, _ = x_sample.shape
    n_ctx = n_ctx_seq * ctx_len
    assert n_ctx == GROUP_TOKENS and lat_len == GROUP_TOKENS
    x = (x_prompt.reshape(n_ctx, D_MODEL), x_sample.reshape(-1, D_MODEL))

    cond = jnp.concatenate([c_ctx[None, :], c, jnp.zeros((ADA_ROWS - 1 - n_lat_seq, D_MODEL), F32)], axis=0)
    n_groups = 1 + n_lat_seq
    mods_all = _adaln_all(cond, ada_w, ada_b)[:, :n_groups].reshape(DEPTH, n_groups, 6, D_MODEL)

    moe_w13_b, moe_w2_b = moe_w13.astype(BF16), moe_w2.astype(BF16)

    new_ckv, new_krope, new_gla = [], [], []
    for i in range(DEPTH):
        mods = mods_all[i]
        kind, j = i % 3, i // 3
        lng, lnb = ln_g[i, 0], ln_b[i, 0]
        if kind == 0:
            x = _gmlp_layer(x, mods, gmlp_w_in[j], gmlp_b_in[j], gmlp_v_g[j], gmlp_v_b[j], gmlp_w_s[j],
                            gmlp_b_s[j], gmlp_w_out[j], lng, lnb)
        elif kind == 1:
            swap = _rope_swap_perm()
            wd = mla_w_dqkv[j]
            r0 = MLA_Q_RANK + MLA_KV_RANK
            wd = jnp.concatenate([wd, wd[:, r0:][:, swap]], axis=1).astype(BF16)
            wuq = mla_w_uq[j].reshape(MLA_Q_RANK, MLA_HEADS, MLA_D_QK)
            wuq_rope = wuq[:, :, MLA_D_NOPE:]
            wuq = jnp.concatenate([wuq[:, :, :MLA_D_NOPE].reshape(MLA_Q_RANK, -1),
                                   wuq_rope.reshape(MLA_Q_RANK, -1),
                                   wuq_rope[:, :, swap].reshape(MLA_Q_RANK, -1)], axis=1).astype(BF16)
            wukv = mla_w_ukv[j].reshape(MLA_KV_RANK, MLA_HEADS, MLA_D_NOPE + MLA_D_V)
            wukv = jnp.concatenate([wukv[:, :, :MLA_D_NOPE].reshape(MLA_KV_RANK, -1),
                                    wukv[:, :, MLA_D_NOPE:].reshape(MLA_KV_RANK, -1)], axis=1).astype(BF16)
            weights = (wd, mla_q_norm[j].reshape(1, -1), mla_kv_norm[j].reshape(1, -1), wuq, wukv)
            cos, sin = _rope_tables(lat_len)
            tables = (jnp.tile(cos, (1, MLA_HEADS)), jnp.tile(sin, (1, MLA_HEADS)), cos, sin)

            qc, kc, vc, ckv, krope = _mla_project(x, 0, n_ctx_seq, ctx_len, mods, weights, None)
            new_ckv.append(ckv.reshape(n_ctx_seq, ctx_len, MLA_KV_RANK))
            new_krope.append(krope.reshape(n_ctx_seq, ctx_len, MLA_D_ROPE))
            ql, kl, vl, _, _ = _mla_project(x, n_ctx, n_lat_seq, lat_len, mods, weights, tables)
            k_cache, v_cache = _mla_cache_keys(cache_mla_ckv[:, j], cache_mla_krope[:, j], wukv)
            a_ctx = _attention(qc, [kc], [vc]).reshape(n_ctx, -1)
            a_lat = _attention(ql, [k_cache, kl], [v_cache, vl]).reshape(n_lat_seq * lat_len, -1)
            x = _out_proj_layer(x, mods, a_ctx, a_lat, mla_w_o[j].astype(BF16), lng, lnb)
        else:
            qkvr, g = _gla_project(x, mods, gla_w_in[j], gla_wa1[j], gla_wa2[j], gla_ba[j])
            zero_state = jnp.zeros((n_ctx_seq, 2, GLA_HEADS, GLA_DV, GLA_DK), F32)
            lat_state = jnp.swapaxes(state_gla[:, j], -1, -2)
            ofc, obc, s_ctx = _gla_scan(qkvr, g, 0, n_ctx_seq, ctx_len, zero_state)
            ofl, obl, _ = _gla_scan(qkvr, g, n_ctx, n_lat_seq, lat_len, lat_state)
            new_gla.append(jnp.swapaxes(s_ctx, -1, -2))
            x = _gla_out_layer(x, mods, (ofc, ofl), (obc, obl), qkvr, gla_norm[j], gla_w_o[j], lng, lnb)

        lng, lnb = ln_g[i, 1], ln_b[i, 1]
        f = i // 2
        if i % 2 == 0:
            x = _ffn_layer(x, mods, ffn_w13[f].astype(BF16), ffn_w2[f].astype(BF16), lng, lnb)
        else:
            split = n_ctx if i == DEPTH - 1 else None
            x = _moe_layer(x, mods, moe_router[f], moe_w13_b, moe_w2_b, f, lng, lnb, split)

    y_prompt = x[0].reshape(n_ctx_seq, ctx_len, D_MODEL)
    y_sample = x[1].reshape(n_lat_seq, lat_len, D_MODEL)
    return (y_prompt, y_sample, jnp.stack(new_ckv, axis=1), jnp.stack(new_krope, axis=1),
            jnp.stack(new_gla, axis=1))
```

```python
import functools

import jax
import jax.numpy as jnp
from jax import lax
from jax.experimental import pallas as pl
from jax.experimental.pallas import tpu as pltpu

F32 = jnp.float32
BF16 = jnp.bfloat16
HIGHEST = lax.Precision.HIGHEST

D_MODEL = 1024
DEPTH = 4
GRID_W = 64
DEEPNORM_ALPHA = (2.0 * DEPTH) ** 0.25
LN_EPS = 1e-5
RMS_EPS = 1e-6
GROUP_TOKENS = 4096

CHUNK_A = 128
GMLP_WIDTH = 2 * D_MODEL
GMLP_GROUPS = 8
GMLP_GROUP_DIM = GMLP_WIDTH // GMLP_GROUPS

MLA_HEADS = 8
MLA_Q_RANK = D_MODEL // 2
MLA_KV_RANK = D_MODEL // 4
MLA_D_NOPE = 128
MLA_D_ROPE = 64
MLA_D_V = 128
MLA_D_QK = MLA_D_NOPE + MLA_D_ROPE
MLA_D_VP = 2 * MLA_D_V
LOG2_E = 1.4426950408889634
ROPE_BASE = 10000.0

GLA_HEADS = 4
GLA_DK = 128
GLA_DV = 256
GLA_DECAY_RANK = 16
GLA_GATE_NORM = 16.0
GLA_CHUNK = 64
GLA_DQK = GLA_HEADS * GLA_DK
GLA_DVT = GLA_HEADS * GLA_DV
GLA_QKVR_W = 2 * GLA_DQK + 2 * GLA_DVT

D_FF = 2816
N_EXPERTS = 8
TOP_K = 2
D_FF_EXPERT = 3584

VMEM_LIMIT_BYTES = 56 * 1024 * 1024
ADA_ROWS = 16
ADA_TN = 1536
TM_GMLP = 512
GMLP_COL_CHUNK = 512
TM_FFN = 512
TM_PROJ = 512
TQ_ATTN = 2048
TQ_SUB_ATTN = 256
TM_OUT = 512
TB_GLA = 512
TM_ROUTE = 512
TM_EXPERT = 512
TF_EXPERT = 1792
F32_SUBLANES = 8
LOCAL_ROWS = -(-(TOP_K * TM_ROUTE + N_EXPERTS * (F32_SUBLANES - 1)) // 16) * 16


def _cparams(*sem):
    return pltpu.CompilerParams(dimension_semantics=sem, vmem_limit_bytes=VMEM_LIMIT_BYTES)


def _layer_norm(y, g, b):
    mu = jnp.mean(y, axis=-1, keepdims=True)
    d = y - mu
    var = jnp.mean(d * d, axis=-1, keepdims=True)
    return d * lax.rsqrt(var + LN_EPS) * g + b


def _rms_norm(y, g):
    return y * lax.rsqrt(jnp.mean(y * y, axis=-1, keepdims=True) + RMS_EPS) * g


def _modulate(x, mod_ref, shift_row):
    return x * (1.0 + mod_ref[shift_row + 1:shift_row + 2, :]) + mod_ref[shift_row:shift_row + 1, :]


def _post_norm(x, out, mod_ref, gate_row, lng_ref, lnb_ref):
    y = DEEPNORM_ALPHA * x + mod_ref[gate_row:gate_row + 1, :] * out
    return _layer_norm(y, lng_ref[...], lnb_ref[...])


def _dot(a, b):
    return jnp.dot(a, b, preferred_element_type=F32)


def _dot_nt(a, b):
    return lax.dot_general(a, b, (((1,), (1,)), ((), ())), preferred_element_type=F32)


def _dot_tn(a, b):
    return lax.dot_general(a, b, (((0,), (0,)), ((), ())), preferred_element_type=F32)


def _mod_spec(tm):
    return pl.BlockSpec((None, 6, D_MODEL), lambda i, *_: ((i * tm) // GROUP_TOKENS, 0, 0))


def _full_spec(shape):
    nd = len(shape)
    return pl.BlockSpec(shape, lambda *_: (0,) * nd, pipeline_mode=pl.Buffered(1))


def _ada_kernel(c_ref, w_ref, b_ref, o_ref):
    c = c_ref[...]
    s = c * jax.nn.sigmoid(c)
    o_ref[...] = jnp.dot(s, w_ref[...], precision=HIGHEST, preferred_element_type=F32) + b_ref[...]


def _adaln_all(cond, ada_w, ada_b):
    n_out = 6 * D_MODEL
    return pl.pallas_call(
        _ada_kernel,
        grid=(DEPTH, n_out // ADA_TN),
        in_specs=[
            pl.BlockSpec((ADA_ROWS, D_MODEL), lambda l, j: (0, 0)),
            pl.BlockSpec((None, D_MODEL, ADA_TN), lambda l, j: (l, 0, j)),
            pl.BlockSpec((None, 1, ADA_TN), lambda l, j: (l, 0, j)),
        ],
        out_specs=pl.BlockSpec((None, ADA_ROWS, ADA_TN), lambda l, j: (l, 0, j)),
        out_shape=jax.ShapeDtypeStruct((DEPTH, ADA_ROWS, n_out), F32),
        compiler_params=_cparams("parallel", "parallel"),
        name="adaln",
    )(cond, ada_w, ada_b.reshape(DEPTH, 1, n_out))


def _gmlp_kernel(*refs, n_ctx_tiles):
    if n_ctx_tiles is None:
        x = refs[0][...]
        refs = refs[1:]
    else:
        x = jnp.where(pl.program_id(0) < n_ctx_tiles, refs[0][...], refs[1][...])
        refs = refs[2:]
    (mod_ref, win_ref, bin_ref, vg_ref, vb_ref, ws_ref, bs_ref, wout_ref, lng_ref, lnb_ref,
     o_ref, u_ref, v_ref, vn_ref, p_ref) = refs
    hb = _modulate(x, mod_ref, 0).astype(BF16)
    n_chunks = GMLP_WIDTH // GMLP_COL_CHUNK
    total = jnp.zeros((x.shape[0], 1), F32)
    total_sq = jnp.zeros((x.shape[0], 1), F32)
    for j in range(n_chunks):
        cols = slice(j * GMLP_COL_CHUNK, (j + 1) * GMLP_COL_CHUNK)
        wcols = slice(GMLP_WIDTH + j * GMLP_COL_CHUNK, GMLP_WIDTH + (j + 1) * GMLP_COL_CHUNK)
        v = jax.nn.gelu(_dot(hb, win_ref[:, wcols]) + bin_ref[:, wcols])
        v_ref[:, cols] = v
        total = total + jnp.sum(v, axis=-1, keepdims=True)
        total_sq = total_sq + jnp.sum(v * v, axis=-1, keepdims=True)
    mean = total * (1.0 / GMLP_WIDTH)
    rstd = lax.rsqrt(total_sq * (1.0 / GMLP_WIDTH) - mean * mean + LN_EPS)
    for j in range(n_chunks):
        cols = slice(j * GMLP_COL_CHUNK, (j + 1) * GMLP_COL_CHUNK)
        u_ref[:, cols] = jax.nn.gelu(_dot(hb, win_ref[:, cols]) + bin_ref[:, cols])
        vn_ref[:, cols] = ((v_ref[:, cols] - mean) * rstd * vg_ref[:, cols] + vb_ref[:, cols]).astype(BF16)
    for c in range(x.shape[0] // CHUNK_A):
        rows = slice(c * CHUNK_A, (c + 1) * CHUNK_A)
        for g in range(GMLP_GROUPS):
            cols = slice(g * GMLP_GROUP_DIM, (g + 1) * GMLP_GROUP_DIM)
            mixed = _dot(ws_ref[g], vn_ref[rows, cols]) + bs_ref[g]
            p_ref[rows, cols] = (u_ref[rows, cols] * mixed).astype(BF16)
    out = _dot(p_ref[...], wout_ref[...])
    o_ref[...] = _post_norm(x, out, mod_ref, 2, lng_ref, lnb_ref)


def _gmlp_layer(x, mods, w_in, b_in, v_g, v_b, w_s, b_s, w_out, ln_g, ln_b):
    tm = TM_GMLP
    row = pl.BlockSpec((tm, D_MODEL), lambda i: (i, 0))
    if isinstance(x, tuple):
        n_ctx_tiles = x[0].shape[0] // tm
        n = x[0].shape[0] + x[1].shape[0]
        x_specs = _ctx_lat_specs(tm, D_MODEL, n_ctx_tiles)
    else:
        n_ctx_tiles = None
        n = x.shape[0]
        x, x_specs = (x,), [row]
    return pl.pallas_call(
        functools.partial(_gmlp_kernel, n_ctx_tiles=n_ctx_tiles),
        grid=(n // tm,),
        in_specs=x_specs + [
            _mod_spec(tm),
            _full_spec((D_MODEL, 2 * GMLP_WIDTH)), _full_spec((1, 2 * GMLP_WIDTH)),
            _full_spec((1, GMLP_WIDTH)), _full_spec((1, GMLP_WIDTH)),
            _full_spec((GMLP_GROUPS, CHUNK_A, CHUNK_A)), _full_spec((GMLP_GROUPS, CHUNK_A, 1)),
            _full_spec((GMLP_WIDTH, D_MODEL)),
            _full_spec((1, D_MODEL)), _full_spec((1, D_MODEL)),
        ],
        out_specs=row,
        out_shape=jax.ShapeDtypeStruct((n, D_MODEL), F32),
        scratch_shapes=[pltpu.VMEM((tm, GMLP_WIDTH), F32), pltpu.VMEM((tm, GMLP_WIDTH), F32),
                        pltpu.VMEM((tm, GMLP_WIDTH), BF16), pltpu.VMEM((tm, GMLP_WIDTH), BF16)],
        compiler_params=_cparams("parallel"),
        name="gmlp",
    )(*x, mods, w_in.astype(BF16), b_in.reshape(1, -1), v_g.reshape(1, -1), v_b.reshape(1, -1),
      w_s.astype(BF16), b_s.reshape(GMLP_GROUPS, CHUNK_A, 1), w_out.astype(BF16),
      ln_g.reshape(1, -1), ln_b.reshape(1, -1))


def _ffn_kernel(x_ref, mod_ref, w13_ref, w2_ref, lng_ref, lnb_ref, o_ref):
    x = x_ref[...]
    hb = _modulate(x, mod_ref, 3).astype(BF16)
    a = _dot(hb, w13_ref[:, :D_FF])
    b = _dot(hb, w13_ref[:, D_FF:])
    out = _dot((a * jax.nn.sigmoid(a) * b).astype(BF16), w2_ref[...])
    o_ref[...] = _post_norm(x, out, mod_ref, 5, lng_ref, lnb_ref)


def _ffn_layer(x, mods, w13, w2, ln_g, ln_b):
    n = x.shape[0]
    tm = TM_FFN
    row = pl.BlockSpec((tm, D_MODEL), lambda i: (i, 0))
    return pl.pallas_call(
        _ffn_kernel,
        grid=(n // tm,),
        in_specs=[row, _mod_spec(tm), _full_spec(w13.shape), _full_spec(w2.shape),
                  _full_spec((1, D_MODEL)), _full_spec((1, D_MODEL))],
        out_specs=row,
        out_shape=jax.ShapeDtypeStruct((n, D_MODEL), F32),
        compiler_params=_cparams("parallel"),
        name="ffn",
    )(x, mods, w13, w2, ln_g.reshape(1, -1), ln_b.reshape(1, -1))


def _rope_swap_perm():
    j = jnp.arange(MLA_D_ROPE)
    half = MLA_D_ROPE // 4
    return jnp.where((j % (2 * half)) < half, j + half, j - half)


def _rope_tables(n_tokens):
    n_freq = MLA_D_ROPE // 4
    t = jnp.arange(n_tokens)
    pos_r = (t // GRID_W).astype(F32)[:, None]
    pos_c = (t % GRID_W).astype(F32)[:, None]
    inv = ROPE_BASE ** (-jnp.arange(n_freq, dtype=F32) / n_freq)
    ang_r, ang_c = pos_r * inv, pos_c * inv
    cos = jnp.concatenate([jnp.cos(ang_r), jnp.cos(ang_r), jnp.cos(ang_c), jnp.cos(ang_c)], axis=-1)
    sin = jnp.concatenate([-jnp.sin(ang_r), jnp.sin(ang_r), -jnp.sin(ang_c), jnp.sin(ang_c)], axis=-1)
    return cos, sin


def _write_heads(q_ref, k_ref, v_ref, q_nope, q_rope, kv, k_rope):
    scale = MLA_D_QK ** -0.5 * LOG2_E
    k_rope = k_rope.astype(BF16)
    ones = jnp.ones((kv.shape[0], MLA_D_VP - MLA_D_V), BF16)
    for h in range(MLA_HEADS):
        nope = slice(h * MLA_D_NOPE, (h + 1) * MLA_D_NOPE)
        if q_ref is not None:
            q_ref[h, :, :MLA_D_NOPE] = (q_nope[:, nope] * scale).astype(BF16)
            q_ref[h, :, MLA_D_NOPE:] = (q_rope[:, h * MLA_D_ROPE:(h + 1) * MLA_D_ROPE] * scale).astype(BF16)
        k_ref[h, :, :MLA_D_NOPE] = kv[:, nope].astype(BF16)
        k_ref[h, :, MLA_D_NOPE:] = k_rope
        vcol = MLA_HEADS * MLA_D_NOPE + h * MLA_D_V
        v_ref[h, :, :MLA_D_V] = kv[:, vcol:vcol + MLA_D_V].astype(BF16)
        v_ref[h, :, MLA_D_V:] = ones


def _mla_proj_kernel(*refs, rope):
    if rope:
        (x_ref, mod_ref, wd_ref, qg_ref, kvg_ref, wuq_ref, wukv_ref, cq_ref, sq_ref, ck_ref, sk_ref,
         q_ref, k_ref, v_ref, ckv_ref, kr_ref) = refs
    else:
        (x_ref, mod_ref, wd_ref, qg_ref, kvg_ref, wuq_ref, wukv_ref,
         q_ref, k_ref, v_ref, ckv_ref, kr_ref) = refs
    hb = _modulate(x_ref[...], mod_ref, 0).astype(BF16)
    c = _dot(hb, wd_ref[...])
    r0 = MLA_Q_RANK + MLA_KV_RANK
    c_q = _rms_norm(c[:, :MLA_Q_RANK], qg_ref[...]).astype(BF16)
    c_kv = _rms_norm(c[:, MLA_Q_RANK:r0], kvg_ref[...])
    k_rope = c[:, r0:r0 + MLA_D_ROPE]
    ckv_ref[...] = c_kv
    kr_ref[...] = k_rope
    q = _dot(c_q, wuq_ref[...])
    n_nope = MLA_HEADS * MLA_D_NOPE
    n_rope = MLA_HEADS * MLA_D_ROPE
    q_rope = q[:, n_nope:n_nope + n_rope]
    if rope:
        q_rope = q_rope * cq_ref[...] + q[:, n_nope + n_rope:] * sq_ref[...]
        k_rope = k_rope * ck_ref[...] + c[:, r0 + MLA_D_ROPE:] * sk_ref[...]
    kv = _dot(c_kv.astype(BF16), wukv_ref[...])
    _write_heads(q_ref, k_ref, v_ref, q[:, :n_nope], q_rope, kv, k_rope)


def _mla_project(x, row_offset, n_seq, seq_len, mods, weights, tables):
    wd, qg, kvg, wuq, wukv = weights
    tm = min(TM_PROJ, seq_len)
    per_seq = seq_len // tm
    n = n_seq * seq_len
    off = row_offset // tm
    rope = tables is not None
    in_specs = [
        pl.BlockSpec((tm, D_MODEL), lambda i: (off + i, 0)),
        pl.BlockSpec((None, 6, D_MODEL), lambda i: ((row_offset + i * tm) // GROUP_TOKENS, 0, 0)),
        _full_spec(wd.shape), _full_spec(qg.shape), _full_spec(kvg.shape),
        _full_spec(wuq.shape), _full_spec(wukv.shape),
    ]
    args = [x, mods, wd, qg, kvg, wuq, wukv]
    if rope:
        pos = lambda i: (i % per_seq, 0)
        in_specs += [pl.BlockSpec((tm, MLA_HEADS * MLA_D_ROPE), pos)] * 2 + [pl.BlockSpec((tm, MLA_D_ROPE), pos)] * 2
        args += list(tables)
    head_spec = lambda d: pl.BlockSpec((None, MLA_HEADS, tm, d), lambda i: (i // per_seq, 0, i % per_seq, 0))
    head_shape = lambda d: jax.ShapeDtypeStruct((n_seq, MLA_HEADS, seq_len, d), BF16)
    return pl.pallas_call(
        functools.partial(_mla_proj_kernel, rope=rope),
        grid=(n // tm,),
        in_specs=in_specs,
        out_specs=[head_spec(MLA_D_QK), head_spec(MLA_D_QK), head_spec(MLA_D_VP),
                   pl.BlockSpec((tm, MLA_KV_RANK), lambda i: (i, 0)),
                   pl.BlockSpec((tm, MLA_D_ROPE), lambda i: (i, 0))],
        out_shape=[head_shape(MLA_D_QK), head_shape(MLA_D_QK), head_shape(MLA_D_VP),
                   jax.ShapeDtypeStruct((n, MLA_KV_RANK), F32),
                   jax.ShapeDtypeStruct((n, MLA_D_ROPE), F32)],
        compiler_params=_cparams("parallel"),
        name="mla_proj_rope" if rope else "mla_proj",
    )(*args)


def _mla_cache_kernel(ckv_ref, kr_ref, wukv_ref, k_ref, v_ref):
    kv = _dot(ckv_ref[...].astype(BF16), wukv_ref[...])
    _write_heads(None, k_ref, v_ref, None, None, kv, kr_ref[...])


def _mla_cache_keys(ckv, krope, wukv):
    b, l, _ = ckv.shape
    head_spec = lambda d: pl.BlockSpec((None, MLA_HEADS, l, d), lambda i: (i, 0, 0, 0))
    return pl.pallas_call(
        _mla_cache_kernel,
        grid=(b,),
        in_specs=[pl.BlockSpec((None, l, MLA_KV_RANK), lambda i: (i, 0, 0)),
                  pl.BlockSpec((None, l, MLA_D_ROPE), lambda i: (i, 0, 0)),
                  _full_spec(wukv.shape)],
        out_specs=[head_spec(MLA_D_QK), head_spec(MLA_D_VP)],
        out_shape=[jax.ShapeDtypeStruct((b, MLA_HEADS, l, MLA_D_QK), BF16),
                   jax.ShapeDtypeStruct((b, MLA_HEADS, l, MLA_D_VP), BF16)],
        compiler_params=_cparams("parallel"),
        name="mla_cache_keys",
    )(ckv, krope, wukv)


def _attn_kernel(*refs, n_seg):
    q_ref = refs[0]
    k_refs = refs[1:1 + n_seg]
    v_refs = refs[1 + n_seg:1 + 2 * n_seg]
    o_ref = refs[1 + 2 * n_seg]
    s_ref, p_ref = refs[2 + 2 * n_seg:]
    n_buf, sub, _ = s_ref.shape
    n_sub = q_ref.shape[0] // sub
    seg_cols = []
    col = 0
    for k_ref in k_refs:
        seg_cols.append(slice(col, col + k_ref.shape[0]))
        col += k_ref.shape[0]

    def scores(i):
        q = q_ref[i * sub:(i + 1) * sub, :]
        for k_ref, cols in zip(k_refs, seg_cols):
            s_ref[i % n_buf, :, cols] = _dot_nt(q, k_ref[...])

    def finish(i):
        s = s_ref[i % n_buf]
        m = jnp.max(s, axis=-1, keepdims=True)
        p_ref[i % n_buf] = jnp.exp2((s - m).astype(BF16))
        acc = sum(_dot(p_ref[i % n_buf, :, cols], v_ref[...]) for v_ref, cols in zip(v_refs, seg_cols))
        o_ref[i * sub:(i + 1) * sub, :] = (acc[:, :MLA_D_V] / acc[:, MLA_D_V:MLA_D_V + 1]).astype(BF16)

    scores(0)
    for i in range(n_sub):
        if i + 1 < n_sub:
            scores(i + 1)
        finish(i)


def _attention(q, ks, vs):
    b, h, t, _ = q.shape
    tq = min(TQ_ATTN, t)
    sub = min(TQ_SUB_ATTN, tq)
    n_buf = min(2, tq // sub)
    n_keys = sum(a.shape[2] for a in ks)
    seg = lambda a: pl.BlockSpec((None, None) + a.shape[2:], lambda bi, hi, qi: (bi, hi, 0, 0))
    return pl.pallas_call(
        functools.partial(_attn_kernel, n_seg=len(ks)),
        grid=(b, h, t // tq),
        in_specs=[pl.BlockSpec((None, None, tq, MLA_D_QK), lambda bi, hi, qi: (bi, hi, qi, 0))]
        + [seg(a) for a in ks] + [seg(a) for a in vs],
        out_specs=pl.BlockSpec((None, tq, MLA_D_V), lambda bi, hi, qi: (bi, qi, hi)),
        out_shape=jax.ShapeDtypeStruct((b, t, h * MLA_D_V), BF16),
        scratch_shapes=[pltpu.VMEM((n_buf, sub, n_keys), F32), pltpu.VMEM((n_buf, sub, n_keys), BF16)],
        compiler_params=_cparams("parallel", "parallel", "arbitrary"),
        name="mla_attention",
    )(q, *ks, *vs)


def _ctx_lat_specs(tm, width, n_ctx_tiles, col=0):
    return [pl.BlockSpec((tm, width), lambda i: (jnp.minimum(i, n_ctx_tiles - 1), col)),
            pl.BlockSpec((tm, width), lambda i: (jnp.maximum(i - n_ctx_tiles, 0), col))]


def _out_proj_kernel(x_ref, mod_ref, ac_ref, al_ref, w_ref, lng_ref, lnb_ref, o_ref, *, n_ctx_tiles):
    def finish(a_ref):
        out = _dot(a_ref[...], w_ref[...])
        o_ref[...] = _post_norm(x_ref[...], out, mod_ref, 2, lng_ref, lnb_ref)

    is_ctx = pl.program_id(0) < n_ctx_tiles
    pl.when(is_ctx)(lambda: finish(ac_ref))
    pl.when(jnp.logical_not(is_ctx))(lambda: finish(al_ref))


def _out_proj_layer(x, mods, a_ctx, a_lat, w_o, ln_g, ln_b):
    n = x.shape[0]
    tm = TM_OUT
    n_ctx_tiles = a_ctx.shape[0] // tm
    row = pl.BlockSpec((tm, D_MODEL), lambda i: (i, 0))
    return pl.pallas_call(
        functools.partial(_out_proj_kernel, n_ctx_tiles=n_ctx_tiles),
        grid=(n // tm,),
        in_specs=[row, _mod_spec(tm)] + _ctx_lat_specs(tm, a_ctx.shape[1], n_ctx_tiles)
        + [_full_spec(w_o.shape), _full_spec((1, D_MODEL)), _full_spec((1, D_MODEL))],
        out_specs=row,
        out_shape=jax.ShapeDtypeStruct((n, D_MODEL), F32),
        compiler_params=_cparams("parallel"),
        name="mla_out_proj",
    )(x, mods, a_ctx, a_lat, w_o, ln_g.reshape(1, -1), ln_b.reshape(1, -1))


def _gla_proj_kernel(x_ref, mod_ref, win_ref, wa2_ref, ba_ref, qkvr_ref, g_ref):
    h = _modulate(x_ref[...], mod_ref, 0).astype(BF16)
    proj = _dot(h, win_ref[...])
    qkvr_ref[:, :GLA_DQK] = (proj[:, :GLA_DQK] * (GLA_DK ** -0.5)).astype(BF16)
    qkvr_ref[:, GLA_DQK:] = proj[:, GLA_DQK:GLA_QKVR_W].astype(BF16)
    logit = _dot(proj[:, GLA_QKVR_W:].astype(BF16), wa2_ref[...]) + ba_ref[...]
    log_sig = jnp.minimum(logit, 0.0) - jnp.log(1.0 + jnp.exp(-jnp.abs(logit)))
    g_ref[...] = log_sig / GLA_GATE_NORM


def _gla_project(x, mods, w_in, wa1, wa2, ba):
    n = x.shape[0]
    tm = TM_PROJ
    lane = 128
    low_w = 2 * GLA_DECAY_RANK
    w_ext = jnp.concatenate([w_in, wa1[0], wa1[1], jnp.zeros((D_MODEL, lane - low_w), F32)], axis=1)
    zeros = jnp.zeros((GLA_DECAY_RANK, GLA_DQK), F32)
    wa2_blk = jnp.concatenate([jnp.concatenate([wa2[0], zeros], axis=1),
                               jnp.concatenate([zeros, wa2[1]], axis=1),
                               jnp.zeros((lane - low_w, 2 * GLA_DQK), F32)], axis=0)
    return pl.pallas_call(
        _gla_proj_kernel,
        grid=(n // tm,),
        in_specs=[pl.BlockSpec((tm, D_MODEL), lambda i: (i, 0)), _mod_spec(tm),
                  _full_spec(w_ext.shape), _full_spec(wa2_blk.shape), _full_spec((1, 2 * GLA_DQK))],
        out_specs=[pl.BlockSpec((tm, GLA_QKVR_W), lambda i: (i, 0)),
                   pl.BlockSpec((tm, 2 * GLA_DQK), lambda i: (i, 0))],
        out_shape=[jax.ShapeDtypeStruct((n, GLA_QKVR_W), BF16), jax.ShapeDtypeStruct((n, 2 * GLA_DQK), F32)],
        compiler_params=_cparams("parallel"),
        name="gla_proj",
    )(x, mods, w_ext.astype(BF16), wa2_blk.astype(BF16), ba.reshape(1, -1))


def _chunk_cumsum(g, reverse):
    n = g.shape[0]
    pos = lax.broadcasted_iota(jnp.int32, g.shape, 0) % GLA_CHUNK
    b = g
    shift = 1
    while shift < GLA_CHUNK:
        if reverse:
            moved = pltpu.roll(b, n - shift, 0)
            keep = pos < GLA_CHUNK - shift
        else:
            moved = pltpu.roll(b, shift, 0)
            keep = pos >= shift
        b = b + jnp.where(keep, moved, 0.0)
        shift *= 2
    return b


def _gla_scan_kernel(qf_ref, kf_ref, vf_ref, gf_ref, qb_ref, kb_ref, vb_ref, gb_ref, s0_ref,
                     of_ref, ob_ref, sfin_ref, st_ref):
    i = pl.program_id(2)

    @pl.when(i == 0)
    def _():
        st_ref[...] = s0_ref[...]

    n_chunks = qf_ref.shape[0] // GLA_CHUNK
    ii = lax.broadcasted_iota(jnp.int32, (GLA_CHUNK, GLA_CHUNK), 0)
    jj = lax.broadcasted_iota(jnp.int32, (GLA_CHUNK, GLA_CHUNK), 1)
    dirs = ((0, False, qf_ref, kf_ref, vf_ref, of_ref, gf_ref, jj <= ii),
            (1, True, qb_ref, kb_ref, vb_ref, ob_ref, gb_ref, jj >= ii))
    for d, reverse, q_ref, k_ref, v_ref, o_ref, g_ref, mask in dirs:
        b_all = _chunk_cumsum(g_ref[...], reverse)
        qes, intra, updates, decays = [], [], [], []
        for c in range(n_chunks):
            rows = slice(c * GLA_CHUNK, (c + 1) * GLA_CHUNK)
            b = b_all[rows]
            b_tot = b[:1] if reverse else b[GLA_CHUNK - 1:]
            q = q_ref[rows, :].astype(F32)
            k = k_ref[rows, :].astype(F32)
            v = v_ref[rows, :]
            qe = (q * jnp.exp(b)).astype(BF16)
            ke = (k * jnp.exp(-b)).astype(BF16)
            kd = (k * jnp.exp(b_tot - b)).astype(BF16)
            a = jnp.where(mask, _dot_nt(qe, ke), 0.0).astype(BF16)
            qes.append(qe)
            intra.append(_dot(a, v))
            updates.append(_dot_tn(v, kd))
            decays.append(jnp.exp(b_tot))
        st = st_ref[d]
        entering = [None] * n_chunks
        for c in (reversed(range(n_chunks)) if reverse else range(n_chunks)):
            entering[c] = st.astype(BF16)
            st = st * decays[c] + updates[c]
        st_ref[d] = st
        for c in range(n_chunks):
            rows = slice(c * GLA_CHUNK, (c + 1) * GLA_CHUNK)
            o_ref[rows, :] = (intra[c] + _dot_nt(qes[c], entering[c])).astype(o_ref.dtype)

    @pl.when(i == pl.num_programs(2) - 1)
    def _():
        sfin_ref[...] = st_ref[...]


def _gla_scan(qkvr, g, row_offset, n_seq, seq_len, s0_t):
    tb = min(TB_GLA, seq_len)
    nb = seq_len // tb
    off = row_offset // tb
    v_col0 = 2 * GLA_DQK // GLA_DV

    def in_specs(reverse):
        blk = (lambda s, i: off + s * nb + nb - 1 - i) if reverse else (lambda s, i: off + s * nb + i)
        g_col = GLA_HEADS if reverse else 0
        return [pl.BlockSpec((tb, GLA_DK), lambda s, h, i: (blk(s, i), h)),
                pl.BlockSpec((tb, GLA_DK), lambda s, h, i: (blk(s, i), GLA_HEADS + h)),
                pl.BlockSpec((tb, GLA_DV), lambda s, h, i: (blk(s, i), v_col0 + h)),
                pl.BlockSpec((tb, GLA_DK), lambda s, h, i: (blk(s, i), g_col + h))]

    state_spec = pl.BlockSpec((None, 2, None, GLA_DV, GLA_DK), lambda s, h, i: (s, 0, h, 0, 0))
    o_shape = jax.ShapeDtypeStruct((n_seq * seq_len, GLA_DVT), BF16)
    return pl.pallas_call(
        _gla_scan_kernel,
        grid=(n_seq, GLA_HEADS, nb),
        in_specs=in_specs(False) + in_specs(True) + [state_spec],
        out_specs=[pl.BlockSpec((tb, GLA_DV), lambda s, h, i: (s * nb + i, h)),
                   pl.BlockSpec((tb, GLA_DV), lambda s, h, i: (s * nb + nb - 1 - i, h)),
                   state_spec],
        out_shape=[o_shape, o_shape, jax.ShapeDtypeStruct((n_seq, 2, GLA_HEADS, GLA_DV, GLA_DK), F32)],
        scratch_shapes=[pltpu.VMEM((2, GLA_DV, GLA_DK), F32)],
        compiler_params=_cparams("parallel", "parallel", "arbitrary"),
        name="gla_scan",
    )(qkvr, qkvr, qkvr, g, qkvr, qkvr, qkvr, g, s0_t)


def _gla_out_kernel(x_ref, mod_ref, ofc_ref, ofl_ref, obc_ref, obl_ref, r_ref, ng_ref, wo_ref, lng_ref, lnb_ref,
                    o_ref, p_ref, *, n_ctx_tiles):
    def finish(of_ref, ob_ref):
        for h in range(GLA_HEADS):
            cols = slice(h * GLA_DV, (h + 1) * GLA_DV)
            o = _rms_norm(of_ref[:, cols].astype(F32) + ob_ref[:, cols].astype(F32), ng_ref[...])
            r = r_ref[:, cols].astype(F32)
            p_ref[:, cols] = (o * (r * jax.nn.sigmoid(r))).astype(BF16)
        out = _dot(p_ref[...], wo_ref[...])
        o_ref[...] = _post_norm(x_ref[...], out, mod_ref, 2, lng_ref, lnb_ref)

    is_ctx = pl.program_id(0) < n_ctx_tiles
    pl.when(is_ctx)(lambda: finish(ofc_ref, obc_ref))
    pl.when(jnp.logical_not(is_ctx))(lambda: finish(ofl_ref, obl_ref))


def _gla_out_layer(x, mods, o_fwd, o_bwd, proj, norm_g, w_o, ln_g, ln_b):
    n = x.shape[0]
    tm = TM_OUT
    n_ctx_tiles = o_fwd[0].shape[0] // tm
    row = pl.BlockSpec((tm, D_MODEL), lambda i: (i, 0))
    r_block = (2 * GLA_DQK + GLA_DVT) // GLA_DVT
    return pl.pallas_call(
        functools.partial(_gla_out_kernel, n_ctx_tiles=n_ctx_tiles),
        grid=(n // tm,),
        in_specs=[row, _mod_spec(tm)] + _ctx_lat_specs(tm, GLA_DVT, n_ctx_tiles) * 2
        + [pl.BlockSpec((tm, GLA_DVT), lambda i: (i, r_block)),
           _full_spec((1, GLA_DV)), _full_spec(w_o.shape),
           _full_spec((1, D_MODEL)), _full_spec((1, D_MODEL))],
        out_specs=row,
        out_shape=jax.ShapeDtypeStruct((n, D_MODEL), F32),
        scratch_shapes=[pltpu.VMEM((tm, GLA_DVT), BF16)],
        compiler_params=_cparams("parallel"),
        name="gla_out_proj",
    )(x, mods, *o_fwd, *o_bwd, proj, norm_g.reshape(1, -1), w_o.astype(BF16),
      ln_g.reshape(1, -1), ln_b.reshape(1, -1))


def _router_kernel(x_ref, mod_ref, wr_ref, before_ref, idx_ref, wt_ref, rank_ref, count_ref):
    h = _modulate(x_ref[...], mod_ref, 3)
    logits = lax.dot_general(wr_ref[...], h, (((1,), (1,)), ((), ())), precision=HIGHEST,
                             preferred_element_type=F32)
    e = lax.broadcasted_iota(jnp.int32, logits.shape, 0)
    m1 = jnp.max(logits, axis=0, keepdims=True)
    i1 = jnp.min(jnp.where(logits == m1, e, N_EXPERTS), axis=0, keepdims=True)
    rest = jnp.where(e == i1, -jnp.inf, logits)
    m2 = jnp.max(rest, axis=0, keepdims=True)
    i2 = jnp.min(jnp.where(rest == m2, e, N_EXPERTS), axis=0, keepdims=True)
    z = jnp.exp(m2 - m1)
    idx_ref[0:1, :] = i1
    idx_ref[1:2, :] = i2
    wt_ref[0:1, :] = 1.0 / (1.0 + z)
    wt_ref[1:2, :] = z / (1.0 + z)
    first, second = e == i1, e == i2
    chosen = jnp.logical_or(first, second)
    earlier = _dot(chosen.astype(BF16), before_ref[...])
    rank_ref[0:1, :] = jnp.sum(jnp.where(first, earlier, 0.0), axis=0, keepdims=True).astype(jnp.int32)
    rank_ref[1:2, :] = jnp.sum(jnp.where(second, earlier, 0.0), axis=0, keepdims=True).astype(jnp.int32)
    count = jnp.sum(chosen.astype(F32), axis=1, keepdims=True).astype(jnp.int32)
    count_ref[...] = jnp.broadcast_to(count, count_ref.shape)


def _route(x, mods, w_router):
    n = x.shape[0]
    tm = TM_ROUTE
    lane = 128
    row = pl.BlockSpec((tm, D_MODEL), lambda i: (i, 0))
    pair = pl.BlockSpec((TOP_K, tm), lambda i: (0, i))
    t = jnp.arange(tm)
    before = (t[:, None] < t[None, :]).astype(BF16)
    idx, wts, rank, counts = pl.pallas_call(
        _router_kernel,
        grid=(n // tm,),
        in_specs=[row, _mod_spec(tm), _full_spec((N_EXPERTS, D_MODEL)), _full_spec((tm, tm))],
        out_specs=[pair, pair, pair, pl.BlockSpec((None, N_EXPERTS, lane), lambda i: (i, 0, 0))],
        out_shape=[jax.ShapeDtypeStruct((TOP_K, n), jnp.int32),
                   jax.ShapeDtypeStruct((TOP_K, n), F32),
                   jax.ShapeDtypeStruct((TOP_K, n), jnp.int32),
                   jax.ShapeDtypeStruct((n // tm, N_EXPERTS, lane), jnp.int32)],
        compiler_params=_cparams("parallel"),
        name="moe_router",
    )(x, mods, w_router.T, before)
    return idx, wts, rank, counts[:, :, 0]


def _dispatch_plan(expert_idx, rank, counts, n_tiles):
    n = expert_idx.shape[1]
    seg_rows = (counts + F32_SUBLANES - 1) // F32_SUBLANES * F32_SUBLANES
    seg_local = jnp.cumsum(seg_rows, axis=1) - seg_rows
    start_of = jnp.repeat(seg_local.T, TM_ROUTE, axis=1)
    experts = jnp.arange(N_EXPERTS)[None, :, None]
    pos = rank + jnp.sum(jnp.where(expert_idx[:, None, :] == experts, start_of[None], 0), axis=1)
    tiles = (jnp.sum(seg_rows, axis=0) + TM_EXPERT - 1) // TM_EXPERT
    tile_end = jnp.cumsum(tiles)
    tile_start = tile_end - tiles
    seg_global = tile_start[None, :] * TM_EXPERT + jnp.cumsum(seg_rows, axis=0) - seg_rows
    tile_id = jnp.arange(n_tiles)
    tile_active = (tile_id < tile_end[-1]).astype(jnp.int32)
    last_tile = jnp.minimum(tile_id, tile_end[-1] - 1)
    tile_expert = jnp.sum((tile_end[None, :] <= last_tile[:, None]).astype(jnp.int32), axis=1)
    idle = tile_end[-1] + jnp.arange(n_tiles - (TOP_K * n) // TM_EXPERT)
    fill_tiles = jnp.concatenate([jnp.where(tiles > 0, tile_end - 1, -1), jnp.where(idle < n_tiles, idle, -1)])
    i32 = lambda a: a.astype(jnp.int32)
    return (i32(pos), i32(seg_rows.reshape(-1)), i32(seg_local.reshape(-1)),
            i32(seg_global.reshape(-1)), i32(tile_expert), tile_active, i32(fill_tiles))


def _segment_copies(rows_ref, local_ref, global_ref, tile, make_copy, wait):
    for e in range(N_EXPERTS):
        seg = tile * N_EXPERTS + e
        rows, local, glob = rows_ref[seg], local_ref[seg], global_ref[seg]
        size = TM_ROUTE
        while size >= F32_SUBLANES:
            done = rows & (-2 * size)

            def piece(size=size, done=done):
                copy = make_copy(pl.multiple_of(local + done, F32_SUBLANES),
                                 pl.multiple_of(glob + done, F32_SUBLANES), size)
                copy.wait() if wait else copy.start()

            pl.when((rows & size) != 0)(piece)
            size //= 2


def _dispatch_kernel(rows_ref, local_ref, global_ref, fill_ref, x_ref, mod_ref, pos_ref, xs_hbm,
                     sbuf_ref, zbuf_ref, sems, zsem):
    i, n_steps = pl.program_id(0), pl.num_programs(0)
    b = i % 2

    def segment_copies(tile, buf, wait):
        def make_copy(local, glob, size):
            return pltpu.make_async_copy(sbuf_ref.at[buf, pl.ds(local, size), :],
                                         xs_hbm.at[pl.ds(glob, size), :], sems.at[buf])
        _segment_copies(rows_ref, local_ref, global_ref, tile, make_copy, wait)

    @pl.when(i == 0)
    def _():
        zbuf_ref[...] = jnp.zeros_like(zbuf_ref)
        rows = zbuf_ref.shape[0]

        def fill(k):
            start = pl.multiple_of(fill_ref[k] * rows, rows)
            return pltpu.make_async_copy(zbuf_ref, xs_hbm.at[pl.ds(start, rows), :], zsem)

        for k in range(fill_ref.shape[0]):
            pl.when(fill_ref[k] >= 0)(lambda k=k: fill(k).start())
        for k in range(fill_ref.shape[0]):
            pl.when(fill_ref[k] >= 0)(lambda k=k: fill(k).wait())

    @pl.when(i >= 2)
    def _():
        segment_copies(i - 2, b, wait=True)

    hb = _modulate(x_ref[...], mod_ref, 3).astype(BF16)
    r = lax.broadcasted_iota(jnp.int32, (sbuf_ref.shape[1], hb.shape[0]), 0)
    onehot = jnp.logical_or(r == pos_ref[0:1, :], r == pos_ref[1:2, :])
    sbuf_ref[b] = _dot(onehot.astype(BF16), hb)
    segment_copies(i, b, wait=False)

    @pl.when(i == n_steps - 1)
    def _():
        segment_copies(i, b, wait=True)

    @pl.when(jnp.logical_and(i == n_steps - 1, i >= 1))
    def _():
        segment_copies(i - 1, 1 - b, wait=True)


def _dispatch(x, mods, pos_t, segments, fill_tiles, n_slots):
    n = x.shape[0]
    tm = TM_ROUTE
    return pl.pallas_call(
        _dispatch_kernel,
        grid_spec=pltpu.PrefetchScalarGridSpec(
            num_scalar_prefetch=4,
            grid=(n // tm,),
            in_specs=[pl.BlockSpec((tm, D_MODEL), lambda i, *_: (i, 0)),
                      pl.BlockSpec((None, 6, D_MODEL), lambda i, *_: ((i * tm) // GROUP_TOKENS, 0, 0)),
                      pl.BlockSpec((TOP_K, tm), lambda i, *_: (0, i))],
            out_specs=pl.BlockSpec(memory_space=pl.ANY),
            scratch_shapes=[pltpu.VMEM((2, LOCAL_ROWS, D_MODEL), F32),
                            pltpu.VMEM((TM_EXPERT, D_MODEL), F32),
                            pltpu.SemaphoreType.DMA((2,)), pltpu.SemaphoreType.DMA(())],
        ),
        out_shape=jax.ShapeDtypeStruct((n_slots, D_MODEL), F32),
        compiler_params=_cparams("arbitrary"),
        name="moe_dispatch",
    )(*segments, fill_tiles, x, mods, pos_t)


def _expert_kernel(te_ref, ta_ref, x_ref, w1_ref, w3_ref, w2_ref, o_ref, xb_ref, acc_ref):
    i, j = pl.program_id(0), pl.program_id(1)
    last = pl.num_programs(1) - 1
    active = ta_ref[i] == 1

    @pl.when(jnp.logical_and(j == 0, active))
    def _():
        xb_ref[...] = x_ref[...].astype(BF16)

    @pl.when(active)
    def _():
        x = xb_ref[...]
        a = _dot(x, w1_ref[...])
        b = _dot(x, w3_ref[...])
        part = _dot((a * jax.nn.sigmoid(a) * b).astype(BF16), w2_ref[...])

        @pl.when(j == 0)
        def _():
            acc_ref[...] = part

        @pl.when(j > 0)
        def _():
            acc_ref[...] += part

    @pl.when(jnp.logical_and(j == last, active))
    def _():
        o_ref[...] = acc_ref[...]

    @pl.when(jnp.logical_and(j == last, jnp.logical_not(active)))
    def _():
        o_ref[...] = jnp.zeros_like(o_ref)


def _expert_ffn(xs, tile_expert, tile_active, w13, w2, layer):
    tm, tf = TM_EXPERT, TF_EXPERT
    nf = D_FF_EXPERT // tf
    n_slots = xs.shape[0]
    row = pl.BlockSpec((tm, D_MODEL), lambda i, j, te, ta: (i, 0))
    return pl.pallas_call(
        _expert_kernel,
        grid_spec=pltpu.PrefetchScalarGridSpec(
            num_scalar_prefetch=2,
            grid=(n_slots // tm, nf),
            in_specs=[
                row,
                pl.BlockSpec((None, None, D_MODEL, tf), lambda i, j, te, ta: (layer, te[i], 0, j)),
                pl.BlockSpec((None, None, D_MODEL, tf), lambda i, j, te, ta: (layer, te[i], 0, j + nf)),
                pl.BlockSpec((None, None, tf, D_MODEL), lambda i, j, te, ta: (layer, te[i], j, 0)),
            ],
            out_specs=row,
            scratch_shapes=[pltpu.VMEM((tm, D_MODEL), BF16), pltpu.VMEM((tm, D_MODEL), F32)],
        ),
        out_shape=jax.ShapeDtypeStruct((n_slots, D_MODEL), F32),
        compiler_params=_cparams("parallel", "arbitrary"),
        name="moe_experts",
    )(tile_expert, tile_active, xs, w13, w13, w2)


def _combine_kernel(rows_ref, local_ref, global_ref, x_ref, mod_ref, pos_ref, wt_ref, y_hbm, lng_ref, lnb_ref,
                    *refs, n_ctx_tiles):
    ybuf_ref, sems = refs[-2:]
    o_refs = refs[:-2]
    i, n_steps = pl.program_id(0), pl.num_programs(0)

    def segment_copies(tile, buf, wait):
        def make_copy(local, glob, size):
            return pltpu.make_async_copy(y_hbm.at[pl.ds(glob, size), :],
                                         ybuf_ref.at[buf, pl.ds(local, size), :], sems.at[buf])
        _segment_copies(rows_ref, local_ref, global_ref, tile, make_copy, wait)

    @pl.when(i == 0)
    def _():
        ybuf_ref[...] = jnp.zeros_like(ybuf_ref)
        segment_copies(0, 0, wait=False)

    @pl.when(i + 1 < n_steps)
    def _():
        segment_copies(i + 1, (i + 1) % 2, wait=False)

    b = i % 2
    segment_copies(i, b, wait=True)
    yb = ybuf_ref[b].astype(BF16)
    r = lax.broadcasted_iota(jnp.int32, (x_ref.shape[0], yb.shape[0]), 1)
    y = (wt_ref[:, 0:1] * _dot((r == pos_ref[:, 0:1]).astype(BF16), yb)
         + wt_ref[:, 1:2] * _dot((r == pos_ref[:, 1:2]).astype(BF16), yb))
    out = _post_norm(x_ref[...], y, mod_ref, 5, lng_ref, lnb_ref)
    if n_ctx_tiles is None:
        o_refs[0][...] = out
    else:
        is_ctx = i < n_ctx_tiles

        @pl.when(is_ctx)
        def _():
            o_refs[0][...] = out

        @pl.when(jnp.logical_not(is_ctx))
        def _():
            o_refs[1][...] = out


def _combine(x, mods, pos, segments, weights_t, ys, ln_g, ln_b, n_ctx_split=None):
    n = x.shape[0]
    tm = TM_ROUTE
    row = pl.BlockSpec((tm, D_MODEL), lambda i, *_: (i, 0))
    if n_ctx_split is None:
        n_ctx_tiles = None
        out_specs, out_shape = row, jax.ShapeDtypeStruct((n, D_MODEL), F32)
    else:
        n_ctx_tiles = n_ctx_split // tm
        out_specs = [pl.BlockSpec((tm, D_MODEL), lambda i, *_: (jnp.minimum(i, n_ctx_tiles - 1), 0)),
                     pl.BlockSpec((tm, D_MODEL), lambda i, *_: (jnp.maximum(i - n_ctx_tiles, 0), 0))]
        out_shape = [jax.ShapeDtypeStruct((n_ctx_split, D_MODEL), F32),
                     jax.ShapeDtypeStruct((n - n_ctx_split, D_MODEL), F32)]
    return pl.pallas_call(
        functools.partial(_combine_kernel, n_ctx_tiles=n_ctx_tiles),
        grid_spec=pltpu.PrefetchScalarGridSpec(
            num_scalar_prefetch=3,
            grid=(n // tm,),
            in_specs=[row,
                      pl.BlockSpec((None, 6, D_MODEL), lambda i, *_: ((i * tm) // GROUP_TOKENS, 0, 0)),
                      pl.BlockSpec((tm, TOP_K), lambda i, *_: (i, 0)),
                      pl.BlockSpec((tm, TOP_K), lambda i, *_: (i, 0)),
                      pl.BlockSpec(memory_space=pl.ANY),
                      pl.BlockSpec((1, D_MODEL), lambda i, *_: (0, 0)),
                      pl.BlockSpec((1, D_MODEL), lambda i, *_: (0, 0))],
            out_specs=out_specs,
            scratch_shapes=[pltpu.VMEM((2, LOCAL_ROWS, D_MODEL), F32), pltpu.SemaphoreType.DMA((2,))],
        ),
        out_shape=out_shape,
        compiler_params=_cparams("arbitrary"),
        name="moe_combine",
    )(*segments, x, mods, pos, weights_t, ys, ln_g.reshape(1, -1), ln_b.reshape(1, -1))


def _moe_layer(x, mods, w_router, w13, w2, layer, ln_g, ln_b, n_ctx_split=None):
    n = x.shape[0]
    seg_pad = (n // TM_ROUTE) * N_EXPERTS * (F32_SUBLANES - 1)
    n_tiles = -(-(TOP_K * n + seg_pad) // TM_EXPERT) + N_EXPERTS
    expert_idx, weights, rank, counts = _route(x, mods, w_router)
    pos, seg_rows, seg_local, seg_global, tile_expert, tile_active, fill_tiles = _dispatch_plan(
        expert_idx, rank, counts, n_tiles)
    segments = (seg_rows, seg_local, seg_global)
    xs = _dispatch(x, mods, pos, segments, fill_tiles, n_tiles * TM_EXPERT)
    ys = _expert_ffn(xs, tile_expert, tile_active, w13, w2, layer)
    return _combine(x, mods, pos.T, segments, weights.T, ys, ln_g, ln_b, n_ctx_split)


def kernel(x_prompt, x_sample, cache_mla_ckv, cache_mla_krope, state_gla, c, c_ctx, ada_w, ada_b, ln_g, ln_b, gmlp_w_in, gmlp_b_in, gmlp_v_g, gmlp_v_b, gmlp_w_s, gmlp_b_s, gmlp_w_out, mla_w_dqkv, mla_q_norm, mla_kv_norm, mla_w_uq, mla_w_ukv, mla_w_o, gla_w_in, gla_wa1, gla_wa2, gla_ba, gla_norm, gla_w_o, ffn_w13, ffn_w2, moe_router, moe_w13, moe_w2):
    n_ctx_seq, ctx_len, _ = x_prompt.shape
    n_lat_seq, lat_len, _ = x_sample.shape
    n_ctx = n_ctx_seq * ctx_len
    assert n_ctx == GROUP_TOKENS and lat_len == GROUP_TOKENS
    x = (x_prompt.reshape(n_ctx, D_MODEL), x_sample.reshape(-1, D_MODEL))

    cond = jnp.concatenate([c_ctx[None, :], c, jnp.zeros((ADA_ROWS - 1 - n_lat_seq, D_MODEL), F32)], axis=0)
    n_groups = 1 + n_lat_seq
    mods_all = _adaln_all(cond, ada_w, ada_b)[:, :n_groups].reshape(DEPTH, n_groups, 6, D_MODEL)

    moe_w13_b, moe_w2_b = moe_w13.astype(BF16), moe_w2.astype(BF16)

    new_ckv, new_krope, new_gla = [], [], []
    for i in range(DEPTH):
        mods = mods_all[i]
        kind, j = i % 3, i // 3
        lng, lnb = ln_g[i, 0], ln_b[i, 0]
        if kind == 0:
            x = _gmlp_layer(x, mods, gmlp_w_in[j], gmlp_b_in[j], gmlp_v_g[j], gmlp_v_b[j], gmlp_w_s[j],
                            gmlp_b_s[j], gmlp_w_out[j], lng, lnb)
        elif kind == 1:
            swap = _rope_swap_perm()
            wd = mla_w_dqkv[j]
            r0 = MLA_Q_RANK + MLA_KV_RANK
            wd = jnp.concatenate([wd, wd[:, r0:][:, swap]], axis=1).astype(BF16)
            wuq = mla_w_uq[j].reshape(MLA_Q_RANK, MLA_HEADS, MLA_D_QK)
            wuq_rope = wuq[:, :, MLA_D_NOPE:]
            wuq = jnp.concatenate([wuq[:, :, :MLA_D_NOPE].reshape(MLA_Q_RANK, -1),
                                   wuq_rope.reshape(MLA_Q_RANK, -1),
                                   wuq_rope[:, :, swap].reshape(MLA_Q_RANK, -1)], axis=1).astype(BF16)
            wukv = mla_w_ukv[j].reshape(MLA_KV_RANK, MLA_HEADS, MLA_D_NOPE + MLA_D_V)
            wukv = jnp.concatenate([wukv[:, :, :MLA_D_NOPE].reshape(MLA_KV_RANK, -1),
                                    wukv[:, :, MLA_D_NOPE:].reshape(MLA_KV_RANK, -1)], axis=1).astype(BF16)
            weights = (wd, mla_q_norm[j].reshape(1, -1), mla_kv_norm[j].reshape(1, -1), wuq, wukv)
            cos, sin = _rope_tables(lat_len)
            tables = (jnp.tile(cos, (1, MLA_HEADS)), jnp.tile(sin, (1, MLA_HEADS)), cos, sin)

            qc, kc, vc, ckv, krope = _mla_project(x, 0, n_ctx_seq, ctx_len, mods, weights, None)
            new_ckv.append(ckv.reshape(n_ctx_seq, ctx_len, MLA_KV_RANK))
            new_krope.append(krope.reshape(n_ctx_seq, ctx_len, MLA_D_ROPE))
            ql, kl, vl, _, _ = _mla_project(x, n_ctx, n_lat_seq, lat_len, mods, weights, tables)
            k_cache, v_cache = _mla_cache_keys(cache_mla_ckv[:, j], cache_mla_krope[:, j], wukv)
            a_ctx = _attention(qc, [kc], [vc]).reshape(n_ctx, -1)
            a_lat = _attention(ql, [k_cache, kl], [v_cache, vl]).reshape(n_lat_seq * lat_len, -1)
            x = _out_proj_layer(x, mods, a_ctx, a_lat, mla_w_o[j].astype(BF16), lng, lnb)
        else:
            qkvr, g = _gla_project(x, mods, gla_w_in[j], gla_wa1[j], gla_wa2[j], gla_ba[j])
            zero_state = jnp.zeros((n_ctx_seq, 2, GLA_HEADS, GLA_DV, GLA_DK), F32)
            lat_state = jnp.swapaxes(state_gla[:, j], -1, -2)
            ofc, obc, s_ctx = _gla_scan(qkvr, g, 0, n_ctx_seq, ctx_len, zero_state)
            ofl, obl, _ = _gla_scan(qkvr, g, n_ctx, n_lat_seq, lat_len, lat_state)
            new_gla.append(jnp.swapaxes(s_ctx, -1, -2))
            x = _gla_out_layer(x, mods, (ofc, ofl), (obc, obl), qkvr, gla_norm[j], gla_w_o[j], lng, lnb)

        lng, lnb = ln_g[i, 1], ln_b[i, 1]
        f = i // 2
        if i % 2 == 0:
            x = _ffn_layer(x, mods, ffn_w13[f].astype(BF16), ffn_w2[f].astype(BF16), lng, lnb)
        else:
            split = n_ctx if i == DEPTH - 1 else None
            x = _moe_layer(x, mods, moe_router[f], moe_w13_b, moe_w2_b, f, lng, lnb, split)

    y_prompt = x[0].reshape(n_ctx_seq, ctx_len, D_MODEL)
    y_sample = x[1].reshape(n_lat_seq, lat_len, D_MODEL)
    return (y_prompt, y_sample, jnp.stack(new_ckv, axis=1), jnp.stack(new_krope, axis=1),
            jnp.stack(new_gla, axis=1))
```

```python
import functools

import jax
import jax.numpy as jnp
from jax import lax
from jax.experimental import pallas as pl
from jax.experimental.pallas import tpu as pltpu

F32 = jnp.float32
BF16 = jnp.bfloat16
HIGHEST = lax.Precision.HIGHEST

D_MODEL = 1024
DEPTH = 4
GRID_W = 64
DEEPNORM_ALPHA = (2.0 * DEPTH) ** 0.25
LN_EPS = 1e-5
RMS_EPS = 1e-6
GROUP_TOKENS = 4096

CHUNK_A = 128
GMLP_WIDTH = 2 * D_MODEL
GMLP_GROUPS = 8
GMLP_GROUP_DIM = GMLP_WIDTH // GMLP_GROUPS

MLA_HEADS = 8
MLA_Q_RANK = D_MODEL // 2
MLA_KV_RANK = D_MODEL // 4
MLA_D_NOPE = 128
MLA_D_ROPE = 64
MLA_D_V = 128
MLA_D_QK = MLA_D_NOPE + MLA_D_ROPE
MLA_D_VP = 2 * MLA_D_V
LOG2_E = 1.4426950408889634
ROPE_BASE = 10000.0

GLA_HEADS = 4
GLA_DK = 128
GLA_DV = 256
GLA_DECAY_RANK = 16
GLA_GATE_NORM = 16.0
GLA_CHUNK = 64
GLA_DQK = GLA_HEADS * GLA_DK
GLA_DVT = GLA_HEADS * GLA_DV
GLA_QKVR_W = 2 * GLA_DQK + 2 * GLA_DVT

D_FF = 2816
N_EXPERTS = 8
TOP_K = 2
D_FF_EXPERT = 3584

VMEM_LIMIT_BYTES = 56 * 1024 * 1024
ADA_ROWS = 16
ADA_TN = 1536
TM_GMLP = 512
GMLP_COL_CHUNK = 512
TM_FFN = 512
TM_PROJ = 512
TQ_ATTN = 2048
TQ_SUB_ATTN = 256
TM_OUT = 512
TB_GLA = 512
TM_ROUTE = 512
TM_EXPERT = 512
TF_EXPERT = 1792
F32_SUBLANES = 8
LOCAL_ROWS = -(-(TOP_K * TM_ROUTE + N_EXPERTS * (F32_SUBLANES - 1)) // 16) * 16


def _cparams(*sem):
    return pltpu.CompilerParams(dimension_semantics=sem, vmem_limit_bytes=VMEM_LIMIT_BYTES)


def _layer_norm(y, g, b):
    mu = jnp.mean(y, axis=-1, keepdims=True)
    d = y - mu
    var = jnp.mean(d * d, axis=-1, keepdims=True)
    return d * lax.rsqrt(var + LN_EPS) * g + b


def _rms_norm(y, g):
    return y * lax.rsqrt(jnp.mean(y * y, axis=-1, keepdims=True) + RMS_EPS) * g


def _modulate(x, mod_ref, shift_row):
    return x * (1.0 + mod_ref[shift_row + 1:shift_row + 2, :]) + mod_ref[shift_row:shift_row + 1, :]


def _post_norm(x, out, mod_ref, gate_row, lng_ref, lnb_ref):
    y = DEEPNORM_ALPHA * x + mod_ref[gate_row:gate_row + 1, :] * out
    return _layer_norm(y, lng_ref[...], lnb_ref[...])


def _dot(a, b):
    return jnp.dot(a, b, preferred_element_type=F32)


def _dot_nt(a, b):
    return lax.dot_general(a, b, (((1,), (1,)), ((), ())), preferred_element_type=F32)


def _dot_tn(a, b):
    return lax.dot_general(a, b, (((0,), (0,)), ((), ())), preferred_element_type=F32)


def _mod_spec(tm):
    return pl.BlockSpec((None, 6, D_MODEL), lambda i, *_: ((i * tm) // GROUP_TOKENS, 0, 0))


def _full_spec(shape):
    nd = len(shape)
    return pl.BlockSpec(shape, lambda *_: (0,) * nd, pipeline_mode=pl.Buffered(1))


def _ada_kernel(c_ref, w_ref, b_ref, o_ref):
    c = c_ref[...]
    s = c * jax.nn.sigmoid(c)
    o_ref[...] = jnp.dot(s, w_ref[...], precision=HIGHEST, preferred_element_type=F32) + b_ref[...]


def _adaln_all(cond, ada_w, ada_b):
    n_out = 6 * D_MODEL
    return pl.pallas_call(
        _ada_kernel,
        grid=(DEPTH, n_out // ADA_TN),
        in_specs=[
            pl.BlockSpec((ADA_ROWS, D_MODEL), lambda l, j: (0, 0)),
            pl.BlockSpec((None, D_MODEL, ADA_TN), lambda l, j: (l, 0, j)),
            pl.BlockSpec((None, 1, ADA_TN), lambda l, j: (l, 0, j)),
        ],
        out_specs=pl.BlockSpec((None, ADA_ROWS, ADA_TN), lambda l, j: (l, 0, j)),
        out_shape=jax.ShapeDtypeStruct((DEPTH, ADA_ROWS, n_out), F32),
        compiler_params=_cparams("parallel", "parallel"),
        name="adaln",
    )(cond, ada_w, ada_b.reshape(DEPTH, 1, n_out))


def _gmlp_kernel(*refs, n_ctx_tiles):
    if n_ctx_tiles is None:
        x = refs[0][...]
        refs = refs[1:]
    else:
        x = jnp.where(pl.program_id(0) < n_ctx_tiles, refs[0][...], refs[1][...])
        refs = refs[2:]
    (mod_ref, win_ref, bin_ref, vg_ref, vb_ref, ws_ref, bs_ref, wout_ref, lng_ref, lnb_ref,
     o_ref, u_ref, v_ref, vn_ref, p_ref) = refs
    hb = _modulate(x, mod_ref, 0).astype(BF16)
    n_chunks = GMLP_WIDTH // GMLP_COL_CHUNK
    total = jnp.zeros((x.shape[0], 1), F32)
    total_sq = jnp.zeros((x.shape[0], 1), F32)
    for j in range(n_chunks):
        cols = slice(j * GMLP_COL_CHUNK, (j + 1) * GMLP_COL_CHUNK)
        wcols = slice(GMLP_WIDTH + j * GMLP_COL_CHUNK, GMLP_WIDTH + (j + 1) * GMLP_COL_CHUNK)
        v = jax.nn.gelu(_dot(hb, win_ref[:, wcols]) + bin_ref[:, wcols])
        v_ref[:, cols] = v
        total = total + jnp.sum(v, axis=-1, keepdims=True)
        total_sq = total_sq + jnp.sum(v * v, axis=-1, keepdims=True)
    mean = total * (1.0 / GMLP_WIDTH)
    rstd = lax.rsqrt(total_sq * (1.0 / GMLP_WIDTH) - mean * mean + LN_EPS)
    for j in range(n_chunks):
        cols = slice(j * GMLP_COL_CHUNK, (j + 1) * GMLP_COL_CHUNK)
        u_ref[:, cols] = jax.nn.gelu(_dot(hb, win_ref[:, cols]) + bin_ref[:, cols])
        vn_ref[:, cols] = ((v_ref[:, cols] - mean) * rstd * vg_ref[:, cols] + vb_ref[:, cols]).astype(BF16)
    for c in range(x.shape[0] // CHUNK_A):
        rows = slice(c * CHUNK_A, (c + 1) * CHUNK_A)
        for g in range(GMLP_GROUPS):
            cols = slice(g * GMLP_GROUP_DIM, (g + 1) * GMLP_GROUP_DIM)
            mixed = _dot(ws_ref[g], vn_ref[rows, cols]) + bs_ref[g]
            p_ref[rows, cols] = (u_ref[rows, cols] * mixed).astype(BF16)
    out = _dot(p_ref[...], wout_ref[...])
    o_ref[...] = _post_norm(x, out, mod_ref, 2, lng_ref, lnb_ref)


def _gmlp_layer(x, mods, w_in, b_in, v_g, v_b, w_s, b_s, w_out, ln_g, ln_b):
    tm = TM_GMLP
    row = pl.BlockSpec((tm, D_MODEL), lambda i: (i, 0))
    if isinstance(x, tuple):
        n_ctx_tiles = x[0].shape[0] // tm
        n = x[0].shape[0] + x[1].shape[0]
        x_specs = _ctx_lat_specs(tm, D_MODEL, n_ctx_tiles)
    else:
        n_ctx_tiles = None
        n = x.shape[0]
        x, x_specs = (x,), [row]
    return pl.pallas_call(
        functools.partial(_gmlp_kernel, n_ctx_tiles=n_ctx_tiles),
        grid=(n // tm,),
        in_specs=x_specs + [
            _mod_spec(tm),
            _full_spec((D_MODEL, 2 * GMLP_WIDTH)), _full_spec((1, 2 * GMLP_WIDTH)),
            _full_spec((1, GMLP_WIDTH)), _full_spec((1, GMLP_WIDTH)),
            _full_spec((GMLP_GROUPS, CHUNK_A, CHUNK_A)), _full_spec((GMLP_GROUPS, CHUNK_A, 1)),
            _full_spec((GMLP_WIDTH, D_MODEL)),
            _full_spec((1, D_MODEL)), _full_spec((1, D_MODEL)),
        ],
        out_specs=row,
        out_shape=jax.ShapeDtypeStruct((n, D_MODEL), F32),
        scratch_shapes=[pltpu.VMEM((tm, GMLP_WIDTH), F32), pltpu.VMEM((tm, GMLP_WIDTH), F32),
                        pltpu.VMEM((tm, GMLP_WIDTH), BF16), pltpu.VMEM((tm, GMLP_WIDTH), BF16)],
        compiler_params=_cparams("parallel"),
        name="gmlp",
    )(*x, mods, w_in.astype(BF16), b_in.reshape(1, -1), v_g.reshape(1, -1), v_b.reshape(1, -1),
      w_s.astype(BF16), b_s.reshape(GMLP_GROUPS, CHUNK_A, 1), w_out.astype(BF16),
      ln_g.reshape(1, -1), ln_b.reshape(1, -1))


def _ffn_kernel(x_ref, mod_ref, w13_ref, w2_ref, lng_ref, lnb_ref, o_ref):
    x = x_ref[...]
    hb = _modulate(x, mod_ref, 3).astype(BF16)
    a = _dot(hb, w13_ref[:, :D_FF])
    b = _dot(hb, w13_ref[:, D_FF:])
    out = _dot((a * jax.nn.sigmoid(a) * b).astype(BF16), w2_ref[...])
    o_ref[...] = _post_norm(x, out, mod_ref, 5, lng_ref, lnb_ref)


def _ffn_layer(x, mods, w13, w2, ln_g, ln_b):
    n = x.shape[0]
    tm = TM_FFN
    row = pl.BlockSpec((tm, D_MODEL), lambda i: (i, 0))
    return pl.pallas_call(
        _ffn_kernel,
        grid=(n // tm,),
        in_specs=[row, _mod_spec(tm), _full_spec(w13.shape), _full_spec(w2.shape),
                  _full_spec((1, D_MODEL)), _full_spec((1, D_MODEL))],
        out_specs=row,
        out_shape=jax.ShapeDtypeStruct((n, D_MODEL), F32),
        compiler_params=_cparams("parallel"),
        name="ffn",
    )(x, mods, w13, w2, ln_g.reshape(1, -1), ln_b.reshape(1, -1))


def _rope_swap_perm():
    j = jnp.arange(MLA_D_ROPE)
    half = MLA_D_ROPE // 4
    return jnp.where((j % (2 * half)) < half, j + half, j - half)


def _rope_tables(n_tokens):
    n_freq = MLA_D_ROPE // 4
    t = jnp.arange(n_tokens)
    pos_r = (t // GRID_W).astype(F32)[:, None]
    pos_c = (t % GRID_W).astype(F32)[:, None]
    inv = ROPE_BASE ** (-jnp.arange(n_freq, dtype=F32) / n_freq)
    ang_r, ang_c = pos_r * inv, pos_c * inv
    cos = jnp.concatenate([jnp.cos(ang_r), jnp.cos(ang_r), jnp.cos(ang_c), jnp.cos(ang_c)], axis=-1)
    sin = jnp.concatenate([-jnp.sin(ang_r), jnp.sin(ang_r), -jnp.sin(ang_c), jnp.sin(ang_c)], axis=-1)
    return cos, sin


def _write_heads(q_ref, k_ref, v_ref, q_nope, q_rope, kv, k_rope):
    scale = MLA_D_QK ** -0.5 * LOG2_E
    k_rope = k_rope.astype(BF16)
    ones = jnp.ones((kv.shape[0], MLA_D_VP - MLA_D_V), BF16)
    for h in range(MLA_HEADS):
        nope = slice(h * MLA_D_NOPE, (h + 1) * MLA_D_NOPE)
        if q_ref is not None:
            q_ref[h, :, :MLA_D_NOPE] = (q_nope[:, nope] * scale).astype(BF16)
            q_ref[h, :, MLA_D_NOPE:] = (q_rope[:, h * MLA_D_ROPE:(h + 1) * MLA_D_ROPE] * scale).astype(BF16)
        k_ref[h, :, :MLA_D_NOPE] = kv[:, nope].astype(BF16)
        k_ref[h, :, MLA_D_NOPE:] = k_rope
        vcol = MLA_HEADS * MLA_D_NOPE + h * MLA_D_V
        v_ref[h, :, :MLA_D_V] = kv[:, vcol:vcol + MLA_D_V].astype(BF16)
        v_ref[h, :, MLA_D_V:] = ones


def _mla_proj_kernel(*refs, rope):
    if rope:
        (x_ref, mod_ref, wd_ref, qg_ref, kvg_ref, wuq_ref, wukv_ref, cq_ref, sq_ref, ck_ref, sk_ref,
         q_ref, k_ref, v_ref, ckv_ref, kr_ref) = refs
    else:
        (x_ref, mod_ref, wd_ref, qg_ref, kvg_ref, wuq_ref, wukv_ref,
         q_ref, k_ref, v_ref, ckv_ref, kr_ref) = refs
    hb = _modulate(x_ref[...], mod_ref, 0).astype(BF16)
    c = _dot(hb, wd_ref[...])
    r0 = MLA_Q_RANK + MLA_KV_RANK
    c_q = _rms_norm(c[:, :MLA_Q_RANK], qg_ref[...]).astype(BF16)
    c_kv = _rms_norm(c[:, MLA_Q_RANK:r0], kvg_ref[...])
    k_rope = c[:, r0:r0 + MLA_D_ROPE]
    ckv_ref[...] = c_kv
    kr_ref[...] = k_rope
    q = _dot(c_q, wuq_ref[...])
    n_nope = MLA_HEADS * MLA_D_NOPE
    n_rope = MLA_HEADS * MLA_D_ROPE
    q_rope = q[:, n_nope:n_nope + n_rope]
    if rope:
        q_rope = q_rope * cq_ref[...] + q[:, n_nope + n_rope:] * sq_ref[...]
        k_rope = k_rope * ck_ref[...] + c[:, r0 + MLA_D_ROPE:] * sk_ref[...]
    kv = _dot(c_kv.astype(BF16), wukv_ref[...])
    _write_heads(q_ref, k_ref, v_ref, q[:, :n_nope], q_rope, kv, k_rope)


def _mla_project(x, row_offset, n_seq, seq_len, mods, weights, tables):
    wd, qg, kvg, wuq, wukv = weights
    tm = min(TM_PROJ, seq_len)
    per_seq = seq_len // tm
    n = n_seq * seq_len
    off = row_offset // tm
    rope = tables is not None
    in_specs = [
        pl.BlockSpec((tm, D_MODEL), lambda i: (off + i, 0)),
        pl.BlockSpec((None, 6, D_MODEL), lambda i: ((row_offset + i * tm) // GROUP_TOKENS, 0, 0)),
        _full_spec(wd.shape), _full_spec(qg.shape), _full_spec(kvg.shape),
        _full_spec(wuq.shape), _full_spec(wukv.shape),
    ]
    args = [x, mods, wd, qg, kvg, wuq, wukv]
    if rope:
        pos = lambda i: (i % per_seq, 0)
        in_specs += [pl.BlockSpec((tm, MLA_HEADS * MLA_D_ROPE), pos)] * 2 + [pl.BlockSpec((tm, MLA_D_ROPE), pos)] * 2
        args += list(tables)
    head_spec = lambda d: pl.BlockSpec((None, MLA_HEADS, tm, d), lambda i: (i // per_seq, 0, i % per_seq, 0))
    head_shape = lambda d: jax.ShapeDtypeStruct((n_seq, MLA_HEADS, seq_len, d), BF16)
    return pl.pallas_call(
        functools.partial(_mla_proj_kernel, rope=rope),
        grid=(n // tm,),
        in_specs=in_specs,
        out_specs=[head_spec(MLA_D_QK), head_spec(MLA_D_QK), head_spec(MLA_D_VP),
                   pl.BlockSpec((tm, MLA_KV_RANK), lambda i: (i, 0)),
                   pl.BlockSpec((tm, MLA_D_ROPE), lambda i: (i, 0))],
        out_shape=[head_shape(MLA_D_QK), head_shape(MLA_D_QK), head_shape(MLA_D_VP),
                   jax.ShapeDtypeStruct((n, MLA_KV_RANK), F32),
                   jax.ShapeDtypeStruct((n, MLA_D_ROPE), F32)],
        compiler_params=_cparams("parallel"),
        name="mla_proj_rope" if rope else "mla_proj",
    )(*args)


def _mla_cache_kernel(ckv_ref, kr_ref, wukv_ref, k_ref, v_ref):
    kv = _dot(ckv_ref[...].astype(BF16), wukv_ref[...])
    _write_heads(None, k_ref, v_ref, None, None, kv, kr_ref[...])


def _mla_cache_keys(ckv, krope, wukv):
    b, l, _ = ckv.shape
    head_spec = lambda d: pl.BlockSpec((None, MLA_HEADS, l, d), lambda i: (i, 0, 0, 0))
    return pl.pallas_call(
        _mla_cache_kernel,
        grid=(b,),
        in_specs=[pl.BlockSpec((None, l, MLA_KV_RANK), lambda i: (i, 0, 0)),
                  pl.BlockSpec((None, l, MLA_D_ROPE), lambda i: (i, 0, 0)),
                  _full_spec(wukv.shape)],
        out_specs=[head_spec(MLA_D_QK), head_spec(MLA_D_VP)],
        out_shape=[jax.ShapeDtypeStruct((b, MLA_HEADS, l, MLA_D_QK), BF16),
                   jax.ShapeDtypeStruct((b, MLA_HEADS, l, MLA_D_VP), BF16)],
        compiler_params=_cparams("parallel"),
        name="mla_cache_keys",
    )(ckv, krope, wukv)


def _attn_kernel(*refs, n_seg):
    q_ref = refs[0]
    k_refs = refs[1:1 + n_seg]
    v_refs = refs[1 + n_seg:1 + 2 * n_seg]
    o_ref = refs[1 + 2 * n_seg]
    s_ref, p_ref = refs[2 + 2 * n_seg:]
    n_buf, sub, _ = s_ref.shape
    n_sub = q_ref.shape[0] // sub
    seg_cols = []
    col = 0
    for k_ref in k_refs:
        seg_cols.append(slice(col, col + k_ref.shape[0]))
        col += k_ref.shape[0]

    def scores(i):
        q = q_ref[i * sub:(i + 1) * sub, :]
        for k_ref, cols in zip(k_refs, seg_cols):
            s_ref[i % n_buf, :, cols] = _dot_nt(q, k_ref[...])

    def finish(i):
        s = s_ref[i % n_buf]
        m = jnp.max(s, axis=-1, keepdims=True)
        p_ref[i % n_buf] = jnp.exp2((s - m).astype(BF16))
        acc = sum(_dot(p_ref[i % n_buf, :, cols], v_ref[...]) for v_ref, cols in zip(v_refs, seg_cols))
        o_ref[i * sub:(i + 1) * sub, :] = (acc[:, :MLA_D_V] / acc[:, MLA_D_V:MLA_D_V + 1]).astype(BF16)

    scores(0)
    for i in range(n_sub):
        if i + 1 < n_sub:
            scores(i + 1)
        finish(i)


def _attention(q, ks, vs):
    b, h, t, _ = q.shape
    tq = min(TQ_ATTN, t)
    sub = min(TQ_SUB_ATTN, tq)
    n_buf = min(2, tq // sub)
    n_keys = sum(a.shape[2] for a in ks)
    seg = lambda a: pl.BlockSpec((None, None) + a.shape[2:], lambda bi, hi, qi: (bi, hi, 0, 0))
    return pl.pallas_call(
        functools.partial(_attn_kernel, n_seg=len(ks)),
        grid=(b, h, t // tq),
        in_specs=[pl.BlockSpec((None, None, tq, MLA_D_QK), lambda bi, hi, qi: (bi, hi, qi, 0))]
        + [seg(a) for a in ks] + [seg(a) for a in vs],
        out_specs=pl.BlockSpec((None, tq, MLA_D_V), lambda bi, hi, qi: (bi, qi, hi)),
        out_shape=jax.ShapeDtypeStruct((b, t, h * MLA_D_V), BF16),
        scratch_shapes=[pltpu.VMEM((n_buf, sub, n_keys), F32), pltpu.VMEM((n_buf, sub, n_keys), BF16)],
        compiler_params=_cparams("parallel", "parallel", "arbitrary"),
        name="mla_attention",
    )(q, *ks, *vs)


def _ctx_lat_specs(tm, width, n_ctx_tiles, col=0):
    return [pl.BlockSpec((tm, width), lambda i: (jnp.minimum(i, n_ctx_tiles - 1), col)),
            pl.BlockSpec((tm, width), lambda i: (jnp.maximum(i - n_ctx_tiles, 0), col))]


def _out_proj_kernel(x_ref, mod_ref, ac_ref, al_ref, w_ref, lng_ref, lnb_ref, o_ref, *, n_ctx_tiles):
    def finish(a_ref):
        out = _dot(a_ref[...], w_ref[...])
        o_ref[...] = _post_norm(x_ref[...], out, mod_ref, 2, lng_ref, lnb_ref)

    is_ctx = pl.program_id(0) < n_ctx_tiles
    pl.when(is_ctx)(lambda: finish(ac_ref))
    pl.when(jnp.logical_not(is_ctx))(lambda: finish(al_ref))


def _out_proj_layer(x, mods, a_ctx, a_lat, w_o, ln_g, ln_b):
    n = x.shape[0]
    tm = TM_OUT
    n_ctx_tiles = a_ctx.shape[0] // tm
    row = pl.BlockSpec((tm, D_MODEL), lambda i: (i, 0))
    return pl.pallas_call(
        functools.partial(_out_proj_kernel, n_ctx_tiles=n_ctx_tiles),
        grid=(n // tm,),
        in_specs=[row, _mod_spec(tm)] + _ctx_lat_specs(tm, a_ctx.shape[1], n_ctx_tiles)
        + [_full_spec(w_o.shape), _full_spec((1, D_MODEL)), _full_spec((1, D_MODEL))],
        out_specs=row,
        out_shape=jax.ShapeDtypeStruct((n, D_MODEL), F32),
        compiler_params=_cparams("parallel"),
        name="mla_out_proj",
    )(x, mods, a_ctx, a_lat, w_o, ln_g.reshape(1, -1), ln_b.reshape(1, -1))


def _gla_proj_kernel(x_ref, mod_ref, win_ref, wa2_ref, ba_ref, qkvr_ref, g_ref):
    h = _modulate(x_ref[...], mod_ref, 0).astype(BF16)
    proj = _dot(h, win_ref[...])
    qkvr_ref[:, :GLA_DQK] = (proj[:, :GLA_DQK] * (GLA_DK ** -0.5)).astype(BF16)
    qkvr_ref[:, GLA_DQK:] = proj[:, GLA_DQK:GLA_QKVR_W].astype(BF16)
    logit = _dot(proj[:, GLA_QKVR_W:].astype(BF16), wa2_ref[...]) + ba_ref[...]
    log_sig = jnp.minimum(logit, 0.0) - jnp.log(1.0 + jnp.exp(-jnp.abs(logit)))
    g_ref[...] = log_sig / GLA_GATE_NORM


def _gla_project(x, mods, w_in, wa1, wa2, ba):
    n = x.shape[0]
    tm = TM_PROJ
    lane = 128
    low_w = 2 * GLA_DECAY_RANK
    w_ext = jnp.concatenate([w_in, wa1[0], wa1[1], jnp.zeros((D_MODEL, lane - low_w), F32)], axis=1)
    zeros = jnp.zeros((GLA_DECAY_RANK, GLA_DQK), F32)
    wa2_blk = jnp.concatenate([jnp.concatenate([wa2[0], zeros], axis=1),
                               jnp.concatenate([zeros, wa2[1]], axis=1),
                               jnp.zeros((lane - low_w, 2 * GLA_DQK), F32)], axis=0)
    return pl.pallas_call(
        _gla_proj_kernel,
        grid=(n // tm,),
        in_specs=[pl.BlockSpec((tm, D_MODEL), lambda i: (i, 0)), _mod_spec(tm),
                  _full_spec(w_ext.shape), _full_spec(wa2_blk.shape), _full_spec((1, 2 * GLA_DQK))],
        out_specs=[pl.BlockSpec((tm, GLA_QKVR_W), lambda i: (i, 0)),
                   pl.BlockSpec((tm, 2 * GLA_DQK), lambda i: (i, 0))],
        out_shape=[jax.ShapeDtypeStruct((n, GLA_QKVR_W), BF16), jax.ShapeDtypeStruct((n, 2 * GLA_DQK), F32)],
        compiler_params=_cparams("parallel"),
        name="gla_proj",
    )(x, mods, w_ext.astype(BF16), wa2_blk.astype(BF16), ba.reshape(1, -1))


def _chunk_cumsum(g, reverse):
    n = g.shape[0]
    pos = lax.broadcasted_iota(jnp.int32, g.shape, 0) % GLA_CHUNK
    b = g
    shift = 1
    while shift < GLA_CHUNK:
        if reverse:
            moved = pltpu.roll(b, n - shift, 0)
            keep = pos < GLA_CHUNK - shift
        else:
            moved = pltpu.roll(b, shift, 0)
            keep = pos >= shift
        b = b + jnp.where(keep, moved, 0.0)
        shift *= 2
    return b


def _gla_scan_kernel(qf_ref, kf_ref, vf_ref, gf_ref, qb_ref, kb_ref, vb_ref, gb_ref, s0_ref,
                     of_ref, ob_ref, sfin_ref, st_ref):
    i = pl.program_id(2)

    @pl.when(i == 0)
    def _():
        st_ref[...] = s0_ref[...]

    n_chunks = qf_ref.shape[0] // GLA_CHUNK
    ii = lax.broadcasted_iota(jnp.int32, (GLA_CHUNK, GLA_CHUNK), 0)
    jj = lax.broadcasted_iota(jnp.int32, (GLA_CHUNK, GLA_CHUNK), 1)
    dirs = ((0, False, qf_ref, kf_ref, vf_ref, of_ref, gf_ref, jj <= ii),
            (1, True, qb_ref, kb_ref, vb_ref, ob_ref, gb_ref, jj >= ii))
    for d, reverse, q_ref, k_ref, v_ref, o_ref, g_ref, mask in dirs:
        b_all = _chunk_cumsum(g_ref[...], reverse)
        qes, intra, updates, decays = [], [], [], []
        for c in range(n_chunks):
            rows = slice(c * GLA_CHUNK, (c + 1) * GLA_CHUNK)
            b = b_all[rows]
            b_tot = b[:1] if reverse else b[GLA_CHUNK - 1:]
            q = q_ref[rows, :].astype(F32)
            k = k_ref[rows, :].astype(F32)
            v = v_ref[rows, :]
            qe = (q * jnp.exp(b)).astype(BF16)
            ke = (k * jnp.exp(-b)).astype(BF16)
            kd = (k * jnp.exp(b_tot - b)).astype(BF16)
            a = jnp.where(mask, _dot_nt(qe, ke), 0.0).astype(BF16)
            qes.append(qe)
            intra.append(_dot(a, v))
            updates.append(_dot_tn(v, kd))
            decays.append(jnp.exp(b_tot))
        st = st_ref[d]
        entering = [None] * n_chunks
        for c in (reversed(range(n_chunks)) if reverse else range(n_chunks)):
            entering[c] = st.astype(BF16)
            st = st * decays[c] + updates[c]
        st_ref[d] = st
        for c in range(n_chunks):
            rows = slice(c * GLA_CHUNK, (c + 1) * GLA_CHUNK)
            o_ref[rows, :] = (intra[c] + _dot_nt(qes[c], entering[c])).astype(o_ref.dtype)

    @pl.when(i == pl.num_programs(2) - 1)
    def _():
        sfin_ref[...] = st_ref[...]


def _gla_scan(qkvr, g, row_offset, n_seq, seq_len, s0_t):
    tb = min(TB_GLA, seq_len)
    nb = seq_len // tb
    off = row_offset // tb
    v_col0 = 2 * GLA_DQK // GLA_DV

    def in_specs(reverse):
        blk = (lambda s, i: off + s * nb + nb - 1 - i) if reverse else (lambda s, i: off + s * nb + i)
        g_col = GLA_HEADS if reverse else 0
        return [pl.BlockSpec((tb, GLA_DK), lambda s, h, i: (blk(s, i), h)),
                pl.BlockSpec((tb, GLA_DK), lambda s, h, i: (blk(s, i), GLA_HEADS + h)),
                pl.BlockSpec((tb, GLA_DV), lambda s, h, i: (blk(s, i), v_col0 + h)),
                pl.BlockSpec((tb, GLA_DK), lambda s, h, i: (blk(s, i), g_col + h))]

    state_spec = pl.BlockSpec((None, 2, None, GLA_DV, GLA_DK), lambda s, h, i: (s, 0, h, 0, 0))
    o_shape = jax.ShapeDtypeStruct((n_seq * seq_len, GLA_DVT), BF16)
    return pl.pallas_call(
        _gla_scan_kernel,
        grid=(n_seq, GLA_HEADS, nb),
        in_specs=in_specs(False) + in_specs(True) + [state_spec],
        out_specs=[pl.BlockSpec((tb, GLA_DV), lambda s, h, i: (s * nb + i, h)),
                   pl.BlockSpec((tb, GLA_DV), lambda s, h, i: (s * nb + nb - 1 - i, h)),
                   state_spec],
        out_shape=[o_shape, o_shape, jax.ShapeDtypeStruct((n_seq, 2, GLA_HEADS, GLA_DV, GLA_DK), F32)],
        scratch_shapes=[pltpu.VMEM((2, GLA_DV, GLA_DK), F32)],
        compiler_params=_cparams("parallel", "parallel", "arbitrary"),
        name="gla_scan",
    )(qkvr, qkvr, qkvr, g, qkvr, qkvr, qkvr, g, s0_t)


def _gla_out_kernel(x_ref, mod_ref, ofc_ref, ofl_ref, obc_ref, obl_ref, r_ref, ng_ref, wo_ref, lng_ref, lnb_ref,
                    o_ref, p_ref, *, n_ctx_tiles):
    def finish(of_ref, ob_ref):
        for h in range(GLA_HEADS):
            cols = slice(h * GLA_DV, (h + 1) * GLA_DV)
            o = _rms_norm(of_ref[:, cols].astype(F32) + ob_ref[:, cols].astype(F32), ng_ref[...])
            r = r_ref[:, cols].astype(F32)
            p_ref[:, cols] = (o * (r * jax.nn.sigmoid(r))).astype(BF16)
        out = _dot(p_ref[...], wo_ref[...])
        o_ref[...] = _post_norm(x_ref[...], out, mod_ref, 2, lng_ref, lnb_ref)

    is_ctx = pl.program_id(0) < n_ctx_tiles
    pl.when(is_ctx)(lambda: finish(ofc_ref, obc_ref))
    pl.when(jnp.logical_not(is_ctx))(lambda: finish(ofl_ref, obl_ref))


def _gla_out_layer(x, mods, o_fwd, o_bwd, proj, norm_g, w_o, ln_g, ln_b):
    n = x.shape[0]
    tm = TM_OUT
    n_ctx_tiles = o_fwd[0].shape[0] // tm
    row = pl.BlockSpec((tm, D_MODEL), lambda i: (i, 0))
    r_block = (2 * GLA_DQK + GLA_DVT) // GLA_DVT
    return pl.pallas_call(
        functools.partial(_gla_out_kernel, n_ctx_tiles=n_ctx_tiles),
        grid=(n // tm,),
        in_specs=[row, _mod_spec(tm)] + _ctx_lat_specs(tm, GLA_DVT, n_ctx_tiles) * 2
        + [pl.BlockSpec((tm, GLA_DVT), lambda i: (i, r_block)),
           _full_spec((1, GLA_DV)), _full_spec(w_o.shape),
           _full_spec((1, D_MODEL)), _full_spec((1, D_MODEL))],
        out_specs=row,
        out_shape=jax.ShapeDtypeStruct((n, D_MODEL), F32),
        scratch_shapes=[pltpu.VMEM((tm, GLA_DVT), BF16)],
        compiler_params=_cparams("parallel"),
        name="gla_out_proj",
    )(x, mods, *o_fwd, *o_bwd, proj, norm_g.reshape(1, -1), w_o.astype(BF16),
      ln_g.reshape(1, -1), ln_b.reshape(1, -1))


def _router_kernel(x_ref, mod_ref, wr_ref, before_ref, idx_ref, wt_ref, rank_ref, count_ref):
    h = _modulate(x_ref[...], mod_ref, 3)
    w = wr_ref[...]
    h_hi, w_hi = h.astype(BF16), w.astype(BF16)
    h_lo, w_lo = (h - h_hi.astype(F32)).astype(BF16), (w - w_hi.astype(F32)).astype(BF16)
    logits = _dot_nt(w_hi, h_hi) + (_dot_nt(w_hi, h_lo) + _dot_nt(w_lo, h_hi))
    e = lax.broadcasted_iota(jnp.int32, logits.shape, 0)
    m1 = jnp.max(logits, axis=0, keepdims=True)
    i1 = jnp.min(jnp.where(logits == m1, e, N_EXPERTS), axis=0, keepdims=True)
    rest = jnp.where(e == i1, -jnp.inf, logits)
    m2 = jnp.max(rest, axis=0, keepdims=True)
    i2 = jnp.min(jnp.where(rest == m2, e, N_EXPERTS), axis=0, keepdims=True)
    z = jnp.exp(m2 - m1)
    idx_ref[0:1, :] = i1
    idx_ref[1:2, :] = i2
    wt_ref[0:1, :] = 1.0 / (1.0 + z)
    wt_ref[1:2, :] = z / (1.0 + z)
    first, second = e == i1, e == i2
    chosen = jnp.logical_or(first, second)
    earlier = _dot(chosen.astype(BF16), before_ref[...])
    rank_ref[0:1, :] = jnp.sum(jnp.where(first, earlier, 0.0), axis=0, keepdims=True).astype(jnp.int32)
    rank_ref[1:2, :] = jnp.sum(jnp.where(second, earlier, 0.0), axis=0, keepdims=True).astype(jnp.int32)
    count = jnp.sum(chosen.astype(F32), axis=1, keepdims=True).astype(jnp.int32)
    count_ref[...] = jnp.broadcast_to(count, count_ref.shape)


def _route(x, mods, w_router):
    n = x.shape[0]
    tm = TM_ROUTE
    lane = 128
    row = pl.BlockSpec((tm, D_MODEL), lambda i: (i, 0))
    pair = pl.BlockSpec((TOP_K, tm), lambda i: (0, i))
    t = jnp.arange(tm)
    before = (t[:, None] < t[None, :]).astype(BF16)
    idx, wts, rank, counts = pl.pallas_call(
        _router_kernel,
        grid=(n // tm,),
        in_specs=[row, _mod_spec(tm), _full_spec((N_EXPERTS, D_MODEL)), _full_spec((tm, tm))],
        out_specs=[pair, pair, pair, pl.BlockSpec((None, N_EXPERTS, lane), lambda i: (i, 0, 0))],
        out_shape=[jax.ShapeDtypeStruct((TOP_K, n), jnp.int32),
                   jax.ShapeDtypeStruct((TOP_K, n), F32),
                   jax.ShapeDtypeStruct((TOP_K, n), jnp.int32),
                   jax.ShapeDtypeStruct((n // tm, N_EXPERTS, lane), jnp.int32)],
        compiler_params=_cparams("parallel"),
        name="moe_router",
    )(x, mods, w_router.T, before)
    return idx, wts, rank, counts[:, :, 0]


def _dispatch_plan(expert_idx, rank, counts, n_tiles):
    n = expert_idx.shape[1]
    seg_rows = (counts + F32_SUBLANES - 1) // F32_SUBLANES * F32_SUBLANES
    seg_local = jnp.cumsum(seg_rows, axis=1) - seg_rows
    start_of = jnp.repeat(seg_local.T, TM_ROUTE, axis=1)
    experts = jnp.arange(N_EXPERTS)[None, :, None]
    pos = rank + jnp.sum(jnp.where(expert_idx[:, None, :] == experts, start_of[None], 0), axis=1)
    tiles = (jnp.sum(seg_rows, axis=0) + TM_EXPERT - 1) // TM_EXPERT
    tile_end = jnp.cumsum(tiles)
    tile_start = tile_end - tiles
    seg_global = tile_start[None, :] * TM_EXPERT + jnp.cumsum(seg_rows, axis=0) - seg_rows
    tile_id = jnp.arange(n_tiles)
    tile_active = (tile_id < tile_end[-1]).astype(jnp.int32)
    last_tile = jnp.minimum(tile_id, tile_end[-1] - 1)
    tile_expert = jnp.sum((tile_end[None, :] <= last_tile[:, None]).astype(jnp.int32), axis=1)
    idle = tile_end[-1] + jnp.arange(n_tiles - (TOP_K * n) // TM_EXPERT)
    fill_tiles = jnp.concatenate([jnp.where(tiles > 0, tile_end - 1, -1), jnp.where(idle < n_tiles, idle, -1)])
    i32 = lambda a: a.astype(jnp.int32)
    return (i32(pos), i32(seg_rows.reshape(-1)), i32(seg_local.reshape(-1)),
            i32(seg_global.reshape(-1)), i32(tile_expert), tile_active, i32(fill_tiles))


def _segment_copies(rows_ref, local_ref, global_ref, tile, make_copy, wait):
    for e in range(N_EXPERTS):
        seg = tile * N_EXPERTS + e
        rows, local, glob = rows_ref[seg], local_ref[seg], global_ref[seg]
        size = TM_ROUTE
        while size >= F32_SUBLANES:
            done = rows & (-2 * size)

            def piece(size=size, done=done):
                copy = make_copy(pl.multiple_of(local + done, F32_SUBLANES),
                                 pl.multiple_of(glob + done, F32_SUBLANES), size)
                copy.wait() if wait else copy.start()

            pl.when((rows & size) != 0)(piece)
            size //= 2


def _dispatch_kernel(rows_ref, local_ref, global_ref, fill_ref, x_ref, mod_ref, pos_ref, wt_ref, xs_hbm, gate_ref,
                     sbuf_ref, zbuf_ref, sems, zsem):
    i, n_steps = pl.program_id(0), pl.num_programs(0)
    b = i % 2

    def segment_copies(tile, buf, wait):
        def make_copy(local, glob, size):
            return pltpu.make_async_copy(sbuf_ref.at[buf, pl.ds(local, size), :],
                                         xs_hbm.at[pl.ds(glob, size), :], sems.at[buf])
        _segment_copies(rows_ref, local_ref, global_ref, tile, make_copy, wait)

    @pl.when(i == 0)
    def _():
        zbuf_ref[...] = jnp.zeros_like(zbuf_ref)
        rows = zbuf_ref.shape[0]

        def fill(k):
            start = pl.multiple_of(fill_ref[k] * rows, rows)
            return pltpu.make_async_copy(zbuf_ref, xs_hbm.at[pl.ds(start, rows), :], zsem)

        for k in range(fill_ref.shape[0]):
            pl.when(fill_ref[k] >= 0)(lambda k=k: fill(k).start())
        for k in range(fill_ref.shape[0]):
            pl.when(fill_ref[k] >= 0)(lambda k=k: fill(k).wait())

    @pl.when(i >= 2)
    def _():
        segment_copies(i - 2, b, wait=True)

    hb = _modulate(x_ref[...], mod_ref, 3).astype(BF16)
    r = lax.broadcasted_iota(jnp.int32, (sbuf_ref.shape[1], hb.shape[0]), 0)
    first, second = r == pos_ref[0:1, :], r == pos_ref[1:2, :]
    sbuf_ref[b] = _dot(jnp.logical_or(first, second).astype(BF16), hb)
    segment_copies(i, b, wait=False)
    gate_ref[...] = jnp.sum(jnp.where(first, wt_ref[0:1, :], 0.0) + jnp.where(second, wt_ref[1:2, :], 0.0),
                            axis=1, keepdims=True)

    @pl.when(i == n_steps - 1)
    def _():
        segment_copies(i, b, wait=True)

    @pl.when(jnp.logical_and(i == n_steps - 1, i >= 1))
    def _():
        segment_copies(i - 1, 1 - b, wait=True)


def _dispatch(x, mods, pos_t, weights, segments, fill_tiles, n_slots):
    n = x.shape[0]
    tm = TM_ROUTE
    pair = pl.BlockSpec((TOP_K, tm), lambda i, *_: (0, i))
    return pl.pallas_call(
        _dispatch_kernel,
        grid_spec=pltpu.PrefetchScalarGridSpec(
            num_scalar_prefetch=4,
            grid=(n // tm,),
            in_specs=[pl.BlockSpec((tm, D_MODEL), lambda i, *_: (i, 0)),
                      pl.BlockSpec((None, 6, D_MODEL), lambda i, *_: ((i * tm) // GROUP_TOKENS, 0, 0)),
                      pair, pair],
            out_specs=[pl.BlockSpec(memory_space=pl.ANY),
                       pl.BlockSpec((None, LOCAL_ROWS, 1), lambda i, *_: (i, 0, 0))],
            scratch_shapes=[pltpu.VMEM((2, LOCAL_ROWS, D_MODEL), F32),
                            pltpu.VMEM((TM_EXPERT, D_MODEL), F32),
                            pltpu.SemaphoreType.DMA((2,)), pltpu.SemaphoreType.DMA(())],
        ),
        out_shape=[jax.ShapeDtypeStruct((n_slots, D_MODEL), F32),
                   jax.ShapeDtypeStruct((n // tm, LOCAL_ROWS, 1), F32)],
        compiler_params=_cparams("arbitrary"),
        name="moe_dispatch",
    )(*segments, fill_tiles, x, mods, pos_t, weights)


def _expert_kernel(te_ref, ta_ref, x_ref, w1_ref, w3_ref, w2_ref, o_ref, xb_ref, acc_ref):
    i, j = pl.program_id(0), pl.program_id(1)
    last = pl.num_programs(1) - 1
    active = ta_ref[i] == 1

    @pl.when(jnp.logical_and(j == 0, active))
    def _():
        xb_ref[...] = x_ref[...].astype(BF16)

    @pl.when(active)
    def _():
        x = xb_ref[...]
        a = _dot(x, w1_ref[...])
        b = _dot(x, w3_ref[...])
        part = _dot((a * jax.nn.sigmoid(a) * b).astype(BF16), w2_ref[...])

        @pl.when(j == 0)
        def _():
            acc_ref[...] = part

        @pl.when(j > 0)
        def _():
            acc_ref[...] += part

    @pl.when(jnp.logical_and(j == last, active))
    def _():
        o_ref[...] = acc_ref[...]

    @pl.when(jnp.logical_and(j == last, jnp.logical_not(active)))
    def _():
        o_ref[...] = jnp.zeros_like(o_ref)


def _expert_ffn(xs, tile_expert, tile_active, w13, w2, layer):
    tm, tf = TM_EXPERT, TF_EXPERT
    nf = D_FF_EXPERT // tf
    n_slots = xs.shape[0]
    row = pl.BlockSpec((tm, D_MODEL), lambda i, j, te, ta: (i, 0))
    return pl.pallas_call(
        _expert_kernel,
        grid_spec=pltpu.PrefetchScalarGridSpec(
            num_scalar_prefetch=2,
            grid=(n_slots // tm, nf),
            in_specs=[
                row,
                pl.BlockSpec((None, None, D_MODEL, tf), lambda i, j, te, ta: (layer, te[i], 0, j)),
                pl.BlockSpec((None, None, D_MODEL, tf), lambda i, j, te, ta: (layer, te[i], 0, j + nf)),
                pl.BlockSpec((None, None, tf, D_MODEL), lambda i, j, te, ta: (layer, te[i], j, 0)),
            ],
            out_specs=row,
            scratch_shapes=[pltpu.VMEM((tm, D_MODEL), BF16), pltpu.VMEM((tm, D_MODEL), F32)],
        ),
        out_shape=jax.ShapeDtypeStruct((n_slots, D_MODEL), F32),
        compiler_params=_cparams("parallel", "arbitrary"),
        name="moe_experts",
    )(tile_expert, tile_active, xs, w13, w13, w2)


def _combine_kernel(rows_ref, local_ref, global_ref, x_ref, mod_ref, pos_ref, gate_ref, y_hbm, lng_ref, lnb_ref,
                    *refs, n_ctx_tiles):
    ybuf_ref, sems = refs[-2:]
    o_refs = refs[:-2]
    i, n_steps = pl.program_id(0), pl.num_programs(0)

    def segment_copies(tile, buf, wait):
        def make_copy(local, glob, size):
            return pltpu.make_async_copy(y_hbm.at[pl.ds(glob, size), :],
                                         ybuf_ref.at[buf, pl.ds(local, size), :], sems.at[buf])
        _segment_copies(rows_ref, local_ref, global_ref, tile, make_copy, wait)

    @pl.when(i == 0)
    def _():
        ybuf_ref[...] = jnp.zeros_like(ybuf_ref)
        segment_copies(0, 0, wait=False)

    @pl.when(i + 1 < n_steps)
    def _():
        segment_copies(i + 1, (i + 1) % 2, wait=False)

    b = i % 2
    segment_copies(i, b, wait=True)
    yb = (ybuf_ref[b] * gate_ref[...]).astype(BF16)
    r = lax.broadcasted_iota(jnp.int32, (x_ref.shape[0], yb.shape[0]), 1)
    y = _dot(jnp.logical_or(r == pos_ref[:, 0:1], r == pos_ref[:, 1:2]).astype(BF16), yb)
    out = _post_norm(x_ref[...], y, mod_ref, 5, lng_ref, lnb_ref)
    if n_ctx_tiles is None:
        o_refs[0][...] = out
    else:
        is_ctx = i < n_ctx_tiles

        @pl.when(is_ctx)
        def _():
            o_refs[0][...] = out

        @pl.when(jnp.logical_not(is_ctx))
        def _():
            o_refs[1][...] = out


def _combine(x, mods, pos, segments, gates, ys, ln_g, ln_b, n_ctx_split=None):
    n = x.shape[0]
    tm = TM_ROUTE
    row = pl.BlockSpec((tm, D_MODEL), lambda i, *_: (i, 0))
    if n_ctx_split is None:
        n_ctx_tiles = None
        out_specs, out_shape = row, jax.ShapeDtypeStruct((n, D_MODEL), F32)
    else:
        n_ctx_tiles = n_ctx_split // tm
        out_specs = [pl.BlockSpec((tm, D_MODEL), lambda i, *_: (jnp.minimum(i, n_ctx_tiles - 1), 0)),
                     pl.BlockSpec((tm, D_MODEL), lambda i, *_: (jnp.maximum(i - n_ctx_tiles, 0), 0))]
        out_shape = [jax.ShapeDtypeStruct((n_ctx_split, D_MODEL), F32),
                     jax.ShapeDtypeStruct((n - n_ctx_split, D_MODEL), F32)]
    return pl.pallas_call(
        functools.partial(_combine_kernel, n_ctx_tiles=n_ctx_tiles),
        grid_spec=pltpu.PrefetchScalarGridSpec(
            num_scalar_prefetch=3,
            grid=(n // tm,),
            in_specs=[row,
                      pl.BlockSpec((None, 6, D_MODEL), lambda i, *_: ((i * tm) // GROUP_TOKENS, 0, 0)),
                      pl.BlockSpec((tm, TOP_K), lambda i, *_: (i, 0)),
                      pl.BlockSpec((None, LOCAL_ROWS, 1), lambda i, *_: (i, 0, 0)),
                      pl.BlockSpec(memory_space=pl.ANY),
                      pl.BlockSpec((1, D_MODEL), lambda i, *_: (0, 0)),
                      pl.BlockSpec((1, D_MODEL), lambda i, *_: (0, 0))],
            out_specs=out_specs,
            scratch_shapes=[pltpu.VMEM((2, LOCAL_ROWS, D_MODEL), F32), pltpu.SemaphoreType.DMA((2,))],
        ),
        out_shape=out_shape,
        compiler_params=_cparams("arbitrary"),
        name="moe_combine",
    )(*segments, x, mods, pos, gates, ys, ln_g.reshape(1, -1), ln_b.reshape(1, -1))


def _moe_layer(x, mods, w_router, w13, w2, layer, ln_g, ln_b, n_ctx_split=None):
    n = x.shape[0]
    seg_pad = (n // TM_ROUTE) * N_EXPERTS * (F32_SUBLANES - 1)
    n_tiles = -(-(TOP_K * n + seg_pad) // TM_EXPERT) + N_EXPERTS
    expert_idx, weights, rank, counts = _route(x, mods, w_router)
    pos, seg_rows, seg_local, seg_global, tile_expert, tile_active, fill_tiles = _dispatch_plan(
        expert_idx, rank, counts, n_tiles)
    segments = (seg_rows, seg_local, seg_global)
    xs, gates = _dispatch(x, mods, pos, weights, segments, fill_tiles, n_tiles * TM_EXPERT)
    ys = _expert_ffn(xs, tile_expert, tile_active, w13, w2, layer)
    return _combine(x, mods, pos.T, segments, gates, ys, ln_g, ln_b, n_ctx_split)


def kernel(x_prompt, x_sample, cache_mla_ckv, cache_mla_krope, state_gla, c, c_ctx, ada_w, ada_b, ln_g, ln_b, gmlp_w_in, gmlp_b_in, gmlp_v_g, gmlp_v_b, gmlp_w_s, gmlp_b_s, gmlp_w_out, mla_w_dqkv, mla_q_norm, mla_kv_norm, mla_w_uq, mla_w_ukv, mla_w_o, gla_w_in, gla_wa1, gla_wa2, gla_ba, gla_norm, gla_w_o, ffn_w13, ffn_w2, moe_router, moe_w13, moe_w2):
    n_ctx_seq, ctx_len, _ = x_prompt.shape
    n_lat_seq, lat_len, _ = x_sample.shape
    n_ctx = n_ctx_seq * ctx_len
    assert n_ctx == GROUP_TOKENS and lat_len == GROUP_TOKENS
    x = (x_prompt.reshape(n_ctx, D_MODEL), x_sample.reshape(-1, D_MODEL))

    cond = jnp.concatenate([c_ctx[None, :], c, jnp.zeros((ADA_ROWS - 1 - n_lat_seq, D_MODEL), F32)], axis=0)
    n_groups = 1 + n_lat_seq
    mods_all = _adaln_all(cond, ada_w, ada_b)[:, :n_groups].reshape(DEPTH, n_groups, 6, D_MODEL)

    moe_w13_b, moe_w2_b = moe_w13.astype(BF16), moe_w2.astype(BF16)

    new_ckv, new_krope, new_gla = [], [], []
    for i in range(DEPTH):
        mods = mods_all[i]
        kind, j = i % 3, i // 3
        lng, lnb = ln_g[i, 0], ln_b[i, 0]
        if kind == 0:
            x = _gmlp_layer(x, mods, gmlp_w_in[j], gmlp_b_in[j], gmlp_v_g[j], gmlp_v_b[j], gmlp_w_s[j],
                            gmlp_b_s[j], gmlp_w_out[j], lng, lnb)
        elif kind == 1:
            swap = _rope_swap_perm()
            wd = mla_w_dqkv[j]
            r0 = MLA_Q_RANK + MLA_KV_RANK
            wd = jnp.concatenate([wd, wd[:, r0:][:, swap]], axis=1).astype(BF16)
            wuq = mla_w_uq[j].reshape(MLA_Q_RANK, MLA_HEADS, MLA_D_QK)
            wuq_rope = wuq[:, :, MLA_D_NOPE:]
            wuq = jnp.concatenate([wuq[:, :, :MLA_D_NOPE].reshape(MLA_Q_RANK, -1),
                                   wuq_rope.reshape(MLA_Q_RANK, -1),
                                   wuq_rope[:, :, swap].reshape(MLA_Q_RANK, -1)], axis=1).astype(BF16)
            wukv = mla_w_ukv[j].reshape(MLA_KV_RANK, MLA_HEADS, MLA_D_NOPE + MLA_D_V)
            wukv = jnp.concatenate([wukv[:, :, :MLA_D_NOPE].reshape(MLA_KV_RANK, -1),
                                    wukv[:, :, MLA_D_NOPE:].reshape(MLA_KV_RANK, -1)], axis=1).astype(BF16)
            weights = (wd, mla_q_norm[j].reshape(1, -1), mla_kv_norm[j].reshape(1, -1), wuq, wukv)
            cos, sin = _rope_tables(lat_len)
            tables = (jnp.tile(cos, (1, MLA_HEADS)), jnp.tile(sin, (1, MLA_HEADS)), cos, sin)

            qc, kc, vc, ckv, krope = _mla_project(x, 0, n_ctx_seq, ctx_len, mods, weights, None)
            new_ckv.append(ckv.reshape(n_ctx_seq, ctx_len, MLA_KV_RANK))
            new_krope.append(krope.reshape(n_ctx_seq, ctx_len, MLA_D_ROPE))
            ql, kl, vl, _, _ = _mla_project(x, n_ctx, n_lat_seq, lat_len, mods, weights, tables)
            k_cache, v_cache = _mla_cache_keys(cache_mla_ckv[:, j], cache_mla_krope[:, j], wukv)
            a_ctx = _attention(qc, [kc], [vc]).reshape(n_ctx, -1)
            a_lat = _attention(ql, [k_cache, kl], [v_cache, vl]).reshape(n_lat_seq * lat_len, -1)
            x = _out_proj_layer(x, mods, a_ctx, a_lat, mla_w_o[j].astype(BF16), lng, lnb)
        else:
            qkvr, g = _gla_project(x, mods, gla_w_in[j], gla_wa1[j], gla_wa2[j], gla_ba[j])
            zero_state = jnp.zeros((n_ctx_seq, 2, GLA_HEADS, GLA_DV, GLA_DK), F32)
            lat_state = jnp.swapaxes(state_gla[:, j], -1, -2)
            ofc, obc, s_ctx = _gla_scan(qkvr, g, 0, n_ctx_seq, ctx_len, zero_state)
            ofl, obl, _ = _gla_scan(qkvr, g, n_ctx, n_lat_seq, lat_len, lat_state)
            new_gla.append(jnp.swapaxes(s_ctx, -1, -2))
            x = _gla_out_layer(x, mods, (ofc, ofl), (obc, obl), qkvr, gla_norm[j], gla_w_o[j], lng, lnb)

        lng, lnb = ln_g[i, 1], ln_b[i, 1]
        f = i // 2
        if i % 2 == 0:
            x = _ffn_layer(x, mods, ffn_w13[f].astype(BF16), ffn_w2[f].astype(BF16), lng, lnb)
        else:
            split = n_ctx if i == DEPTH - 1 else None
            x = _moe_layer(x, mods, moe_router[f], moe_w13_b, moe_w2_b, f, lng, lnb, split)

    y_prompt = x[0].reshape(n_ctx_seq, ctx_len, D_MODEL)
    y_sample = x[1].reshape(n_lat_seq, lat_len, D_MODEL)
    return (y_prompt, y_sample, jnp.stack(new_ckv, axis=1), jnp.stack(new_krope, axis=1),
            jnp.stack(new_gla, axis=1))
```
